```python
import math
import jax, jax.numpy as jnp
from jax import lax
import numpy as np

D_MODEL = 1024
BATCH = 8
SEQ = 4096
DEPTH = 2

N_GROUPS = 4
MIX_HEADS = 4
HEAD_DIM = D_MODEL // (N_GROUPS * MIX_HEADS)
GROUP_WIDTH = MIX_HEADS * HEAD_DIM
MIX_WIDTH = N_GROUPS * GROUP_WIDTH
Q_BLOCK = 128
MLSTM_CHUNK = 64
MLSTM_CONV = 4
NSA_CMP_BLOCK = 32
NSA_CMP_STRIDE = 16
NSA_SEL_BLOCK = 64
NSA_N_SEL = 16
NSA_WINDOW = 512
NUM_BUCKETS = 32
MAX_DISTANCE = 128
D_FF = 2816
N_EXPERTS = 8
TOP_K = 2
D_FF_EXPERT = 3584
MOE_ROW_BLOCK = 128
NORM_EPS = 1e-6
MASK_VALUE = 1e30
N_DENSE = (DEPTH + 1) // 2
N_MOE = DEPTH // 2
IN_WIDTHS = (2 * GROUP_WIDTH, GROUP_WIDTH, GROUP_WIDTH, 2 * MIX_HEADS,
             GROUP_WIDTH, GROUP_WIDTH, GROUP_WIDTH, MIX_HEADS,
             GROUP_WIDTH, GROUP_WIDTH, GROUP_WIDTH,
             GROUP_WIDTH, 6 * HEAD_DIM, 3 * MIX_HEADS)
IN_WIDTH = sum(IN_WIDTHS)

kernel_name = "hybrid_parallel_heads_mlstm_fox_sb_nsa_moe"


def rmsnorm(x, g):
    xf = x.astype(jnp.float32)
    y = xf * lax.rsqrt(jnp.mean(xf * xf, axis=-1, keepdims=True) + NORM_EPS)
    return (y * g.astype(jnp.float32)).astype(x.dtype)


def masked_softmax(s, valid):
    p = jax.nn.softmax(jnp.where(valid, s, -MASK_VALUE), axis=-1)
    return jnp.where(valid, p, 0.0)


def t5_bias(dist, table):
    n = jnp.maximum(dist, 0)
    max_exact = NUM_BUCKETS // 2
    large = max_exact + (jnp.log(jnp.maximum(n, 1).astype(jnp.float32) / max_exact)
                         / math.log(MAX_DISTANCE / max_exact) * (NUM_BUCKETS - max_exact)).astype(jnp.int32)
    bucket = jnp.where(n < max_exact, n, jnp.minimum(large, NUM_BUCKETS - 1))
    return table[bucket].astype(jnp.float32)


def block_sweep(fn, seq):
    out = lax.map(fn, jnp.arange(seq // Q_BLOCK))
    n, b, qb, h, d = out.shape
    return jnp.moveaxis(out, 0, 1).reshape(b, n * qb, h, d)


def causal_conv(x, w):
    k, s = w.shape[0], x.shape[1]
    xp = jnp.pad(x, ((0, 0), (k - 1, 0), (0, 0)))
    return sum(w[j] * xp[:, k - 1 - j:k - 1 - j + s] for j in range(k))


def mlstm_chunkwise(q, k, v, ig, fg):
    B, S, H, Dh = q.shape
    L = MLSTM_CHUNK
    nc = S // L

    def chunks(a):
        return jnp.moveaxis(a.reshape(B, nc, L, H, *a.shape[3:]), 3, 1)

    q, k, v, ig = chunks(q), chunks(k), chunks(v), chunks(ig)
    b = jnp.cumsum(jax.nn.log_sigmoid(chunks(fg)), axis=-1)
    g = b[..., -1]
    w = g[..., None] - b + ig
    m_loc = jnp.max(w, axis=-1)
    ew = jnp.exp(w - m_loc[..., None])
    c_chunk = jnp.einsum('bhcl,bhcld,bhcle->bhcde', ew, v, k)
    n_chunk = jnp.einsum('bhcl,bhcle->bhce', ew, k)

    def step(carry, xs):
        C, n, m = carry
        c_c, n_c, ml_c, g_c = xs
        m_new = jnp.maximum(g_c + m, ml_c)
        a = jnp.exp(g_c + m - m_new)
        bb = jnp.exp(ml_c - m_new)
        C_new = a[..., None, None] * C + bb[..., None, None] * c_c
        n_new = a[..., None] * n + bb[..., None] * n_c
        return (C_new, n_new, m_new), (C, n, m)

    init = (jnp.zeros((B, H, Dh, Dh), jnp.float32), jnp.zeros((B, H, Dh), jnp.float32),
            jnp.zeros((B, H), jnp.float32))
    xs = tuple(jnp.moveaxis(a, 2, 0) for a in (c_chunk, n_chunk, m_loc, g))
    _, (C_prev, n_prev, m_prev) = lax.scan(step, init, xs)
    C_prev = jnp.moveaxis(C_prev, 0, 2)
    n_prev = jnp.moveaxis(n_prev, 0, 2)
    m_prev = jnp.moveaxis(m_prev, 0, 2)

    tri = jnp.tril(jnp.ones((L, L), bool))
    d_log = jnp.where(tri, b[..., :, None] - b[..., None, :] + ig[..., None, :], -MASK_VALUE)
    m_inter = b + m_prev[..., None]
    m_t = jnp.maximum(jnp.max(d_log, axis=-1), m_inter)
    s = jnp.einsum('bhcld,bhcsd->bhcls', q, k) * jnp.exp(d_log - m_t[..., None])
    inter_w = jnp.exp(m_inter - m_t)
    num = jnp.einsum('bhcls,bhcsd->bhcld', s, v) + inter_w[..., None] * jnp.einsum('bhcde,bhcle->bhcld', C_prev, q)
    den = jnp.sum(s, axis=-1) + inter_w * jnp.einsum('bhce,bhcle->bhcl', n_prev, q)
    h = num / jnp.maximum(jnp.abs(den), jnp.exp(-m_t))[..., None]
    return jnp.moveaxis(h, 1, 3).reshape(B, S, H, Dh)


def mlstm_mixer(qk_raw, v_raw, o_pre, if_pre, conv_w, gate_b, out_g):
    B, S, _ = v_raw.shape
    H, Dh = MIX_HEADS, HEAD_DIM
    qk = jax.nn.silu(causal_conv(qk_raw, conv_w))
    q, k = jnp.split(qk, 2, axis=-1)
    q = q.reshape(B, S, H, Dh).astype(jnp.float32)
    k = k.reshape(B, S, H, Dh).astype(jnp.float32) * (Dh ** -0.5)
    v = v_raw.reshape(B, S, H, Dh).astype(jnp.float32)
    gates = if_pre.astype(jnp.float32) + gate_b.astype(jnp.float32)
    h = mlstm_chunkwise(q, k, v, gates[..., :H], gates[..., H:])
    h = rmsnorm(h, out_g.reshape(H, Dh)).reshape(B, S, GROUP_WIDTH)
    return (h * jax.nn.sigmoid(o_pre.astype(jnp.float32))).astype(v_raw.dtype)


def fox_mixer(q, k, v, f_pre, f_b, q_g, k_g):
    B, S, _ = q.shape
    H, Dh = MIX_HEADS, HEAD_DIM
    q = rmsnorm(q.reshape(B, S, H, Dh), q_g)
    k = rmsnorm(k.reshape(B, S, H, Dh), k_g)
    v = v.reshape(B, S, H, Dh)
    F = jnp.cumsum(jax.nn.log_sigmoid(f_pre.astype(jnp.float32) + f_b.astype(jnp.float32)), axis=1)
    F = jnp.transpose(F, (0, 2, 1))
    kpos = jnp.arange(S)
    scale = Dh ** -0.5

    def blk(i):
        t0 = i * Q_BLOCK
        t = t0 + jnp.arange(Q_BLOCK)
        qb = lax.dynamic_slice_in_dim(q, t0, Q_BLOCK, axis=1)
        Fq = lax.dynamic_slice_in_dim(F, t0, Q_BLOCK, axis=2)
        s = jnp.einsum('bthd,bshd->bhts', qb, k).astype(jnp.float32) * scale + (Fq[..., :, None] - F[..., None, :])
        p = masked_softmax(s, kpos[None, :] <= t[:, None])
        return jnp.einsum('bhts,bshd->bthd', p.astype(v.dtype), v)

    return block_sweep(blk, S).reshape(B, S, GROUP_WIDTH)


def stick_breaking_mixer(q, k, v):
    B, S, _ = q.shape
    H, Dh = MIX_HEADS, HEAD_DIM
    q = q.reshape(B, S, H, Dh)
    k = k.reshape(B, S, H, Dh)
    v = v.reshape(B, S, H, Dh)
    kpos = jnp.arange(S)
    scale = Dh ** -0.5

    def blk(i):
        t0 = i * Q_BLOCK
        t = t0 + jnp.arange(Q_BLOCK)
        qb = lax.dynamic_slice_in_dim(q, t0, Q_BLOCK, axis=1)
        z = jnp.einsum('bthd,bshd->bhts', qb, k).astype(jnp.float32) * scale
        strict = kpos[None, :] < t[:, None]
        log_rest = jnp.where(strict, jax.nn.log_sigmoid(-z), 0.0)
        suffix = lax.cumsum(log_rest, axis=3, reverse=True) - log_rest
        a = jnp.where(strict, jnp.exp(jax.nn.log_sigmoid(z) + suffix), 0.0)
        return jnp.einsum('bhts,bshd->bthd', a.astype(v.dtype), v)

    return block_sweep(blk, S).reshape(B, S, GROUP_WIDTH)


def nsa_mixer(q, kv, g_pre, rel_bias, q_g, k_g, cmp_pos, cmp_wk, cmp_wv, gate_b):
    B, S, _ = q.shape
    H, Dh = MIX_HEADS, HEAD_DIM
    q = rmsnorm(q.reshape(B, S, H, Dh), q_g)
    kc_tok, vc_tok, k_sel, v_sel, k_win, v_win = jnp.split(kv, 6, axis=-1)
    k_sel = rmsnorm(k_sel, k_g[1])
    k_win = rmsnorm(k_win, k_g[2])
    n_cmp = (S - NSA_CMP_BLOCK) // NSA_CMP_STRIDE + 1
    cmp_start = jnp.arange(n_cmp) * NSA_CMP_STRIDE
    cmp_idx = cmp_start[:, None] + jnp.arange(NSA_CMP_BLOCK)[None, :]

    def compress(tok, w):
        return (tok[:, cmp_idx] + cmp_pos).reshape(B, n_cmp, NSA_CMP_BLOCK * Dh) @ w

    k_cmp = rmsnorm(compress(kc_tok, cmp_wk), k_g[0])
    v_cmp = compress(vc_tok, cmp_wv)
    cmp_end = cmp_start + NSA_CMP_BLOCK - 1
    n_sel = S // NSA_SEL_BLOCK
    k_top = min(NSA_N_SEL, n_sel)
    sel_start = jnp.arange(n_sel) * NSA_SEL_BLOCK
    overlap = jnp.clip(jnp.minimum(cmp_end[:, None] + 1, sel_start[None, :] + NSA_SEL_BLOCK)
                       - jnp.maximum(cmp_start[:, None], sel_start[None, :]), 0).astype(jnp.float32) / NSA_CMP_BLOCK
    k_win_p = jnp.pad(k_win, ((0, 0), (NSA_WINDOW, 0), (0, 0)))
    v_win_p = jnp.pad(v_win, ((0, 0), (NSA_WINDOW, 0), (0, 0)))
    gates = jax.nn.sigmoid(g_pre.astype(jnp.float32) + gate_b.astype(jnp.float32)).reshape(B, S, H, 3)
    scale = Dh ** -0.5
    gather = jax.vmap(lambda a, idx: a[idx])
    blk_id = jnp.arange(n_sel)

    def blk(i):
        t0 = i * Q_BLOCK
        t = t0 + jnp.arange(Q_BLOCK)
        qb = lax.dynamic_slice_in_dim(q, t0, Q_BLOCK, axis=1)
        s_c = (jnp.einsum('bthd,bnd->bhtn', qb, k_cmp).astype(jnp.float32) * scale
               + jnp.transpose(t5_bias(t[:, None] - cmp_end[None, :], rel_bias), (2, 0, 1)))
        p_c = masked_softmax(s_c, cmp_end[None, :] <= t[:, None])
        o_cmp = jnp.einsum('bhtn,bnd->bthd', p_c.astype(v_cmp.dtype), v_cmp)
        imp = jnp.einsum('bhtn,nj->btj', p_c, overlap)
        cur = t // NSA_SEL_BLOCK
        forced = (blk_id[None, :] == 0) | (blk_id[None, :] == cur[:, None]) | (blk_id[None, :] == cur[:, None] - 1)
        score = jnp.where(forced, MASK_VALUE, jnp.where(sel_start[None, :] <= t[:, None], imp, -MASK_VALUE))
        _, top = lax.top_k(score, k_top)
        tok = (top[..., None] * NSA_SEL_BLOCK + jnp.arange(NSA_SEL_BLOCK)).reshape(B, Q_BLOCK, k_top * NSA_SEL_BLOCK)
        ks = gather(k_sel, tok)
        vs = gather(v_sel, tok)
        dist_s = t[None, :, None] - tok
        s_s = (jnp.einsum('bthd,btkd->bhtk', qb, ks).astype(jnp.float32) * scale
               + jnp.transpose(t5_bias(dist_s, rel_bias), (0, 3, 1, 2)))
        p_s = masked_softmax(s_s, (dist_s >= 0)[:, None])
        o_sel = jnp.einsum('bhtk,btkd->bthd', p_s.astype(vs.dtype), vs)
        kw = lax.dynamic_slice_in_dim(k_win_p, t0, Q_BLOCK + NSA_WINDOW, axis=1)
        vw = lax.dynamic_slice_in_dim(v_win_p, t0, Q_BLOCK + NSA_WINDOW, axis=1)
        spos = t0 - NSA_WINDOW + jnp.arange(Q_BLOCK + NSA_WINDOW)
        dist_w = t[:, None] - spos[None, :]
        valid_w = (dist_w >= 0) & (dist_w < NSA_WINDOW) & (spos[None, :] >= 0)
        s_w = (jnp.einsum('bthd,bsd->bhts', qb, kw).astype(jnp.float32) * scale
               + jnp.transpose(t5_bias(dist_w, rel_bias), (2, 0, 1)))
        p_w = masked_softmax(s_w, valid_w)
        o_win = jnp.einsum('bhts,bsd->bthd', p_w.astype(vw.dtype), vw)
        g = lax.dynamic_slice_in_dim(gates, t0, Q_BLOCK, axis=1)
        return (g[..., 0:1] * o_cmp + g[..., 1:2] * o_sel + g[..., 2:3] * o_win).astype(q.dtype)

    return block_sweep(blk, S).reshape(B, S, GROUP_WIDTH)


def swiglu(h, wg, wu, wd):
    return (jax.nn.silu(h @ wg) * (h @ wu)) @ wd


def moe_swiglu(h, router_w, router_b, wg, wu, wd):
    B, S, D = h.shape
    T = B * S
    xf = h.reshape(T, D)
    logits = (xf @ router_w).astype(jnp.float32) + router_b.astype(jnp.float32)
    top_logit, top_e = lax.top_k(logits, TOP_K)
    top_w = jax.nn.softmax(top_logit, axis=-1)
    n_assign = T * TOP_K
    e_flat = top_e.reshape(-1)
    tok_flat = jnp.repeat(jnp.arange(T, dtype=jnp.int32), TOP_K)
    w_flat = top_w.reshape(-1)
    order = jnp.argsort(e_flat)
    e_s = e_flat[order]
    counts = jnp.bincount(e_flat, length=N_EXPERTS)
    padded = (counts + MOE_ROW_BLOCK - 1) // MOE_ROW_BLOCK * MOE_ROW_BLOCK
    start = jnp.cumsum(counts) - counts
    pstart = jnp.cumsum(padded) - padded
    dest = pstart[e_s] + jnp.arange(n_assign) - start[e_s]
    n_blocks = -(-(n_assign + N_EXPERTS * MOE_ROW_BLOCK) // MOE_ROW_BLOCK)
    n_rows = n_blocks * MOE_ROW_BLOCK
    row_tok = jnp.zeros((n_rows,), jnp.int32).at[dest].set(tok_flat[order])
    row_w = jnp.zeros((n_rows,), jnp.float32).at[dest].set(w_flat[order])
    blk_expert = jnp.minimum(jnp.searchsorted(jnp.cumsum(padded), jnp.arange(n_blocks) * MOE_ROW_BLOCK,
                                              side='right'), N_EXPERTS - 1)

    def expert_block(args):
        tok_b, e = args
        xb = xf[tok_b]
        return (jax.nn.silu(xb @ wg[e]) * (xb @ wu[e])) @ wd[e]

    y = lax.map(expert_block, (row_tok.reshape(n_blocks, MOE_ROW_BLOCK), blk_expert)).reshape(n_rows, D)
    out = jnp.zeros((T, D), y.dtype).at[row_tok].add(y * row_w[:, None].astype(y.dtype))
    return out.reshape(B, S, D).astype(h.dtype)


def setup_inputs(seed: int = 0) -> dict:
    key = jax.random.key(seed)
    ks = iter(jax.random.split(key, 40))
    f32 = jnp.float32

    def nrm(shape, scale):
        return jax.random.normal(next(ks), shape, f32) * scale

    def gain(shape):
        return 1.0 + nrm(shape, 0.05)

    D, H, Dh = D_MODEL, MIX_HEADS, HEAD_DIM
    return {
        "x": nrm((BATCH, SEQ, D), 1.0),
        "c": nrm((BATCH, D), 1.0),
        "rel_bias": nrm((NUM_BUCKETS, H), 0.5),
        "ada_w": nrm((DEPTH, D, 6 * D), 0.5 * D ** -0.5),
        "ada_b": nrm((DEPTH, 6 * D), 0.02),
        "norm1_g": gain((DEPTH, D)),
        "norm2_g": gain((DEPTH, D)),
        "w_in": nrm((DEPTH, D, IN_WIDTH), D ** -0.5),
        "w_out": nrm((DEPTH, MIX_WIDTH, D), MIX_WIDTH ** -0.5),
        "mlstm_conv_w": nrm((DEPTH, MLSTM_CONV, 2 * GROUP_WIDTH), MLSTM_CONV ** -0.5),
        "mlstm_gate_b": jnp.concatenate([nrm((DEPTH, H), 0.1), 3.0 + nrm((DEPTH, H), 0.5)], axis=-1),
        "mlstm_out_g": gain((DEPTH, GROUP_WIDTH)),
        "fox_f_b": 3.0 + nrm((DEPTH, H), 0.5),
        "fox_q_g": gain((DEPTH, Dh)),
        "fox_k_g": gain((DEPTH, Dh)),
        "nsa_q_g": gain((DEPTH, Dh)),
        "nsa_k_g": gain((DEPTH, 3, Dh)),
        "nsa_cmp_pos": nrm((DEPTH, NSA_CMP_BLOCK, Dh), 0.1),
        "nsa_cmp_wk": nrm((DEPTH, NSA_CMP_BLOCK * Dh, Dh), (NSA_CMP_BLOCK * Dh) ** -0.5),
        "nsa_cmp_wv": nrm((DEPTH, NSA_CMP_BLOCK * Dh, Dh), (NSA_CMP_BLOCK * Dh) ** -0.5),
        "nsa_gate_b": nrm((DEPTH, 3 * H), 0.1),
        "ffn_wg": nrm((N_DENSE, D, D_FF), D ** -0.5),
        "ffn_wu": nrm((N_DENSE, D, D_FF), D ** -0.5),
        "ffn_wd": nrm((N_DENSE, D_FF, D), D_FF ** -0.5),
        "moe_router_w": nrm((N_MOE, D, N_EXPERTS), D ** -0.5),
        "moe_router_b": nrm((N_MOE, N_EXPERTS), 0.01),
        "moe_wg": nrm((N_MOE, N_EXPERTS, D, D_FF_EXPERT), D ** -0.5),
        "moe_wu": nrm((N_MOE, N_EXPERTS, D, D_FF_EXPERT), D ** -0.5),
        "moe_wd": nrm((N_MOE, N_EXPERTS, D_FF_EXPERT, D), D_FF_EXPERT ** -0.5),
    }


def reference(x, c, rel_bias, ada_w, ada_b, norm1_g, norm2_g, w_in, w_out,
              mlstm_conv_w, mlstm_gate_b, mlstm_out_g, fox_f_b, fox_q_g, fox_k_g,
              nsa_q_g, nsa_k_g, nsa_cmp_pos, nsa_cmp_wk, nsa_cmp_wv, nsa_gate_b,
              ffn_wg, ffn_wu, ffn_wd, moe_router_w, moe_router_b, moe_wg, moe_wu, moe_wd):
    split_at = [int(v) for v in np.cumsum(IN_WIDTHS)[:-1]]
    cond = jax.nn.silu(c)
    for l in range(DEPTH):
        mod = cond @ ada_w[l] + ada_b[l]
        sh1, sc1, g1, sh2, sc2, g2 = [m[:, None, :] for m in jnp.split(mod, 6, axis=-1)]
        h = rmsnorm(x, norm1_g[l]) * (1.0 + sc1) + sh1
        (m_qk, m_v, m_o, m_if, f_q, f_k, f_v, f_f,
         s_q, s_k, s_v, n_q, n_kv, n_g) = jnp.split(h @ w_in[l], split_at, axis=-1)
        y_a = mlstm_mixer(m_qk, m_v, m_o, m_if, mlstm_conv_w[l], mlstm_gate_b[l], mlstm_out_g[l])
        y_b = fox_mixer(f_q, f_k, f_v, f_f, fox_f_b[l], fox_q_g[l], fox_k_g[l])
        y_c = stick_breaking_mixer(s_q, s_k, s_v)
        y_d = nsa_mixer(n_q, n_kv, n_g, rel_bias, nsa_q_g[l], nsa_k_g[l], nsa_cmp_pos[l],
                        nsa_cmp_wk[l], nsa_cmp_wv[l], nsa_gate_b[l])
        mix = jnp.concatenate([y_a, y_b, y_c, y_d], axis=-1)
        x = x + g1 * (mix @ w_out[l])
        h = rmsnorm(x, norm2_g[l]) * (1.0 + sc2) + sh2
        if l % 2 == 0:
            f = swiglu(h, ffn_wg[l // 2], ffn_wu[l // 2], ffn_wd[l // 2])
        else:
            f = moe_swiglu(h, moe_router_w[l // 2], moe_router_b[l // 2], moe_wg[l // 2], moe_wu[l // 2], moe_wd[l // 2])
        x = x + g2 * f
    return x
```

```python
import functools
import math

import numpy as np
import jax
import jax.numpy as jnp
from jax import lax
from jax.experimental import pallas as pl
from jax.experimental.pallas import tpu as pltpu

F32 = jnp.float32
BF16 = jnp.bfloat16
HI = lax.Precision.HIGHEST

D_MODEL = 1024
HEADS = 4
HEAD_DIM = 64
GROUP = HEADS * HEAD_DIM
NORM_EPS = 1e-6
NEG = -1e30
QK_SCALE = HEAD_DIM ** -0.5
CMP_BLOCK = 32
CMP_STRIDE = 16
SEL_BLOCK = 64
N_SEL_TOP = 16
WINDOW = 512
NUM_BUCKETS = 32
MAX_DISTANCE = 128
N_EXPERTS = 8
VMEM_LIMIT = 56 * 1024 * 1024

PW = 3328
C_MQK, C_MV, C_MO = 0, 512, 768
C_FQ, C_FK, C_FV = 1024, 1280, 1536
C_SQ, C_SK, C_SV = 1792, 2048, 2304
C_NQ, C_NKV, C_GATES = 2560, 2816, 3200
G_MI, G_MF, G_FF, G_NG = 0, 4, 8, 12
GT_ROWS = 32

MLSTM_TS = 512
MLSTM_L = 128
ATT_T = 256
NSA_TQ = 256
NSA_NEAR = WINDOW + NSA_TQ
MOE_BM = 512
MOE_TF = 512
FFN_TM = 512
FFN_TF = 1408


def _cp(sem, vmem=VMEM_LIMIT):
    return pltpu.CompilerParams(dimension_semantics=sem, vmem_limit_bytes=vmem)


def _dot(a, b):
    return jnp.dot(a.astype(BF16), b.astype(BF16), preferred_element_type=F32)


def _dot_t(a, b):
    return lax.dot_general(a.astype(BF16), b.astype(BF16), (((1,), (1,)), ((), ())), preferred_element_type=F32)


def _dot_hi(a, b):
    return jnp.dot(a, b, precision=HI, preferred_element_type=F32)


def _dot_t_hi(a, b):
    return lax.dot_general(a, b, (((1,), (1,)), ((), ())), precision=HI, preferred_element_type=F32)


def _sigmoid(x):
    return 1.0 / (1.0 + jnp.exp(-x))


def _log_sigmoid(x):
    return jnp.minimum(x, 0.0) - jnp.log1p(jnp.exp(-jnp.abs(x)))


def _rms_lanes(x, g):
    return x * lax.rsqrt(jnp.mean(x * x, axis=-1, keepdims=True) + NORM_EPS) * g


def _rms_heads(x, bd, g):
    return x * lax.rsqrt(_dot_hi(x * x, bd) + NORM_EPS) * g


def _adaln_kernel(c_ref, w_ref, b_ref, o_ref):
    c = c_ref[...]
    o_ref[...] = _dot_hi(c * _sigmoid(c), w_ref[...]) + b_ref[...]


def _adaln(c, ada_w, ada_b):
    depth, d, six_d = ada_w.shape
    nb = c.shape[0]
    out = pl.pallas_call(
        _adaln_kernel,
        out_shape=jax.ShapeDtypeStruct((depth, nb, six_d), F32),
        grid=(depth, six_d // d),
        in_specs=[pl.BlockSpec((nb, d), lambda l, j: (0, 0)),
                  pl.BlockSpec((None, d, d), lambda l, j: (l, 0, j)),
                  pl.BlockSpec((None, 1, d), lambda l, j: (l, 0, j))],
        out_specs=pl.BlockSpec((None, nb, d), lambda l, j: (l, 0, j)),
        compiler_params=_cp(("arbitrary", "arbitrary")),
        name="adaln",
    )(c, ada_w, ada_b.reshape(depth, 1, six_d))
    return out.reshape(depth, nb, 6, d)


def _inproj_kernel(x_ref, mod_ref, g_ref, w_ref, wgt_ref, o_ref, gt_ref):
    h = _rms_lanes(x_ref[...], g_ref[...]) * (1.0 + mod_ref[1:2, :]) + mod_ref[0:1, :]
    hb = h.astype(BF16)
    o_ref[...] = jnp.dot(hb, w_ref[...], preferred_element_type=F32)
    gt_ref[...] = lax.dot_general(wgt_ref[...], hb, (((1,), (1,)), ((), ())), preferred_element_type=F32)


def _inproj(x, mod, l, norm_g, w_p, wgt):
    nb, s, d = x.shape
    tm = 256
    return pl.pallas_call(
        _inproj_kernel,
        out_shape=(jax.ShapeDtypeStruct((nb, s, PW), F32), jax.ShapeDtypeStruct((nb, GT_ROWS, s), F32)),
        grid=(nb, s // tm),
        in_specs=[pl.BlockSpec((None, tm, d), lambda b, i: (b, i, 0)),
                  pl.BlockSpec((None, None, 6, d), lambda b, i: (l, b, 0, 0)),
                  pl.BlockSpec((None, 1, d), lambda b, i: (l, 0, 0)),
                  pl.BlockSpec((None, d, PW), lambda b, i: (l, 0, 0)),
                  pl.BlockSpec((None, GT_ROWS, d), lambda b, i: (l, 0, 0))],
        out_specs=(pl.BlockSpec((None, tm, PW), lambda b, i: (b, i, 0)),
                   pl.BlockSpec((None, GT_ROWS, tm), lambda b, i: (b, 0, i))),
        compiler_params=_cp(("arbitrary", "arbitrary")),
        name="inproj",
    )(x, mod, norm_g, w_p, wgt)


def _mlstm_kernel(qk_ref, v_ref, op_ref, gc_ref, gr_ref, cw_ref, bc_ref, br_ref, og_ref, tri_ref, y_ref,
                  prev_ref, qs_ref, ks_ref, c_ref, n_ref, m_ref, *, ts, cl):
    @pl.when(pl.program_id(1) == 0)
    def _():
        prev_ref[...] = jnp.zeros_like(prev_ref)
        c_ref[...] = jnp.zeros_like(c_ref)
        n_ref[...] = jnp.zeros_like(n_ref)
        m_ref[...] = jnp.zeros_like(m_ref)

    x = qk_ref[...]
    xc = jnp.concatenate([prev_ref[...], x], axis=0)
    cw = cw_ref[...]
    y = (cw[0:1] * xc[8:8 + ts] + cw[1:2] * xc[7:7 + ts] + cw[2:3] * xc[6:6 + ts] + cw[3:4] * xc[5:5 + ts])
    prev_ref[...] = x[ts - 8:ts]
    y = y * _sigmoid(y)
    qs_ref[...] = y[:, :GROUP]
    ks_ref[...] = y[:, GROUP:] * QK_SCALE

    tri = tri_ref[...]
    lower = lax.broadcasted_iota(jnp.int32, (cl, cl), 0) >= lax.broadcasted_iota(jnp.int32, (cl, cl), 1)

    def chunk(ci, carry):
        r0 = pl.multiple_of(ci * cl, cl)
        gc = gc_ref[pl.ds(r0, cl), :] + bc_ref[...]
        gr = gr_ref[0:8, pl.ds(r0, cl)] + br_ref[0:8, :]
        b_c = _dot_hi(tri, _log_sigmoid(gc))
        b_r = _dot_t_hi(_log_sigmoid(gr), tri)
        q = qs_ref[pl.ds(r0, cl), :]
        k = ks_ref[pl.ds(r0, cl), :]
        v = v_ref[pl.ds(r0, cl), :]
        outs = []
        for h in range(HEADS):
            sl = slice(h * HEAD_DIM, (h + 1) * HEAD_DIM)
            ig_col, b_col = gc[:, G_MI + h:G_MI + h + 1], b_c[:, G_MF + h:G_MF + h + 1]
            ig_row, b_row = gr[G_MI + h:G_MI + h + 1, :], b_r[G_MF + h:G_MF + h + 1, :]
            g = b_row[:, cl - 1:cl]
            m_loc = jnp.max(g - b_row + ig_row, axis=1, keepdims=True)
            ew_col = jnp.exp(g - b_col + ig_col - m_loc)
            qh, kh, vh = q[:, sl], k[:, sl], v[:, sl]
            c_chunk = lax.dot_general((vh * ew_col).astype(BF16), kh.astype(BF16), (((0,), (0,)), ((), ())),
                                      preferred_element_type=F32)
            n_chunk = jnp.sum(ew_col * kh, axis=0, keepdims=True)
            c_prev, n_prev, m_prev = c_ref[h], n_ref[h], m_ref[h][:, 0:1]
            d_log = jnp.where(lower, b_col - b_row + ig_row, NEG)
            m_inter = b_col + m_prev
            m_t = jnp.maximum(jnp.max(d_log, axis=1, keepdims=True), m_inter)
            s = _dot_t(qh, kh) * jnp.exp(d_log - m_t)
            inter_w = jnp.exp(m_inter - m_t)
            num = _dot(s, vh) + inter_w * _dot_t(qh, c_prev)
            den = jnp.sum(s, axis=1, keepdims=True) + inter_w * jnp.sum(qh * n_prev, axis=1, keepdims=True)
            hh = num / jnp.maximum(jnp.abs(den), jnp.exp(-m_t))
            m_new = jnp.maximum(g + m_prev, m_loc)
            a = jnp.exp(g + m_prev - m_new)
            bb = jnp.exp(m_loc - m_new)
            c_ref[h] = a * c_prev + bb * c_chunk
            n_ref[h] = a * n_prev + bb * n_chunk
            m_ref[h] = jnp.broadcast_to(m_new, (1, 128))
            outs.append(_rms_lanes(hh, og_ref[:, sl]))
        hcat = jnp.concatenate(outs, axis=1)
        y_ref[pl.ds(r0, cl), :] = (hcat * _sigmoid(op_ref[pl.ds(r0, cl), :])).astype(y_ref.dtype)
        return carry

    lax.fori_loop(0, ts // cl, chunk, 0)


def _mlstm(proj, gt, l, conv_w, bias_c, bias_r, out_g, tri):
    nb, s, _ = proj.shape
    ts, cl = MLSTM_TS, MLSTM_L
    kern = functools.partial(_mlstm_kernel, ts=ts, cl=cl)
    return pl.pallas_call(
        kern,
        out_shape=jax.ShapeDtypeStruct((nb, s, GROUP), BF16),
        grid=(nb, s // ts),
        in_specs=[pl.BlockSpec((None, ts, 2 * GROUP), lambda b, i: (b, i, C_MQK // (2 * GROUP))),
                  pl.BlockSpec((None, ts, GROUP), lambda b, i: (b, i, C_MV // GROUP)),
                  pl.BlockSpec((None, ts, GROUP), lambda b, i: (b, i, C_MO // GROUP)),
                  pl.BlockSpec((None, ts, 128), lambda b, i: (b, i, C_GATES // 128)),
                  pl.BlockSpec((None, GT_ROWS, ts), lambda b, i: (b, 0, i)),
                  pl.BlockSpec((None, 4, 2 * GROUP), lambda b, i: (l, 0, 0)),
                  pl.BlockSpec((None, 1, 128), lambda b, i: (l, 0, 0)),
                  pl.BlockSpec((None, GT_ROWS, 1), lambda b, i: (l, 0, 0)),
                  pl.BlockSpec((None, 1, GROUP), lambda b, i: (l, 0, 0)),
                  pl.BlockSpec((cl, cl), lambda b, i: (0, 0))],
        out_specs=pl.BlockSpec((None, ts, GROUP), lambda b, i: (b, i, 0)),
        scratch_shapes=[pltpu.VMEM((8, 2 * GROUP), F32),
                        pltpu.VMEM((ts, GROUP), F32),
                        pltpu.VMEM((ts, GROUP), F32),
                        pltpu.VMEM((HEADS, HEAD_DIM, HEAD_DIM), F32),
                        pltpu.VMEM((HEADS, 1, HEAD_DIM), F32),
                        pltpu.VMEM((HEADS, 1, 128), F32)],
        compiler_params=_cp(("arbitrary", "arbitrary")),
        name="mlstm",
    )(proj, proj, proj, proj, gt, conv_w, bias_c, bias_r, out_g, tri)


def _fox_kernel(q_ref, k_ref, v_ref, gr_ref, br_ref, qg_ref, kg_ref, bd_ref, tri_ref, y_ref,
                kn_ref, vb_ref, fn_ref, *, t, s_len):
    qi = pl.program_id(1)
    bd = bd_ref[...]

    @pl.when(qi == 0)
    def _():
        kg = kg_ref[...]
        for r in range(s_len // t):
            rows = slice(r * t, (r + 1) * t)
            kn = _rms_heads(k_ref[rows, :], bd, kg)
            vv = v_ref[rows, :]
            for h in range(HEADS):
                sl = slice(h * HEAD_DIM, (h + 1) * HEAD_DIM)
                kn_ref[h, rows, :] = kn[:, sl].astype(BF16)
                vb_ref[h, rows, :] = vv[:, sl].astype(BF16)
        carry = jnp.zeros((8, 1), F32)
        for r in range(s_len // t):
            cols = slice(r * t, (r + 1) * t)
            lf = _log_sigmoid(gr_ref[G_FF:G_FF + 8, cols] + br_ref[G_FF:G_FF + 8, :])
            cs = _dot_t_hi(lf, tri_ref[...]) + carry
            fn_ref[:, cols] = -cs
            carry = cs[:, t - 1:t]

    qn = _rms_heads(q_ref[...], bd, qg_ref[...]) * QK_SCALE
    causal = lax.broadcasted_iota(jnp.int32, (t, t), 1) <= lax.broadcasted_iota(jnp.int32, (t, t), 0)
    outs = []
    for h in range(HEADS):
        qh = qn[:, h * HEAD_DIM:(h + 1) * HEAD_DIM].astype(BF16)

        def step(kb, carry, masked, h=h, qh=qh):
            m, l, acc = carry
            k0 = pl.multiple_of(kb * t, t)
            sc = _dot_t(qh, kn_ref[h, pl.ds(k0, t), :]) + fn_ref[h:h + 1, pl.ds(k0, t)]
            if masked:
                sc = jnp.where(causal, sc, NEG)
            m_new = jnp.maximum(m, jnp.max(sc, axis=1, keepdims=True))
            p = jnp.exp(sc - m_new)
            alpha = jnp.exp(m - m_new)
            l = alpha * l + jnp.sum(p, axis=1, keepdims=True)
            acc = alpha * acc + _dot(p, vb_ref[h, pl.ds(k0, t), :])
            return m_new, l, acc

        init = (jnp.full((t, 1), NEG, F32), jnp.zeros((t, 1), F32), jnp.zeros((t, HEAD_DIM), F32))
        carry = lax.fori_loop(0, qi, functools.partial(step, masked=False), init)
        _, l, acc = step(qi, carry, True)
        outs.append(acc / l)
    y_ref[...] = jnp.concatenate(outs, axis=1).astype(y_ref.dtype)


def _fox(proj, gt, l, bias_r, q_g, k_g, bd, tri):
    nb, s, _ = proj.shape
    t = ATT_T
    kern = functools.partial(_fox_kernel, t=t, s_len=s)
    return pl.pallas_call(
        kern,
        out_shape=jax.ShapeDtypeStruct((nb, s, GROUP), BF16),
        grid=(nb, s // t),
        in_specs=[pl.BlockSpec((None, t, GROUP), lambda b, i: (b, i, C_FQ // GROUP)),
                  pl.BlockSpec((None, s, GROUP), lambda b, i: (b, 0, C_FK // GROUP)),
                  pl.BlockSpec((None, s, GROUP), lambda b, i: (b, 0, C_FV // GROUP)),
                  pl.BlockSpec((None, GT_ROWS, s), lambda b, i: (b, 0, 0)),
                  pl.BlockSpec((None, GT_ROWS, 1), lambda b, i: (l, 0, 0)),
                  pl.BlockSpec((None, 1, GROUP), lambda b, i: (l, 0, 0)),
                  pl.BlockSpec((None, 1, GROUP), lambda b, i: (l, 0, 0)),
                  pl.BlockSpec((GROUP, GROUP), lambda b, i: (0, 0)),
                  pl.BlockSpec((t, t), lambda b, i: (0, 0))],
        out_specs=pl.BlockSpec((None, t, GROUP), lambda b, i: (b, i, 0)),
        scratch_shapes=[pltpu.VMEM((HEADS, s, HEAD_DIM), BF16),
                        pltpu.VMEM((HEADS, s, HEAD_DIM), BF16),
                        pltpu.VMEM((8, s), F32)],
        compiler_params=_cp(("arbitrary", "arbitrary")),
        name="fox",
    )(proj, proj, proj, gt, bias_r, q_g, k_g, bd, tri)


def _sb_kernel(q_ref, k_ref, v_ref, sm_ref, y_ref, kb_ref, vb_ref, *, t, s_len):
    qi = pl.program_id(1)

    @pl.when(qi == 0)
    def _():
        for r in range(s_len // t):
            rows = slice(r * t, (r + 1) * t)
            kk = k_ref[rows, :]
            vv = v_ref[rows, :]
            for h in range(HEADS):
                sl = slice(h * HEAD_DIM, (h + 1) * HEAD_DIM)
                kb_ref[h, rows, :] = kk[:, sl].astype(BF16)
                vb_ref[h, rows, :] = vv[:, sl].astype(BF16)

    qs = q_ref[...] * QK_SCALE
    strict = lax.broadcasted_iota(jnp.int32, (t, t), 1) < lax.broadcasted_iota(jnp.int32, (t, t), 0)
    after = sm_ref[...]
    outs = []
    for h in range(HEADS):
        qh = qs[:, h * HEAD_DIM:(h + 1) * HEAD_DIM].astype(BF16)

        def step(kb, carry, masked, h=h, qh=qh):
            rest, acc = carry
            k0 = pl.multiple_of(kb * t, t)
            z = _dot_t(qh, kb_ref[h, pl.ds(k0, t), :])
            ls_pos = jnp.minimum(z, 0.0) - jnp.log1p(jnp.exp(-jnp.abs(z)))
            lr = ls_pos - z
            if masked:
                lr = jnp.where(strict, lr, 0.0)
            hi = lr.astype(BF16)
            lo = (lr - hi.astype(F32)).astype(BF16)
            suffix = (jnp.dot(hi, after, preferred_element_type=F32)
                      + jnp.dot(lo, after, preferred_element_type=F32))
            a = jnp.exp(ls_pos + suffix + rest)
            if masked:
                a = jnp.where(strict, a, 0.0)
            acc = acc + _dot(a, vb_ref[h, pl.ds(k0, t), :])
            rest = rest + jnp.sum(lr, axis=1, keepdims=True)
            return rest, acc

        carry = step(qi, (jnp.zeros((t, 1), F32), jnp.zeros((t, HEAD_DIM), F32)), True)
        _, acc = lax.fori_loop(0, qi, lambda j, c: step(qi - 1 - j, c, False), carry)
        outs.append(acc)
    y_ref[...] = jnp.concatenate(outs, axis=1).astype(y_ref.dtype)


def _sb(proj, after):
    nb, s, _ = proj.shape
    t = ATT_T
    kern = functools.partial(_sb_kernel, t=t, s_len=s)
    return pl.pallas_call(
        kern,
        out_shape=jax.ShapeDtypeStruct((nb, s, GROUP), BF16),
        grid=(nb, s // t),
        in_specs=[pl.BlockSpec((None, t, GROUP), lambda b, i: (b, i, C_SQ // GROUP)),
                  pl.BlockSpec((None, s, GROUP), lambda b, i: (b, 0, C_SK // GROUP)),
                  pl.BlockSpec((None, s, GROUP), lambda b, i: (b, 0, C_SV // GROUP)),
                  pl.BlockSpec((t, t), lambda b, i: (0, 0))],
        out_specs=pl.BlockSpec((None, t, GROUP), lambda b, i: (b, i, 0)),
        scratch_shapes=[pltpu.VMEM((HEADS, s, HEAD_DIM), BF16),
                        pltpu.VMEM((HEADS, s, HEAD_DIM), BF16)],
        compiler_params=_cp(("arbitrary", "arbitrary")),
        name="stick_breaking",
    )(proj, proj, proj, after)


def _nsa_kernel(q_ref, kvs_ref, kvw_ref, ak_ref, av_ref, gc_ref, gb_ref, qg_ref, kg_ref, pos_ref,
                wk_ref, wv_ref, bd_ref, tc_ref, ovt_ref, tsb_ref, twb_ref, cb_ref, y_ref,
                ksel_ref, vsel_ref, kwin_ref, vwin_ref, kcmp_ref, vcmp_ref, m_ref, l_ref, acc_ref,
                *, tq, s_len, n_sel):
    qi = pl.program_id(1)
    nc = s_len // CMP_STRIDE
    rows4 = HEADS * tq

    @pl.when(qi == 0)
    def _():
        zpad = jnp.zeros((WINDOW, HEAD_DIM), BF16)
        ksel_ref[0:WINDOW, :] = zpad
        vsel_ref[0:WINDOW, :] = zpad
        kwin_ref[0:WINDOW, :] = zpad
        vwin_ref[0:WINDOW, :] = zpad
        for r in range(s_len // 512):
            rows = slice(r * 512, (r + 1) * 512)
            dst = slice(WINDOW + r * 512, WINDOW + (r + 1) * 512)
            kvs = kvs_ref[rows, :]
            kvw = kvw_ref[rows, :]
            ksel_ref[dst, :] = _rms_lanes(kvs[:, :HEAD_DIM], kg_ref[1:2, :]).astype(BF16)
            vsel_ref[dst, :] = kvs[:, HEAD_DIM:].astype(BF16)
            kwin_ref[dst, :] = _rms_lanes(kvw[:, :HEAD_DIM], kg_ref[2:3, :]).astype(BF16)
            vwin_ref[dst, :] = kvw[:, HEAD_DIM:].astype(BF16)
        half = CMP_STRIDE * HEAD_DIM
        for a_ref, w_ref, dst_ref, norm in ((ak_ref, wk_ref, kcmp_ref, True), (av_ref, wv_ref, vcmp_ref, False)):
            a = a_ref[...]
            pos_b = _dot_hi(pos_ref[...], w_ref[...])[0:1, :]
            p1 = _dot_hi(a, w_ref[0:half, :])
            p2 = _dot_hi(a, w_ref[half:2 * half, :])
            c = p1 + pltpu.roll(p2, nc - 1, 0) + pos_b
            dst_ref[...] = _rms_lanes(c, kg_ref[0:1, :]) if norm else c

    t0 = qi * tq
    qn = _rms_heads(q_ref[...], bd_ref[...], qg_ref[...]) * QK_SCALE
    qs = jnp.concatenate([qn[:, h * HEAD_DIM:(h + 1) * HEAD_DIM] for h in range(HEADS)], axis=0)
    qb = qs.astype(BF16)

    tc = tc_ref[...].reshape(rows4, nc)
    sc = _dot_t_hi(qs, kcmp_ref[...]) + tc
    valid = tc > 0.5 * NEG
    e = jnp.exp(sc - jnp.max(sc, axis=1, keepdims=True))
    p = jnp.where(valid, e / jnp.sum(e, axis=1, keepdims=True), 0.0)
    o_cmp = _dot(p, vcmp_ref[...])
    psum = p[0:tq] + p[tq:2 * tq] + p[2 * tq:3 * tq] + p[3 * tq:4 * tq]
    imp_t = _dot_t_hi(ovt_ref[...], psum)

    jj = lax.broadcasted_iota(jnp.int32, (128, tq), 0)
    tt = t0 + lax.broadcasted_iota(jnp.int32, (128, tq), 1)
    cur = tt // SEL_BLOCK
    forced = (jj == 0) | (jj == cur) | (jj == cur - 1)
    score = jnp.where(forced, -NEG, jnp.where(jj * SEL_BLOCK <= tt, imp_t, NEG))
    cnt = jnp.zeros((128, tq), F32)
    for i in range(n_sel):
        ri = score[i:i + 1, :]
        cnt = cnt + jnp.where(jj > i, (ri >= score).astype(F32), (ri > score).astype(F32))
    sel = (cnt < float(min(N_SEL_TOP, n_sel))).astype(F32).T
    selb = sel.astype(BF16)

    m_ref[...] = jnp.full((rows4, 1), NEG, F32)
    l_ref[...] = jnp.zeros((rows4, 1), F32)
    acc_ref[...] = jnp.zeros((rows4, HEAD_DIM), F32)
    jrow = lax.broadcasted_iota(jnp.int32, (128, tq), 0)
    jcol = lax.broadcasted_iota(jnp.int32, (128, tq), 1) // SEL_BLOCK

    def sel_chunk(kp0, bias):
        sck = _dot_t(qb, ksel_ref[pl.ds(kp0, tq), :])
        jb = kp0 // SEL_BLOCK - WINDOW // SEL_BLOCK
        expand = (jrow == jb + jcol).astype(BF16)
        mk = jnp.dot(selb, expand, preferred_element_type=F32)
        mb = jnp.where(mk > 0.5, 0.0, NEG)
        sck = (sck.reshape(HEADS, tq, tq) + mb[None]).reshape(rows4, tq) + bias
        m_old = m_ref[...]
        m_new = jnp.maximum(m_old, jnp.max(sck, axis=1, keepdims=True))
        pk = jnp.exp(sck - m_new)
        alpha = jnp.exp(m_old - m_new)
        l_ref[...] = alpha * l_ref[...] + jnp.sum(pk, axis=1, keepdims=True)
        acc_ref[...] = alpha * acc_ref[...] + _dot(pk, vsel_ref[pl.ds(kp0, tq), :])
        m_ref[...] = m_new

    def far(ci, carry):
        sel_chunk(pl.multiple_of(WINDOW + ci * tq, tq), cb_ref[...])
        return carry

    lax.fori_loop(0, jnp.maximum(qi - WINDOW // tq, 0), far, 0)
    for r in range(NSA_NEAR // tq):
        @pl.when(qi + r >= WINDOW // tq)
        def _(r=r):
            sel_chunk(pl.multiple_of(t0 + r * tq, tq), tsb_ref[:, r * tq:(r + 1) * tq])
    o_sel = acc_ref[...] / l_ref[...]

    sw = _dot_t(qb, kwin_ref[pl.ds(pl.multiple_of(t0, tq), NSA_NEAR), :]) + twb_ref[...]
    wcol = lax.broadcasted_iota(jnp.int32, (rows4, NSA_NEAR), 1)
    sw = jnp.where(wcol >= WINDOW - t0, sw, NEG)
    ew = jnp.exp(sw - jnp.max(sw, axis=1, keepdims=True))
    pw = ew / jnp.sum(ew, axis=1, keepdims=True)
    o_win = _dot(pw, vwin_ref[pl.ds(pl.multiple_of(t0, tq), NSA_NEAR), :])

    g = _sigmoid(gc_ref[...] + gb_ref[...])
    outs = []
    for h in range(HEADS):
        rows = slice(h * tq, (h + 1) * tq)
        c0 = G_NG + 3 * h
        outs.append(g[:, c0:c0 + 1] * o_cmp[rows] + g[:, c0 + 1:c0 + 2] * o_sel[rows] + g[:, c0 + 2:c0 + 3] * o_win[rows])
    y_ref[...] = jnp.concatenate(outs, axis=1).astype(y_ref.dtype)


def _nsa(proj, a_k, a_v, l, bias_c, q_g, k_g, pos, wk, wv, bd, tc, ovt, tsb, twb, cb):
    nb, s, _ = proj.shape
    tq = NSA_TQ
    nc = s // CMP_STRIDE
    kern = functools.partial(_nsa_kernel, tq=tq, s_len=s, n_sel=s // SEL_BLOCK)
    const2 = lambda b, i: (0, 0)
    lay3 = lambda b, i: (l, 0, 0)
    return pl.pallas_call(
        kern,
        out_shape=jax.ShapeDtypeStruct((nb, s, GROUP), BF16),
        grid=(nb, s // tq),
        in_specs=[pl.BlockSpec((None, tq, GROUP), lambda b, i: (b, i, C_NQ // GROUP)),
                  pl.BlockSpec((None, s, 128), lambda b, i: (b, 0, C_NKV // 128 + 1)),
                  pl.BlockSpec((None, s, 128), lambda b, i: (b, 0, C_NKV // 128 + 2)),
                  pl.BlockSpec((None, nc, CMP_STRIDE * HEAD_DIM), lambda b, i: (b, 0, 0)),
                  pl.BlockSpec((None, nc, CMP_STRIDE * HEAD_DIM), lambda b, i: (b, 0, 0)),
                  pl.BlockSpec((None, tq, 128), lambda b, i: (b, i, C_GATES // 128)),
                  pl.BlockSpec((None, 1, 128), lay3),
                  pl.BlockSpec((None, 1, GROUP), lay3),
                  pl.BlockSpec((None, 8, HEAD_DIM), lay3),
                  pl.BlockSpec((None, 8, CMP_BLOCK * HEAD_DIM), lay3),
                  pl.BlockSpec((None, CMP_BLOCK * HEAD_DIM, HEAD_DIM), lay3),
                  pl.BlockSpec((None, CMP_BLOCK * HEAD_DIM, HEAD_DIM), lay3),
                  pl.BlockSpec((GROUP, GROUP), const2),
                  pl.BlockSpec((HEADS, tq, nc), lambda b, i: (0, i, 0)),
                  pl.BlockSpec((128, nc), const2),
                  pl.BlockSpec((HEADS * tq, NSA_NEAR), const2),
                  pl.BlockSpec((HEADS * tq, NSA_NEAR), const2),
                  pl.BlockSpec((HEADS * tq, 1), const2)],
        out_specs=pl.BlockSpec((None, tq, GROUP), lambda b, i: (b, i, 0)),
        scratch_shapes=[pltpu.VMEM((s + WINDOW, HEAD_DIM), BF16),
                        pltpu.VMEM((s + WINDOW, HEAD_DIM), BF16),
                        pltpu.VMEM((s + WINDOW, HEAD_DIM), BF16),
                        pltpu.VMEM((s + WINDOW, HEAD_DIM), BF16),
                        pltpu.VMEM((nc, HEAD_DIM), F32),
                        pltpu.VMEM((nc, HEAD_DIM), F32),
                        pltpu.VMEM((HEADS * tq, 1), F32),
                        pltpu.VMEM((HEADS * tq, 1), F32),
                        pltpu.VMEM((HEADS * tq, HEAD_DIM), F32)],
        compiler_params=_cp(("arbitrary", "arbitrary")),
        name="nsa",
    )(proj, proj, proj, a_k, a_v, proj, bias_c, q_g, k_g, pos, wk, wv, bd, tc, ovt, tsb, twb, cb)


def _outproj_kernel(x_ref, ya_ref, yb_ref, yc_ref, yd_ref, w_ref, mod_ref, g_ref, xo_ref, h_ref):
    acc = jnp.dot(ya_ref[...], w_ref[0:GROUP, :], preferred_element_type=F32)
    acc += jnp.dot(yb_ref[...], w_ref[GROUP:2 * GROUP, :], preferred_element_type=F32)
    acc += jnp.dot(yc_ref[...], w_ref[2 * GROUP:3 * GROUP, :], preferred_element_type=F32)
    acc += jnp.dot(yd_ref[...], w_ref[3 * GROUP:4 * GROUP, :], preferred_element_type=F32)
    xn = x_ref[...] + mod_ref[2:3, :] * acc
    xo_ref[...] = xn
    h = _rms_lanes(xn, g_ref[...]) * (1.0 + mod_ref[4:5, :]) + mod_ref[3:4, :]
    h_ref[...] = h.astype(h_ref.dtype)


def _outproj(x, ys, mod, l, w_out, norm_g, h_dtype):
    nb, s, d = x.shape
    tm = 512
    yspec = pl.BlockSpec((None, tm, GROUP), lambda b, i: (b, i, 0))
    xspec = pl.BlockSpec((None, tm, d), lambda b, i: (b, i, 0))
    return pl.pallas_call(
        _outproj_kernel,
        out_shape=(jax.ShapeDtypeStruct((nb, s, d), F32), jax.ShapeDtypeStruct((nb, s, d), h_dtype)),
        grid=(nb, s // tm),
        in_specs=[xspec, yspec, yspec, yspec, yspec,
                  pl.BlockSpec((None, d, d), lambda b, i: (l, 0, 0)),
                  pl.BlockSpec((None, None, 6, d), lambda b, i: (l, b, 0, 0)),
                  pl.BlockSpec((None, 1, d), lambda b, i: (l, 0, 0))],
        out_specs=(xspec, xspec),
        compiler_params=_cp(("arbitrary", "arbitrary")),
        name="outproj",
    )(x, *ys, w_out, mod, norm_g)


def _ffn_kernel(h_ref, x_ref, mod_ref, wg_ref, wu_ref, wd_ref, o_ref, acc_ref, *, nf):
    f = pl.program_id(2)

    @pl.when(f == 0)
    def _():
        acc_ref[...] = jnp.zeros_like(acc_ref)

    h = h_ref[...]
    a = jnp.dot(h, wg_ref[...], preferred_element_type=F32)
    u = jnp.dot(h, wu_ref[...], preferred_element_type=F32)
    act = (a * _sigmoid(a) * u).astype(BF16)
    acc_ref[...] += jnp.dot(act, wd_ref[...], preferred_element_type=F32)

    @pl.when(f == nf - 1)
    def _():
        o_ref[...] = x_ref[...] + mod_ref[5:6, :] * acc_ref[...]


def _ffn(h, x, mod, l, li, wg, wu, wd):
    nb, s, d = x.shape
    dff = wg.shape[-1]
    tm, tf = FFN_TM, FFN_TF
    nf = dff // tf
    xspec = pl.BlockSpec((None, tm, d), lambda b, i, f: (b, i, 0))
    return pl.pallas_call(
        functools.partial(_ffn_kernel, nf=nf),
        out_shape=jax.ShapeDtypeStruct((nb, s, d), F32),
        grid=(nb, s // tm, nf),
        in_specs=[xspec, xspec,
                  pl.BlockSpec((None, None, 6, d), lambda b, i, f: (l, b, 0, 0)),
                  pl.BlockSpec((None, d, tf), lambda b, i, f: (li, 0, f)),
                  pl.BlockSpec((None, d, tf), lambda b, i, f: (li, 0, f)),
                  pl.BlockSpec((None, tf, d), lambda b, i, f: (li, f, 0))],
        out_specs=xspec,
        scratch_shapes=[pltpu.VMEM((tm, d), F32)],
        compiler_params=_cp(("arbitrary", "arbitrary", "arbitrary")),
        name="ffn_dense",
    )(h, x, mod, wg, wu, wd)


def _router_kernel(h_ref, w_ref, b_ref, o_ref):
    logits = _dot_hi(h_ref[...], w_ref[...]) + b_ref[...]
    lane = lax.broadcasted_iota(jnp.int32, logits.shape, 1).astype(F32)
    lg = jnp.where(lane < N_EXPERTS, logits, -3e38)
    m1 = jnp.max(lg, axis=1, keepdims=True)
    i1 = jnp.min(jnp.where(lg == m1, lane, 128.0), axis=1, keepdims=True)
    lg2 = jnp.where(lane == i1, -3e38, lg)
    m2 = jnp.max(lg2, axis=1, keepdims=True)
    i2 = jnp.min(jnp.where(lg2 == m2, lane, 128.0), axis=1, keepdims=True)
    e2 = jnp.exp(m2 - m1)
    w1 = 1.0 / (1.0 + e2)
    w2 = e2 / (1.0 + e2)
    o_ref[...] = jnp.where(lane == 0, i1, jnp.where(lane == 1, i2, jnp.where(lane == 2, w1, jnp.where(lane == 3, w2, 0.0))))


def _router(hf, rw, rb):
    t, d = hf.shape
    tm = 1024
    return pl.pallas_call(
        _router_kernel,
        out_shape=jax.ShapeDtypeStruct((t, 128), F32),
        grid=(t // tm,),
        in_specs=[pl.BlockSpec((tm, d), lambda i: (i, 0)),
                  pl.BlockSpec((d, 128), lambda i: (0, 0)),
                  pl.BlockSpec((1, 128), lambda i: (0, 0))],
        out_specs=pl.BlockSpec((tm, 128), lambda i: (i, 0)),
        compiler_params=_cp(("arbitrary",)),
        name="moe_router",
    )(hf, rw, rb)


def _gather_copy(src_hbm, row, dst_buf, slot, r, sem):
    return pltpu.make_async_copy(src_hbm.at[pl.ds(row, 1)], dst_buf.at[slot, pl.ds(r, 1)], sem.at[slot])


def _experts_kernel(be_ref, tok_ref, nv_ref, h_hbm, rw_ref, wg_ref, wu_ref, wd_ref, y_ref,
                    xbuf, xb_ref, acc_ref, sem, *, bm, nf):
    del be_ref
    i = pl.program_id(0)
    f = pl.program_id(1)
    nvalid = nv_ref[0]
    slot = i % 2

    def issue(blk, sl):
        def body(r, c):
            _gather_copy(h_hbm, tok_ref[blk * bm + r], xbuf, sl, r, sem).start()
            return c
        lax.fori_loop(0, bm, body, 0, unroll=8)

    def wait_all(sl):
        def body(r, c):
            _gather_copy(h_hbm, 0, xbuf, sl, r, sem).wait()
            return c
        lax.fori_loop(0, bm, body, 0, unroll=8)

    @pl.when((f == 0) & (i == 0))
    def _():
        issue(0, 0)

    @pl.when((f == 0) & (i < nvalid))
    def _():
        wait_all(slot)

        @pl.when(i + 1 < nvalid)
        def _():
            issue(i + 1, 1 - slot)

        xb_ref[...] = xbuf[slot].astype(BF16)
        acc_ref[...] = jnp.zeros_like(acc_ref)

    @pl.when(i < nvalid)
    def _():
        x = xb_ref[...]
        a = jnp.dot(x, wg_ref[...], preferred_element_type=F32)
        u = jnp.dot(x, wu_ref[...], preferred_element_type=F32)
        act = (a * _sigmoid(a) * u).astype(BF16)
        acc_ref[...] += jnp.dot(act, wd_ref[...], preferred_element_type=F32)

    @pl.when(f == nf - 1)
    def _():
        @pl.when(i < nvalid)
        def _():
            y_ref[...] = acc_ref[...] * rw_ref[...]

        @pl.when(i >= nvalid)
        def _():
            y_ref[...] = jnp.zeros_like(y_ref)


def _experts(hf, blk_expert, row_tok, nvalid, row_w, wg, wu, wd, li):
    t, d = hf.shape
    bm, tf = MOE_BM, MOE_TF
    n_rows = row_tok.shape[0]
    n_blocks = n_rows // bm
    dff = wg.shape[-1]
    nf = dff // tf

    def fidx(i, f, nv):
        return jnp.where(i < nv[0], f, nf - 1)

    grid_spec = pltpu.PrefetchScalarGridSpec(
        num_scalar_prefetch=3,
        grid=(n_blocks, nf),
        in_specs=[pl.BlockSpec(memory_space=pl.ANY),
                  pl.BlockSpec((bm, 1), lambda i, f, be, tok, nv: (i, 0)),
                  pl.BlockSpec((None, None, d, tf), lambda i, f, be, tok, nv: (li, be[i], 0, fidx(i, f, nv))),
                  pl.BlockSpec((None, None, d, tf), lambda i, f, be, tok, nv: (li, be[i], 0, fidx(i, f, nv))),
                  pl.BlockSpec((None, None, tf, d), lambda i, f, be, tok, nv: (li, be[i], fidx(i, f, nv), 0))],
        out_specs=pl.BlockSpec((bm, d), lambda i, f, be, tok, nv: (i, 0)),
        scratch_shapes=[pltpu.VMEM((2, bm, d), F32),
                        pltpu.VMEM((bm, d), BF16),
                        pltpu.VMEM((bm, d), F32),
                        pltpu.SemaphoreType.DMA((2,))],
    )
    return pl.pallas_call(
        functools.partial(_experts_kernel, bm=bm, nf=nf),
        out_shape=jax.ShapeDtypeStruct((n_rows, d), F32),
        grid_spec=grid_spec,
        compiler_params=_cp(("arbitrary", "arbitrary")),
        name="moe_experts",
    )(blk_expert, row_tok, nvalid, hf, row_w, wg, wu, wd)


def _combine_kernel(pos_ref, y_hbm, x_ref, mod_ref, o_ref, ybuf, sem, *, tm, s_len):
    b = pl.program_id(0)
    i = pl.program_id(1)
    base = (b * s_len + i * tm) * 2

    def issue(r, c):
        _gather_copy(y_hbm, pos_ref[base + 2 * r], ybuf, 0, r, sem).start()
        _gather_copy(y_hbm, pos_ref[base + 2 * r + 1], ybuf, 1, r, sem).start()
        return c

    def wait(r, c):
        _gather_copy(y_hbm, 0, ybuf, 0, r, sem).wait()
        _gather_copy(y_hbm, 0, ybuf, 1, r, sem).wait()
        return c

    lax.fori_loop(0, tm, issue, 0, unroll=8)
    lax.fori_loop(0, tm, wait, 0, unroll=8)
    o_ref[...] = x_ref[...] + mod_ref[5:6, :] * (ybuf[0] + ybuf[1])


def _combine(y, pos, x, mod, l):
    nb, s, d = x.shape
    tm = 256
    grid_spec = pltpu.PrefetchScalarGridSpec(
        num_scalar_prefetch=1,
        grid=(nb, s // tm),
        in_specs=[pl.BlockSpec(memory_space=pl.ANY),
                  pl.BlockSpec((None, tm, d), lambda b, i, p: (b, i, 0)),
                  pl.BlockSpec((None, None, 6, d), lambda b, i, p: (l, b, 0, 0))],
        out_specs=pl.BlockSpec((None, tm, d), lambda b, i, p: (b, i, 0)),
        scratch_shapes=[pltpu.VMEM((2, tm, d), F32), pltpu.SemaphoreType.DMA((2,))],
    )
    return pl.pallas_call(
        functools.partial(_combine_kernel, tm=tm, s_len=s),
        out_shape=jax.ShapeDtypeStruct((nb, s, d), F32),
        grid_spec=grid_spec,
        compiler_params=_cp(("arbitrary", "arbitrary")),
        name="moe_combine",
    )(pos, y, x, mod)


def _moe(hf32, x, mod, l, li, router_w, router_b, wg, wu, wd):
    nb, s, d = x.shape
    t = nb * s
    bm = MOE_BM
    hf = hf32.reshape(t, d)
    rw = jnp.zeros((d, 128), F32).at[:, :N_EXPERTS].set(router_w[li])
    rb = jnp.zeros((1, 128), F32).at[0, :N_EXPERTS].set(router_b[li])
    route = _router(hf, rw, rb)
    e_flat = route[:, 0:2].astype(jnp.int32).reshape(-1)
    w_flat = route[:, 2:4].reshape(-1)
    n_assign = 2 * t
    onehot = (e_flat[:, None] == jnp.arange(N_EXPERTS, dtype=jnp.int32)[None, :]).astype(jnp.int32)
    csum = jnp.cumsum(onehot, axis=0)
    rank = jnp.sum(onehot * csum, axis=1) - 1
    counts = csum[-1]
    padded = (counts + bm - 1) // bm * bm
    cum_padded = jnp.cumsum(padded)
    pstart = cum_padded - padded
    dest = pstart[e_flat] + rank
    n_blocks = n_assign // bm + N_EXPERTS
    n_rows = n_blocks * bm
    tok_flat = jnp.arange(n_assign, dtype=jnp.int32) // 2
    row_tok = jnp.zeros((n_rows,), jnp.int32).at[dest].set(tok_flat)
    row_w = jnp.zeros((n_rows,), F32).at[dest].set(w_flat)
    blk_start = jnp.arange(n_blocks, dtype=jnp.int32) * bm
    blk_expert = jnp.minimum(jnp.searchsorted(cum_padded, blk_start, side='right'), N_EXPERTS - 1).astype(jnp.int32)
    nvalid = (cum_padded[-1] // bm).astype(jnp.int32).reshape(1)
    y = _experts(hf, blk_expert, row_tok, nvalid, row_w.reshape(n_rows, 1), wg, wu, wd, li)
    return _combine(y, dest.astype(jnp.int32), x, mod, l)


def _t5_bucket(dist):
    n = np.maximum(dist, 0)
    max_exact = NUM_BUCKETS // 2
    large = max_exact + (np.log(np.maximum(n, 1).astype(np.float32) / max_exact)
                         / math.log(MAX_DISTANCE / max_exact) * (NUM_BUCKETS - max_exact)).astype(np.int32)
    return np.where(n < max_exact, n, np.minimum(large, NUM_BUCKETS - 1)).astype(np.int32)


def _nsa_tables(rel_bias, s):
    tq = NSA_TQ
    nc = s // CMP_STRIDE
    n_cmp = (s - CMP_BLOCK) // CMP_STRIDE + 1
    rb = rel_bias.astype(F32)
    t = np.arange(s)[:, None]
    cmp_end = np.arange(nc)[None, :] * CMP_STRIDE + CMP_BLOCK - 1
    ok = (cmp_end <= t) & (np.arange(nc)[None, :] < n_cmp)
    tc = jnp.where(jnp.asarray(ok)[None], jnp.transpose(rb[_t5_bucket(t - cmp_end)], (2, 0, 1)), NEG)
    dist = np.arange(tq)[:, None] - (np.arange(NSA_NEAR)[None, :] - WINDOW)
    tb = jnp.transpose(rb[_t5_bucket(dist)], (2, 0, 1))
    tsb = jnp.where(jnp.asarray(dist >= 0)[None], tb, NEG).reshape(HEADS * tq, NSA_NEAR)
    twb = jnp.where(jnp.asarray((dist >= 0) & (dist < WINDOW))[None], tb, NEG).reshape(HEADS * tq, NSA_NEAR)
    far_bucket = int(_t5_bucket(np.array([WINDOW]))[0])
    cb = jnp.repeat(rb[far_bucket], tq).reshape(HEADS * tq, 1)
    n_sel = s // SEL_BLOCK
    cs = np.arange(nc)[None, :] * CMP_STRIDE
    ss = np.arange(128)[:, None] * SEL_BLOCK
    ov = np.clip(np.minimum(cs + CMP_BLOCK, ss + SEL_BLOCK) - np.maximum(cs, ss), 0, None).astype(np.float32) / CMP_BLOCK
    ov = ov * (np.arange(128)[:, None] < n_sel) * (np.arange(nc)[None, :] < n_cmp)
    return tc, jnp.asarray(ov, F32), tsb, twb, cb


def _prep_layer_params(w_in, mlstm_gate_b, fox_f_b, nsa_gate_b):
    depth, d, _ = w_in.shape
    small = [w_in[:, :, 1024:1032], w_in[:, :, 1800:1804], w_in[:, :, 3212:3224]]
    w_p = jnp.concatenate([w_in[:, :, 0:1024], w_in[:, :, 1032:1800], w_in[:, :, 1804:2572], w_in[:, :, 2572:3212]]
                          + small + [jnp.zeros((depth, d, 128 - 24), w_in.dtype)], axis=-1).astype(BF16)
    wgt = jnp.concatenate(small + [jnp.zeros((depth, d, GT_ROWS - 24), w_in.dtype)], axis=-1)
    wgt = jnp.transpose(wgt, (0, 2, 1)).astype(BF16)
    gate_b = jnp.concatenate([mlstm_gate_b, fox_f_b, nsa_gate_b], axis=-1).astype(F32)
    bias_c = jnp.zeros((depth, 1, 128), F32).at[:, 0, :24].set(gate_b)
    bias_r = jnp.zeros((depth, GT_ROWS, 1), F32).at[:, :24, 0].set(gate_b)
    return w_p, wgt, bias_c, bias_r


def kernel(x, c, rel_bias, ada_w, ada_b, norm1_g, norm2_g, w_in, w_out, mlstm_conv_w, mlstm_gate_b, mlstm_out_g,
           fox_f_b, fox_q_g, fox_k_g, nsa_q_g, nsa_k_g, nsa_cmp_pos, nsa_cmp_wk, nsa_cmp_wv, nsa_gate_b,
           ffn_wg, ffn_wu, ffn_wd, moe_router_w, moe_router_b, moe_wg, moe_wu, moe_wd):
    nb, s, d = x.shape
    depth = w_in.shape[0]
    assert d == D_MODEL and s % 512 == 0 and s // SEL_BLOCK <= 128

    w_p, wgt, bias_c, bias_r = _prep_layer_params(w_in, mlstm_gate_b, fox_f_b, nsa_gate_b)
    w_out_b = w_out.astype(BF16)
    n1g = norm1_g.reshape(depth, 1, d)
    n2g = norm2_g.reshape(depth, 1, d)
    m_out_g = mlstm_out_g.reshape(depth, 1, GROUP)
    fq_g = jnp.tile(fox_q_g, (1, HEADS)).reshape(depth, 1, GROUP)
    fk_g = jnp.tile(fox_k_g, (1, HEADS)).reshape(depth, 1, GROUP)
    nq_g = jnp.tile(nsa_q_g, (1, HEADS)).reshape(depth, 1, GROUP)
    nk_g = jnp.zeros((depth, 8, HEAD_DIM), F32).at[:, :3].set(nsa_k_g)
    pos8 = jnp.zeros((depth, 8, CMP_BLOCK * HEAD_DIM), F32).at[:, 0].set(nsa_cmp_pos.reshape(depth, -1))
    ffn_wg_b, ffn_wu_b, ffn_wd_b = ffn_wg.astype(BF16), ffn_wu.astype(BF16), ffn_wd.astype(BF16)
    moe_wg_b, moe_wu_b, moe_wd_b = moe_wg.astype(BF16), moe_wu.astype(BF16), moe_wd.astype(BF16)
    hid = np.arange(GROUP) // HEAD_DIM
    bd = jnp.asarray((hid[:, None] == hid[None, :]).astype(np.float32) / HEAD_DIM)
    tri_l = jnp.asarray(np.tril(np.ones((MLSTM_L, MLSTM_L), np.float32)))
    tri_t = jnp.asarray(np.tril(np.ones((ATT_T, ATT_T), np.float32)))
    after = jnp.asarray(np.tril(np.ones((ATT_T, ATT_T), np.float32), -1), BF16)
    tc, ovt, tsb, twb, cb = _nsa_tables(rel_bias, s)

    mod = _adaln(c, ada_w, ada_b)
    for l in range(depth):
        proj, gt = _inproj(x, mod, l, n1g, w_p, wgt)
        y_a = _mlstm(proj, gt, l, mlstm_conv_w, bias_c, bias_r, m_out_g, tri_l)
        y_b = _fox(proj, gt, l, bias_r, fq_g, fk_g, bd, tri_t)
        y_c = _sb(proj, after)
        a_k = proj[:, :, C_NKV:C_NKV + HEAD_DIM].reshape(nb, s // CMP_STRIDE, CMP_STRIDE * HEAD_DIM)
        a_v = proj[:, :, C_NKV + HEAD_DIM:C_NKV + 2 * HEAD_DIM].reshape(nb, s // CMP_STRIDE, CMP_STRIDE * HEAD_DIM)
        y_d = _nsa(proj, a_k, a_v, l, bias_c, nq_g, nk_g, pos8, nsa_cmp_wk, nsa_cmp_wv, bd, tc, ovt, tsb, twb, cb)
        if l % 2 == 0:
            x, h2 = _outproj(x, (y_a, y_b, y_c, y_d), mod, l, w_out_b, n2g, BF16)
            x = _ffn(h2, x, mod, l, l // 2, ffn_wg_b, ffn_wu_b, ffn_wd_b)
        else:
            x, h2 = _outproj(x, (y_a, y_b, y_c, y_d), mod, l, w_out_b, n2g, F32)
            x = _moe(h2, x, mod, l, l // 2, moe_router_w, moe_router_b, moe_wg_b, moe_wu_b, moe_wd_b)
    return x
```

```python
import functools
import math

import numpy as np
import jax
import jax.numpy as jnp
from jax import lax
from jax.experimental import pallas as pl
from jax.experimental.pallas import tpu as pltpu

F32 = jnp.float32
BF16 = jnp.bfloat16
HI = lax.Precision.HIGHEST

D_MODEL = 1024
HEADS = 4
HEAD_DIM = 64
GROUP = HEADS * HEAD_DIM
NORM_EPS = 1e-6
NEG = -1e30
QK_SCALE = HEAD_DIM ** -0.5
CMP_BLOCK = 32
CMP_STRIDE = 16
SEL_BLOCK = 64
N_SEL_TOP = 16
WINDOW = 512
NUM_BUCKETS = 32
MAX_DISTANCE = 128
N_EXPERTS = 8
VMEM_LIMIT = 56 * 1024 * 1024

PW = 3328
C_MQK, C_MV, C_MO = 0, 512, 768
C_FQ, C_FK, C_FV = 1024, 1280, 1536
C_SQ, C_SK, C_SV = 1792, 2048, 2304
C_NQ, C_NKV, C_GATES = 2560, 2816, 3200
G_MI, G_MF, G_FF, G_NG = 0, 4, 8, 12
GT_ROWS = 32

MLSTM_TS = 512
MLSTM_L = 128
ATT_T = 256
NSA_TQ = 256
NSA_NEAR = WINDOW + NSA_TQ
MOE_BM = 512
MOE_TF = 512
FFN_TM = 512
FFN_TF = 1408


def _cp(sem, vmem=VMEM_LIMIT):
    return pltpu.CompilerParams(dimension_semantics=sem, vmem_limit_bytes=vmem)


def _dot(a, b):
    return jnp.dot(a.astype(BF16), b.astype(BF16), preferred_element_type=F32)


def _dot_t(a, b):
    return lax.dot_general(a.astype(BF16), b.astype(BF16), (((1,), (1,)), ((), ())), preferred_element_type=F32)


def _dot_hi(a, b):
    return jnp.dot(a, b, precision=HI, preferred_element_type=F32)


def _dot_t_hi(a, b):
    return lax.dot_general(a, b, (((1,), (1,)), ((), ())), precision=HI, preferred_element_type=F32)


def _sigmoid(x):
    return 1.0 / (1.0 + jnp.exp(-x))


def _log_sigmoid(x):
    return jnp.minimum(x, 0.0) - jnp.log1p(jnp.exp(-jnp.abs(x)))


def _rms_lanes(x, g):
    return x * lax.rsqrt(jnp.mean(x * x, axis=-1, keepdims=True) + NORM_EPS) * g


def _rms_heads(x, bd, g):
    return x * lax.rsqrt(_dot_hi(x * x, bd) + NORM_EPS) * g


def _adaln_kernel(c_ref, w_ref, b_ref, o_ref):
    c = c_ref[...]
    o_ref[...] = _dot_hi(c * _sigmoid(c), w_ref[...]) + b_ref[...]


def _adaln(c, ada_w, ada_b):
    depth, d, six_d = ada_w.shape
    nb = c.shape[0]
    out = pl.pallas_call(
        _adaln_kernel,
        out_shape=jax.ShapeDtypeStruct((depth, nb, six_d), F32),
        grid=(depth, six_d // d),
        in_specs=[pl.BlockSpec((nb, d), lambda l, j: (0, 0)),
                  pl.BlockSpec((None, d, d), lambda l, j: (l, 0, j)),
                  pl.BlockSpec((None, 1, d), lambda l, j: (l, 0, j))],
        out_specs=pl.BlockSpec((None, nb, d), lambda l, j: (l, 0, j)),
        compiler_params=_cp(("arbitrary", "arbitrary")),
        name="adaln",
    )(c, ada_w, ada_b.reshape(depth, 1, six_d))
    return out.reshape(depth, nb, 6, d)


def _inproj_kernel(x_ref, mod_ref, g_ref, w_ref, wgt_ref, o_ref, gt_ref):
    h = _rms_lanes(x_ref[...], g_ref[...]) * (1.0 + mod_ref[1:2, :]) + mod_ref[0:1, :]
    hb = h.astype(BF16)
    o_ref[...] = jnp.dot(hb, w_ref[...], preferred_element_type=F32)
    gt_ref[...] = lax.dot_general(wgt_ref[...], hb, (((1,), (1,)), ((), ())), preferred_element_type=F32)


def _inproj(x, mod, l, norm_g, w_p, wgt):
    nb, s, d = x.shape
    tm = 256
    return pl.pallas_call(
        _inproj_kernel,
        out_shape=(jax.ShapeDtypeStruct((nb, s, PW), F32), jax.ShapeDtypeStruct((nb, GT_ROWS, s), F32)),
        grid=(nb, s // tm),
        in_specs=[pl.BlockSpec((None, tm, d), lambda b, i: (b, i, 0)),
                  pl.BlockSpec((None, None, 6, d), lambda b, i: (l, b, 0, 0)),
                  pl.BlockSpec((None, 1, d), lambda b, i: (l, 0, 0)),
                  pl.BlockSpec((None, d, PW), lambda b, i: (l, 0, 0)),
                  pl.BlockSpec((None, GT_ROWS, d), lambda b, i: (l, 0, 0))],
        out_specs=(pl.BlockSpec((None, tm, PW), lambda b, i: (b, i, 0)),
                   pl.BlockSpec((None, GT_ROWS, tm), lambda b, i: (b, 0, i))),
        compiler_params=_cp(("arbitrary", "arbitrary")),
        name="inproj",
    )(x, mod, norm_g, w_p, wgt)


def _mlstm_kernel(qk_ref, v_ref, op_ref, gc_ref, gr_ref, cw_ref, bc_ref, br_ref, og_ref, tri_ref, y_ref,
                  prev_ref, qs_ref, ks_ref, c_ref, n_ref, m_ref, *, ts, cl):
    @pl.when(pl.program_id(1) == 0)
    def _():
        prev_ref[...] = jnp.zeros_like(prev_ref)
        c_ref[...] = jnp.zeros_like(c_ref)
        n_ref[...] = jnp.zeros_like(n_ref)
        m_ref[...] = jnp.zeros_like(m_ref)

    x = qk_ref[...]
    xc = jnp.concatenate([prev_ref[...], x], axis=0)
    cw = cw_ref[...]
    y = (cw[0:1] * xc[8:8 + ts] + cw[1:2] * xc[7:7 + ts] + cw[2:3] * xc[6:6 + ts] + cw[3:4] * xc[5:5 + ts])
    prev_ref[...] = x[ts - 8:ts]
    y = y * _sigmoid(y)
    qs_ref[...] = y[:, :GROUP]
    ks_ref[...] = y[:, GROUP:] * QK_SCALE

    tri = tri_ref[...]
    lower = lax.broadcasted_iota(jnp.int32, (cl, cl), 0) >= lax.broadcasted_iota(jnp.int32, (cl, cl), 1)

    def chunk(ci, carry):
        r0 = pl.multiple_of(ci * cl, cl)
        gc = gc_ref[pl.ds(r0, cl), :] + bc_ref[...]
        gr = gr_ref[0:8, pl.ds(r0, cl)] + br_ref[0:8, :]
        b_c = _dot_hi(tri, _log_sigmoid(gc))
        b_r = _dot_t_hi(_log_sigmoid(gr), tri)
        q = qs_ref[pl.ds(r0, cl), :]
        k = ks_ref[pl.ds(r0, cl), :]
        v = v_ref[pl.ds(r0, cl), :]
        outs = []
        for h in range(HEADS):
            sl = slice(h * HEAD_DIM, (h + 1) * HEAD_DIM)
            ig_col, b_col = gc[:, G_MI + h:G_MI + h + 1], b_c[:, G_MF + h:G_MF + h + 1]
            ig_row, b_row = gr[G_MI + h:G_MI + h + 1, :], b_r[G_MF + h:G_MF + h + 1, :]
            g = b_row[:, cl - 1:cl]
            m_loc = jnp.max(g - b_row + ig_row, axis=1, keepdims=True)
            ew_col = jnp.exp(g - b_col + ig_col - m_loc)
            qh, kh, vh = q[:, sl], k[:, sl], v[:, sl]
            c_chunk = lax.dot_general((vh * ew_col).astype(BF16), kh.astype(BF16), (((0,), (0,)), ((), ())),
                                      preferred_element_type=F32)
            n_chunk = jnp.sum(ew_col * kh, axis=0, keepdims=True)
            c_prev, n_prev, m_prev = c_ref[h], n_ref[h], m_ref[h][:, 0:1]
            d_log = jnp.where(lower, b_col - b_row + ig_row, NEG)
            m_inter = b_col + m_prev
            m_t = jnp.maximum(jnp.max(d_log, axis=1, keepdims=True), m_inter)
            s = _dot_t(qh, kh) * jnp.exp(d_log - m_t)
            inter_w = jnp.exp(m_inter - m_t)
            num = _dot(s, vh) + inter_w * _dot_t(qh, c_prev)
            den = jnp.sum(s, axis=1, keepdims=True) + inter_w * jnp.sum(qh * n_prev, axis=1, keepdims=True)
            hh = num / jnp.maximum(jnp.abs(den), jnp.exp(-m_t))
            m_new = jnp.maximum(g + m_prev, m_loc)
            a = jnp.exp(g + m_prev - m_new)
            bb = jnp.exp(m_loc - m_new)
            c_ref[h] = a * c_prev + bb * c_chunk
            n_ref[h] = a * n_prev + bb * n_chunk
            m_ref[h] = jnp.broadcast_to(m_new, (1, 128))
            outs.append(_rms_lanes(hh, og_ref[:, sl]))
        hcat = jnp.concatenate(outs, axis=1)
        y_ref[pl.ds(r0, cl), :] = (hcat * _sigmoid(op_ref[pl.ds(r0, cl), :])).astype(y_ref.dtype)
        return carry

    lax.fori_loop(0, ts // cl, chunk, 0)


def _mlstm(proj, gt, l, conv_w, bias_c, bias_r, out_g, tri):
    nb, s, _ = proj.shape
    ts, cl = MLSTM_TS, MLSTM_L
    kern = functools.partial(_mlstm_kernel, ts=ts, cl=cl)
    return pl.pallas_call(
        kern,
        out_shape=jax.ShapeDtypeStruct((nb, s, GROUP), BF16),
        grid=(nb, s // ts),
        in_specs=[pl.BlockSpec((None, ts, 2 * GROUP), lambda b, i: (b, i, C_MQK // (2 * GROUP))),
                  pl.BlockSpec((None, ts, GROUP), lambda b, i: (b, i, C_MV // GROUP)),
                  pl.BlockSpec((None, ts, GROUP), lambda b, i: (b, i, C_MO // GROUP)),
                  pl.BlockSpec((None, ts, 128), lambda b, i: (b, i, C_GATES // 128)),
                  pl.BlockSpec((None, GT_ROWS, ts), lambda b, i: (b, 0, i)),
                  pl.BlockSpec((None, 4, 2 * GROUP), lambda b, i: (l, 0, 0)),
                  pl.BlockSpec((None, 1, 128), lambda b, i: (l, 0, 0)),
                  pl.BlockSpec((None, GT_ROWS, 1), lambda b, i: (l, 0, 0)),
                  pl.BlockSpec((None, 1, GROUP), lambda b, i: (l, 0, 0)),
                  pl.BlockSpec((cl, cl), lambda b, i: (0, 0))],
        out_specs=pl.BlockSpec((None, ts, GROUP), lambda b, i: (b, i, 0)),
        scratch_shapes=[pltpu.VMEM((8, 2 * GROUP), F32),
                        pltpu.VMEM((ts, GROUP), F32),
                        pltpu.VMEM((ts, GROUP), F32),
                        pltpu.VMEM((HEADS, HEAD_DIM, HEAD_DIM), F32),
                        pltpu.VMEM((HEADS, 1, HEAD_DIM), F32),
                        pltpu.VMEM((HEADS, 1, 128), F32)],
        compiler_params=_cp(("arbitrary", "arbitrary")),
        name="mlstm",
    )(proj, proj, proj, proj, gt, conv_w, bias_c, bias_r, out_g, tri)


def _fox_kernel(q_ref, k_ref, v_ref, gr_ref, br_ref, qg_ref, kg_ref, bd_ref, tri_ref, y_ref,
                kn_ref, vx_ref, fn_ref, m_ref, acc_ref, *, t, s_len):
    qi = pl.program_id(1)
    bd = bd_ref[...]

    @pl.when(qi == 0)
    def _():
        kg = kg_ref[...]
        ones = jnp.ones((t, HEAD_DIM), BF16)
        for r in range(s_len // t):
            rows = slice(r * t, (r + 1) * t)
            kn = _rms_heads(k_ref[rows, :], bd, kg)
            vv = v_ref[rows, :]
            for h in range(HEADS):
                sl = slice(h * HEAD_DIM, (h + 1) * HEAD_DIM)
                kn_ref[h, rows, :] = kn[:, sl].astype(BF16)
                vx_ref[h, rows, :] = jnp.concatenate([vv[:, sl].astype(BF16), ones], axis=1)
        carry = jnp.zeros((8, 1), F32)
        for r in range(s_len // t):
            cols = slice(r * t, (r + 1) * t)
            lf = _log_sigmoid(gr_ref[G_FF:G_FF + 8, cols] + br_ref[G_FF:G_FF + 8, :])
            cs = _dot_t_hi(lf, tri_ref[...]) + carry
            fn_ref[:, cols] = -cs
            carry = cs[:, t - 1:t]

    qn = _rms_heads(q_ref[...], bd, qg_ref[...]) * QK_SCALE
    qh = [qn[:, h * HEAD_DIM:(h + 1) * HEAD_DIM].astype(BF16) for h in range(HEADS)]
    causal = lax.broadcasted_iota(jnp.int32, (t, t), 1) <= lax.broadcasted_iota(jnp.int32, (t, t), 0)
    m_ref[...] = jnp.full(m_ref.shape, NEG, F32)
    acc_ref[...] = jnp.zeros(acc_ref.shape, F32)

    def step(kb, masked):
        k0 = pl.multiple_of(kb * t, t)
        hs = range(HEADS)
        sc = [_dot_t(qh[h], kn_ref[h, pl.ds(k0, t), :]) + fn_ref[h:h + 1, pl.ds(k0, t)] for h in hs]
        if masked:
            sc = [jnp.where(causal, s, NEG) for s in sc]
        m_old = [m_ref[h] for h in hs]
        m_new = [jnp.maximum(m_old[h], jnp.max(sc[h], axis=1, keepdims=True)) for h in hs]
        p = [jnp.exp(sc[h] - jnp.concatenate([m_new[h]] * (t // 128), axis=1)).astype(BF16) for h in hs]
        pv = [jnp.dot(p[h], vx_ref[h, pl.ds(k0, t), :], preferred_element_type=F32) for h in hs]
        for h in hs:
            acc_ref[h] = jnp.exp(m_old[h] - m_new[h]) * acc_ref[h] + pv[h]
            m_ref[h] = m_new[h]

    def body(kb, c):
        step(kb, False)
        return c

    lax.fori_loop(0, qi, body, 0)
    step(qi, True)
    outs = []
    for h in range(HEADS):
        a = acc_ref[h]
        outs.append(a[:, :HEAD_DIM] / a[:, HEAD_DIM:])
    y_ref[...] = jnp.concatenate(outs, axis=1).astype(y_ref.dtype)


def _fox(proj, gt, l, bias_r, q_g, k_g, bd, tri):
    nb, s, _ = proj.shape
    t = ATT_T
    kern = functools.partial(_fox_kernel, t=t, s_len=s)
    return pl.pallas_call(
        kern,
        out_shape=jax.ShapeDtypeStruct((nb, s, GROUP), BF16),
        grid=(nb, s // t),
        in_specs=[pl.BlockSpec((None, t, GROUP), lambda b, i: (b, i, C_FQ // GROUP)),
                  pl.BlockSpec((None, s, GROUP), lambda b, i: (b, 0, C_FK // GROUP)),
                  pl.BlockSpec((None, s, GROUP), lambda b, i: (b, 0, C_FV // GROUP)),
                  pl.BlockSpec((None, GT_ROWS, s), lambda b, i: (b, 0, 0)),
                  pl.BlockSpec((None, GT_ROWS, 1), lambda b, i: (l, 0, 0)),
                  pl.BlockSpec((None, 1, GROUP), lambda b, i: (l, 0, 0)),
                  pl.BlockSpec((None, 1, GROUP), lambda b, i: (l, 0, 0)),
                  pl.BlockSpec((GROUP, GROUP), lambda b, i: (0, 0)),
                  pl.BlockSpec((t, t), lambda b, i: (0, 0))],
        out_specs=pl.BlockSpec((None, t, GROUP), lambda b, i: (b, i, 0)),
        scratch_shapes=[pltpu.VMEM((HEADS, s, HEAD_DIM), BF16),
                        pltpu.VMEM((HEADS, s, 2 * HEAD_DIM), BF16),
                        pltpu.VMEM((8, s), F32),
                        pltpu.VMEM((HEADS, t, 128), F32),
                        pltpu.VMEM((HEADS, t, 2 * HEAD_DIM), F32)],
        compiler_params=_cp(("arbitrary", "arbitrary")),
        name="fox",
    )(proj, proj, proj, gt, bias_r, q_g, k_g, bd, tri)


def _sb_kernel(q_ref, k_ref, v_ref, sm_ref, y_ref, kb_ref, vb_ref, rest_ref, acc_ref, *, t, s_len):
    qi = pl.program_id(1)

    @pl.when(qi == 0)
    def _():
        for r in range(s_len // t):
            rows = slice(r * t, (r + 1) * t)
            kk = k_ref[rows, :]
            vv = v_ref[rows, :]
            for h in range(HEADS):
                sl = slice(h * HEAD_DIM, (h + 1) * HEAD_DIM)
                kb_ref[h, rows, :] = kk[:, sl].astype(BF16)
                vb_ref[h, rows, :] = vv[:, sl].astype(BF16)

    qs = q_ref[...] * QK_SCALE
    qh = [qs[:, h * HEAD_DIM:(h + 1) * HEAD_DIM].astype(BF16) for h in range(HEADS)]
    strict = lax.broadcasted_iota(jnp.int32, (t, t), 1) < lax.broadcasted_iota(jnp.int32, (t, t), 0)
    after = sm_ref[...]
    rest_ref[...] = jnp.zeros(rest_ref.shape, F32)
    acc_ref[...] = jnp.zeros(acc_ref.shape, F32)

    def step(kb, masked):
        k0 = pl.multiple_of(kb * t, t)
        hs = range(HEADS)
        z = [_dot_t(qh[h], kb_ref[h, pl.ds(k0, t), :]) for h in hs]
        ls_pos = [jnp.minimum(z[h], 0.0) - jnp.log(1.0 + jnp.exp(-jnp.abs(z[h]))) for h in hs]
        lr = [ls_pos[h] - z[h] for h in hs]
        if masked:
            lr = [jnp.where(strict, v, 0.0) for v in lr]
        hi = [v.astype(BF16) for v in lr]
        lo = [(lr[h] - hi[h].astype(F32)).astype(BF16) for h in hs]
        suffix = [jnp.dot(hi[h], after, preferred_element_type=F32) + jnp.dot(lo[h], after, preferred_element_type=F32)
                  for h in hs]
        rest = [rest_ref[h] for h in hs]
        a = [jnp.exp(ls_pos[h] + suffix[h] + jnp.concatenate([rest[h]] * (t // 128), axis=1)) for h in hs]
        if masked:
            a = [jnp.where(strict, v, 0.0) for v in a]
        av = [jnp.dot(a[h].astype(BF16), vb_ref[h, pl.ds(k0, t), :], preferred_element_type=F32) for h in hs]
        for h in hs:
            acc_ref[h] += av[h]
            rest_ref[h] = rest[h] + jnp.sum(lr[h], axis=1, keepdims=True)

    step(qi, True)

    def body(j, c):
        step(qi - 1 - j, False)
        return c

    lax.fori_loop(0, qi, body, 0)
    y_ref[...] = jnp.concatenate([acc_ref[h] for h in range(HEADS)], axis=1).astype(y_ref.dtype)


def _sb(proj, after):
    nb, s, _ = proj.shape
    t = ATT_T
    kern = functools.partial(_sb_kernel, t=t, s_len=s)
    return pl.pallas_call(
        kern,
        out_shape=jax.ShapeDtypeStruct((nb, s, GROUP), BF16),
        grid=(nb, s // t),
        in_specs=[pl.BlockSpec((None, t, GROUP), lambda b, i: (b, i, C_SQ // GROUP)),
                  pl.BlockSpec((None, s, GROUP), lambda b, i: (b, 0, C_SK // GROUP)),
                  pl.BlockSpec((None, s, GROUP), lambda b, i: (b, 0, C_SV // GROUP)),
                  pl.BlockSpec((t, t), lambda b, i: (0, 0))],
        out_specs=pl.BlockSpec((None, t, GROUP), lambda b, i: (b, i, 0)),
        scratch_shapes=[pltpu.VMEM((HEADS, s, HEAD_DIM), BF16),
                        pltpu.VMEM((HEADS, s, HEAD_DIM), BF16),
                        pltpu.VMEM((HEADS, t, 128), F32),
                        pltpu.VMEM((HEADS, t, HEAD_DIM), F32)],
        compiler_params=_cp(("arbitrary", "arbitrary")),
        name="stick_breaking",
    )(proj, proj, proj, after)


def _nsa_kernel(q_ref, kvs_ref, kvw_ref, ak_ref, av_ref, gc_ref, gb_ref, qg_ref, kg_ref, pos_ref,
                wk_ref, wv_ref, bd_ref, tc_ref, ovt_ref, tsb_ref, twb_ref, cb_ref, y_ref,
                ksel_ref, vsel_ref, kwin_ref, vwin_ref, kcmp_ref, vcmp_ref, m_ref, acc_ref,
                *, tq, s_len, n_sel):
    qi = pl.program_id(1)
    nc = s_len // CMP_STRIDE

    @pl.when(qi == 0)
    def _():
        zk = jnp.zeros((WINDOW, HEAD_DIM), BF16)
        zv = jnp.zeros((WINDOW, 2 * HEAD_DIM), BF16)
        ksel_ref[0:WINDOW, :] = zk
        vsel_ref[0:WINDOW, :] = zv
        kwin_ref[0:WINDOW, :] = zk
        vwin_ref[0:WINDOW, :] = zv
        ones = jnp.ones((512, HEAD_DIM), BF16)
        for r in range(s_len // 512):
            rows = slice(r * 512, (r + 1) * 512)
            dst = slice(WINDOW + r * 512, WINDOW + (r + 1) * 512)
            kvs = kvs_ref[rows, :]
            kvw = kvw_ref[rows, :]
            ksel_ref[dst, :] = _rms_lanes(kvs[:, :HEAD_DIM], kg_ref[1:2, :]).astype(BF16)
            kwin_ref[dst, :] = _rms_lanes(kvw[:, :HEAD_DIM], kg_ref[2:3, :]).astype(BF16)
            vsel_ref[dst, :] = jnp.concatenate([kvs[:, HEAD_DIM:].astype(BF16), ones], axis=1)
            vwin_ref[dst, :] = jnp.concatenate([kvw[:, HEAD_DIM:].astype(BF16), ones], axis=1)
        half = CMP_STRIDE * HEAD_DIM
        for a_ref, w_ref, dst_ref, norm in ((ak_ref, wk_ref, kcmp_ref, True), (av_ref, wv_ref, vcmp_ref, False)):
            a = a_ref[...]
            pos_b = _dot_hi(pos_ref[...], w_ref[...])[0:1, :]
            p1 = _dot_hi(a, w_ref[0:half, :])
            p2 = _dot_hi(a, w_ref[half:2 * half, :])
            c = p1 + pltpu.roll(p2, nc - 1, 0) + pos_b
            dst_ref[...] = _rms_lanes(c, kg_ref[0:1, :]) if norm else c

    t0 = qi * tq
    qn = _rms_heads(q_ref[...], bd_ref[...], qg_ref[...]) * QK_SCALE
    qf = [qn[:, h * HEAD_DIM:(h + 1) * HEAD_DIM] for h in range(HEADS)]
    qh = [q.astype(BF16) for q in qf]

    o_cmp = []
    psum = jnp.zeros((tq, nc), F32)
    for h in range(HEADS):
        tc = tc_ref[h]
        sc = _dot_t_hi(qf[h], kcmp_ref[...]) + tc
        e = jnp.exp(sc - jnp.max(sc, axis=1, keepdims=True))
        p = jnp.where(tc > 0.5 * NEG, e / jnp.sum(e, axis=1, keepdims=True), 0.0)
        o_cmp.append(_dot(p, vcmp_ref[...]))
        psum = psum + p
    imp_t = _dot_t_hi(ovt_ref[...], psum)

    jj = lax.broadcasted_iota(jnp.int32, (128, tq), 0)
    tt = t0 + lax.broadcasted_iota(jnp.int32, (128, tq), 1)
    cur = tt // SEL_BLOCK
    forced = (jj == 0) | (jj == cur) | (jj == cur - 1)
    score = jnp.where(forced, -NEG, jnp.where(jj * SEL_BLOCK <= tt, imp_t, NEG))
    cnt = jnp.zeros((128, tq), F32)
    for i in range(n_sel):
        ri = score[i:i + 1, :]
        cnt = cnt + jnp.where(jj > i, (ri >= score).astype(F32), (ri > score).astype(F32))
    selb = (cnt < float(min(N_SEL_TOP, n_sel))).astype(F32).T.astype(BF16)

    m_ref[...] = jnp.full(m_ref.shape, NEG, F32)
    acc_ref[...] = jnp.zeros(acc_ref.shape, F32)
    jrow = lax.broadcasted_iota(jnp.int32, (128, tq), 0)
    jcol = lax.broadcasted_iota(jnp.int32, (128, tq), 1) // SEL_BLOCK

    def sel_chunk(kp0, near):
        jb = kp0 // SEL_BLOCK - WINDOW // SEL_BLOCK
        expand = (jrow == jb + jcol).astype(BF16)
        picked = jnp.dot(selb, expand, preferred_element_type=F32) > 0.5
        kblk = ksel_ref[pl.ds(kp0, tq), :]
        vblk = vsel_ref[pl.ds(kp0, tq), :]
        hs = range(HEADS)
        bias = [cb_ref[h] if near is None else tsb_ref[h, :, near * tq:(near + 1) * tq] for h in hs]
        sck = [_dot_t(qh[h], kblk) + jnp.where(picked, bias[h], NEG) for h in hs]
        m_old = [m_ref[h] for h in hs]
        m_new = [jnp.maximum(m_old[h], jnp.max(sck[h], axis=1, keepdims=True)) for h in hs]
        pk = [jnp.exp(sck[h] - jnp.concatenate([m_new[h]] * (tq // 128), axis=1)).astype(BF16) for h in hs]
        pv = [jnp.dot(pk[h], vblk, preferred_element_type=F32) for h in hs]
        for h in hs:
            acc_ref[h] = jnp.exp(m_old[h] - m_new[h]) * acc_ref[h] + pv[h]
            m_ref[h] = m_new[h]

    def far(ci, carry):
        sel_chunk(pl.multiple_of(WINDOW + ci * tq, tq), None)
        return carry

    lax.fori_loop(0, jnp.maximum(qi - WINDOW // tq, 0), far, 0)
    for r in range(NSA_NEAR // tq):
        @pl.when(qi + r >= WINDOW // tq)
        def _(r=r):
            sel_chunk(pl.multiple_of(t0 + r * tq, tq), r)

    kw = kwin_ref[pl.ds(pl.multiple_of(t0, tq), NSA_NEAR), :]
    vw = vwin_ref[pl.ds(pl.multiple_of(t0, tq), NSA_NEAR), :]
    in_seq = lax.broadcasted_iota(jnp.int32, (tq, NSA_NEAR), 1) >= WINDOW - t0
    g = _sigmoid(gc_ref[...] + gb_ref[...])
    outs = []
    for h in range(HEADS):
        sw = jnp.where(in_seq, _dot_t(qh[h], kw) + twb_ref[h], NEG)
        ow = _dot(jnp.exp(sw - jnp.max(sw, axis=1, keepdims=True)), vw)
        o_win = ow[:, :HEAD_DIM] / ow[:, HEAD_DIM:]
        a = acc_ref[h]
        o_sel = a[:, :HEAD_DIM] / a[:, HEAD_DIM:]
        c0 = G_NG + 3 * h
        outs.append(g[:, c0:c0 + 1] * o_cmp[h] + g[:, c0 + 1:c0 + 2] * o_sel + g[:, c0 + 2:c0 + 3] * o_win)
    y_ref[...] = jnp.concatenate(outs, axis=1).astype(y_ref.dtype)


def _nsa(proj, a_k, a_v, l, bias_c, q_g, k_g, pos, wk, wv, bd, tc, ovt, tsb, twb, cb):
    nb, s, _ = proj.shape
    tq = NSA_TQ
    nc = s // CMP_STRIDE
    kern = functools.partial(_nsa_kernel, tq=tq, s_len=s, n_sel=s // SEL_BLOCK)
    const2 = lambda b, i: (0, 0)
    const3 = lambda b, i: (0, 0, 0)
    lay3 = lambda b, i: (l, 0, 0)
    return pl.pallas_call(
        kern,
        out_shape=jax.ShapeDtypeStruct((nb, s, GROUP), BF16),
        grid=(nb, s // tq),
        in_specs=[pl.BlockSpec((None, tq, GROUP), lambda b, i: (b, i, C_NQ // GROUP)),
                  pl.BlockSpec((None, s, 128), lambda b, i: (b, 0, C_NKV // 128 + 1)),
                  pl.BlockSpec((None, s, 128), lambda b, i: (b, 0, C_NKV // 128 + 2)),
                  pl.BlockSpec((None, nc, CMP_STRIDE * HEAD_DIM), lambda b, i: (b, 0, 0)),
                  pl.BlockSpec((None, nc, CMP_STRIDE * HEAD_DIM), lambda b, i: (b, 0, 0)),
                  pl.BlockSpec((None, tq, 128), lambda b, i: (b, i, C_GATES // 128)),
                  pl.BlockSpec((None, 1, 128), lay3),
                  pl.BlockSpec((None, 1, GROUP), lay3),
                  pl.BlockSpec((None, 8, HEAD_DIM), lay3),
                  pl.BlockSpec((None, 8, CMP_BLOCK * HEAD_DIM), lay3),
                  pl.BlockSpec((None, CMP_BLOCK * HEAD_DIM, HEAD_DIM), lay3),
                  pl.BlockSpec((None, CMP_BLOCK * HEAD_DIM, HEAD_DIM), lay3),
                  pl.BlockSpec((GROUP, GROUP), const2),
                  pl.BlockSpec((HEADS, tq, nc), lambda b, i: (0, i, 0)),
                  pl.BlockSpec((128, nc), const2),
                  pl.BlockSpec((HEADS, tq, NSA_NEAR), const3),
                  pl.BlockSpec((HEADS, tq, NSA_NEAR), const3),
                  pl.BlockSpec(memory_space=pltpu.SMEM)],
        out_specs=pl.BlockSpec((None, tq, GROUP), lambda b, i: (b, i, 0)),
        scratch_shapes=[pltpu.VMEM((s + WINDOW, HEAD_DIM), BF16),
                        pltpu.VMEM((s + WINDOW, 2 * HEAD_DIM), BF16),
                        pltpu.VMEM((s + WINDOW, HEAD_DIM), BF16),
                        pltpu.VMEM((s + WINDOW, 2 * HEAD_DIM), BF16),
                        pltpu.VMEM((nc, HEAD_DIM), F32),
                        pltpu.VMEM((nc, HEAD_DIM), F32),
                        pltpu.VMEM((HEADS, tq, 128), F32),
                        pltpu.VMEM((HEADS, tq, 2 * HEAD_DIM), F32)],
        compiler_params=_cp(("arbitrary", "arbitrary")),
        name="nsa",
    )(proj, proj, proj, a_k, a_v, proj, bias_c, q_g, k_g, pos, wk, wv, bd, tc, ovt, tsb, twb, cb)


def _outproj_kernel(x_ref, ya_ref, yb_ref, yc_ref, yd_ref, w_ref, mod_ref, g_ref, xo_ref, h_ref):
    acc = jnp.dot(ya_ref[...], w_ref[0:GROUP, :], preferred_element_type=F32)
    acc += jnp.dot(yb_ref[...], w_ref[GROUP:2 * GROUP, :], preferred_element_type=F32)
    acc += jnp.dot(yc_ref[...], w_ref[2 * GROUP:3 * GROUP, :], preferred_element_type=F32)
    acc += jnp.dot(yd_ref[...], w_ref[3 * GROUP:4 * GROUP, :], preferred_element_type=F32)
    xn = x_ref[...] + mod_ref[2:3, :] * acc
    xo_ref[...] = xn
    h = _rms_lanes(xn, g_ref[...]) * (1.0 + mod_ref[4:5, :]) + mod_ref[3:4, :]
    h_ref[...] = h.astype(h_ref.dtype)


def _outproj(x, ys, mod, l, w_out, norm_g, h_dtype):
    nb, s, d = x.shape
    tm = 512
    yspec = pl.BlockSpec((None, tm, GROUP), lambda b, i: (b, i, 0))
    xspec = pl.BlockSpec((None, tm, d), lambda b, i: (b, i, 0))
    return pl.pallas_call(
        _outproj_kernel,
        out_shape=(jax.ShapeDtypeStruct((nb, s, d), F32), jax.ShapeDtypeStruct((nb, s, d), h_dtype)),
        grid=(nb, s // tm),
        in_specs=[xspec, yspec, yspec, yspec, yspec,
                  pl.BlockSpec((None, d, d), lambda b, i: (l, 0, 0)),
                  pl.BlockSpec((None, None, 6, d), lambda b, i: (l, b, 0, 0)),
                  pl.BlockSpec((None, 1, d), lambda b, i: (l, 0, 0))],
        out_specs=(xspec, xspec),
        compiler_params=_cp(("arbitrary", "arbitrary")),
        name="outproj",
    )(x, *ys, w_out, mod, norm_g)


def _ffn_kernel(h_ref, x_ref, mod_ref, wg_ref, wu_ref, wd_ref, o_ref, acc_ref, *, nf):
    f = pl.program_id(2)

    @pl.when(f == 0)
    def _():
        acc_ref[...] = jnp.zeros_like(acc_ref)

    h = h_ref[...]
    a = jnp.dot(h, wg_ref[...], preferred_element_type=F32)
    u = jnp.dot(h, wu_ref[...], preferred_element_type=F32)
    act = (a * _sigmoid(a) * u).astype(BF16)
    acc_ref[...] += jnp.dot(act, wd_ref[...], preferred_element_type=F32)

    @pl.when(f == nf - 1)
    def _():
        o_ref[...] = x_ref[...] + mod_ref[5:6, :] * acc_ref[...]


def _ffn(h, x, mod, l, li, wg, wu, wd):
    nb, s, d = x.shape
    dff = wg.shape[-1]
    tm, tf = FFN_TM, FFN_TF
    nf = dff // tf
    xspec = pl.BlockSpec((None, tm, d), lambda b, i, f: (b, i, 0))
    return pl.pallas_call(
        functools.partial(_ffn_kernel, nf=nf),
        out_shape=jax.ShapeDtypeStruct((nb, s, d), F32),
        grid=(nb, s // tm, nf),
        in_specs=[xspec, xspec,
                  pl.BlockSpec((None, None, 6, d), lambda b, i, f: (l, b, 0, 0)),
                  pl.BlockSpec((None, d, tf), lambda b, i, f: (li, 0, f)),
                  pl.BlockSpec((None, d, tf), lambda b, i, f: (li, 0, f)),
                  pl.BlockSpec((None, tf, d), lambda b, i, f: (li, f, 0))],
        out_specs=xspec,
        scratch_shapes=[pltpu.VMEM((tm, d), F32)],
        compiler_params=_cp(("arbitrary", "arbitrary", "arbitrary")),
        name="ffn_dense",
    )(h, x, mod, wg, wu, wd)


def _router_kernel(h_ref, w_ref, b_ref, o_ref):
    logits = _dot_hi(h_ref[...], w_ref[...]) + b_ref[...]
    lane = lax.broadcasted_iota(jnp.int32, logits.shape, 1).astype(F32)
    lg = jnp.where(lane < N_EXPERTS, logits, -3e38)
    m1 = jnp.max(lg, axis=1, keepdims=True)
    i1 = jnp.min(jnp.where(lg == m1, lane, 128.0), axis=1, keepdims=True)
    lg2 = jnp.where(lane == i1, -3e38, lg)
    m2 = jnp.max(lg2, axis=1, keepdims=True)
    i2 = jnp.min(jnp.where(lg2 == m2, lane, 128.0), axis=1, keepdims=True)
    e2 = jnp.exp(m2 - m1)
    w1 = 1.0 / (1.0 + e2)
    w2 = e2 / (1.0 + e2)
    o_ref[...] = jnp.where(lane == 0, i1, jnp.where(lane == 1, i2, jnp.where(lane == 2, w1, jnp.where(lane == 3, w2, 0.0))))


def _router(hf, rw, rb):
    t, d = hf.shape
    tm = 1024
    return pl.pallas_call(
        _router_kernel,
        out_shape=jax.ShapeDtypeStruct((t, 128), F32),
        grid=(t // tm,),
        in_specs=[pl.BlockSpec((tm, d), lambda i: (i, 0)),
                  pl.BlockSpec((d, 128), lambda i: (0, 0)),
                  pl.BlockSpec((1, 128), lambda i: (0, 0))],
        out_specs=pl.BlockSpec((tm, 128), lambda i: (i, 0)),
        compiler_params=_cp(("arbitrary",)),
        name="moe_router",
    )(hf, rw, rb)


def _gather_copy(src_hbm, row, dst_buf, slot, r, sem):
    return pltpu.make_async_copy(src_hbm.at[pl.ds(row, 1)], dst_buf.at[slot, pl.ds(r, 1)], sem.at[slot])


def _experts_kernel(be_ref, tok_ref, nv_ref, h_hbm, wg_ref, wu_ref, wd_ref, y_ref,
                    xbuf, xb_ref, acc_ref, sem, *, bm, nf):
    del be_ref
    i = pl.program_id(0)
    f = pl.program_id(1)
    nvalid = nv_ref[0]
    slot = i % 2

    def issue(blk, sl):
        def body(r, c):
            _gather_copy(h_hbm, tok_ref[blk * bm + r], xbuf, sl, r, sem).start()
            return c
        lax.fori_loop(0, bm, body, 0, unroll=8)

    def wait_all(sl):
        def body(r, c):
            _gather_copy(h_hbm, 0, xbuf, sl, r, sem).wait()
            return c
        lax.fori_loop(0, bm, body, 0, unroll=8)

    @pl.when((f == 0) & (i == 0))
    def _():
        issue(0, 0)

    @pl.when((f == 0) & (i < nvalid))
    def _():
        wait_all(slot)

        @pl.when(i + 1 < nvalid)
        def _():
            issue(i + 1, 1 - slot)

        xb_ref[...] = xbuf[slot].astype(BF16)
        acc_ref[...] = jnp.zeros_like(acc_ref)

    @pl.when(i < nvalid)
    def _():
        x = xb_ref[...]
        a = jnp.dot(x, wg_ref[...], preferred_element_type=F32)
        u = jnp.dot(x, wu_ref[...], preferred_element_type=F32)
        act = (a * _sigmoid(a) * u).astype(BF16)
        acc_ref[...] += jnp.dot(act, wd_ref[...], preferred_element_type=F32)

    @pl.when(f == nf - 1)
    def _():
        @pl.when(i < nvalid)
        def _():
            y_ref[...] = acc_ref[...]

        @pl.when(i >= nvalid)
        def _():
            y_ref[...] = jnp.zeros_like(y_ref)


def _experts(hf, blk_expert, row_tok, nvalid, wg, wu, wd, li):
    t, d = hf.shape
    bm, tf = MOE_BM, MOE_TF
    n_rows = row_tok.shape[0]
    n_blocks = n_rows // bm
    dff = wg.shape[-1]
    nf = dff // tf

    def fidx(i, f, nv):
        return jnp.where(i < nv[0], f, nf - 1)

    grid_spec = pltpu.PrefetchScalarGridSpec(
        num_scalar_prefetch=3,
        grid=(n_blocks, nf),
        in_specs=[pl.BlockSpec(memory_space=pl.ANY),
                  pl.BlockSpec((None, None, d, tf), lambda i, f, be, tok, nv: (li, be[i], 0, fidx(i, f, nv))),
                  pl.BlockSpec((None, None, d, tf), lambda i, f, be, tok, nv: (li, be[i], 0, fidx(i, f, nv))),
                  pl.BlockSpec((None, None, tf, d), lambda i, f, be, tok, nv: (li, be[i], fidx(i, f, nv), 0))],
        out_specs=pl.BlockSpec((bm, d), lambda i, f, be, tok, nv: (i, 0)),
        scratch_shapes=[pltpu.VMEM((2, bm, d), F32),
                        pltpu.VMEM((bm, d), BF16),
                        pltpu.VMEM((bm, d), F32),
                        pltpu.SemaphoreType.DMA((2,))],
    )
    return pl.pallas_call(
        functools.partial(_experts_kernel, bm=bm, nf=nf),
        out_shape=jax.ShapeDtypeStruct((n_rows, d), F32),
        grid_spec=grid_spec,
        compiler_params=_cp(("arbitrary", "arbitrary")),
        name="moe_experts",
    )(blk_expert, row_tok, nvalid, hf, wg, wu, wd)


def _combine_kernel(pos_ref, y_hbm, x_ref, mod_ref, rt_ref, o_ref, ybuf, sem, *, tm, s_len):
    b = pl.program_id(0)
    i = pl.program_id(1)
    base = (b * s_len + i * tm) * 2

    def issue(r, c):
        _gather_copy(y_hbm, pos_ref[base + 2 * r], ybuf, 0, r, sem).start()
        _gather_copy(y_hbm, pos_ref[base + 2 * r + 1], ybuf, 1, r, sem).start()
        return c

    def wait(r, c):
        _gather_copy(y_hbm, 0, ybuf, 0, r, sem).wait()
        _gather_copy(y_hbm, 0, ybuf, 1, r, sem).wait()
        return c

    lax.fori_loop(0, tm, issue, 0, unroll=8)
    lax.fori_loop(0, tm, wait, 0, unroll=8)
    rt = rt_ref[...]
    o_ref[...] = x_ref[...] + mod_ref[5:6, :] * (rt[:, 2:3] * ybuf[0] + rt[:, 3:4] * ybuf[1])


def _combine(y, pos, x, mod, route, l):
    nb, s, d = x.shape
    tm = 256
    grid_spec = pltpu.PrefetchScalarGridSpec(
        num_scalar_prefetch=1,
        grid=(nb, s // tm),
        in_specs=[pl.BlockSpec(memory_space=pl.ANY),
                  pl.BlockSpec((None, tm, d), lambda b, i, p: (b, i, 0)),
                  pl.BlockSpec((None, None, 6, d), lambda b, i, p: (l, b, 0, 0)),
                  pl.BlockSpec((None, tm, 128), lambda b, i, p: (b, i, 0))],
        out_specs=pl.BlockSpec((None, tm, d), lambda b, i, p: (b, i, 0)),
        scratch_shapes=[pltpu.VMEM((2, tm, d), F32), pltpu.SemaphoreType.DMA((2,))],
    )
    return pl.pallas_call(
        functools.partial(_combine_kernel, tm=tm, s_len=s),
        out_shape=jax.ShapeDtypeStruct((nb, s, d), F32),
        grid_spec=grid_spec,
        compiler_params=_cp(("arbitrary", "arbitrary")),
        name="moe_combine",
    )(pos, y, x, mod, route.reshape(nb, s, 128))


def _moe(hf32, x, mod, l, li, router_w, router_b, wg, wu, wd):
    nb, s, d = x.shape
    t = nb * s
    bm = MOE_BM
    hf = hf32.reshape(t, d)
    rw = jnp.zeros((d, 128), F32).at[:, :N_EXPERTS].set(router_w[li])
    rb = jnp.zeros((1, 128), F32).at[0, :N_EXPERTS].set(router_b[li])
    route = _router(hf, rw, rb)
    e_flat = route[:, 0:2].astype(jnp.int32).reshape(-1)
    n_assign = 2 * t
    onehot = (e_flat[:, None] == jnp.arange(N_EXPERTS, dtype=jnp.int32)[None, :]).astype(jnp.int32)
    csum = jnp.cumsum(onehot, axis=0)
    rank = jnp.sum(onehot * csum, axis=1) - 1
    counts = csum[-1]
    padded = (counts + bm - 1) // bm * bm
    cum_padded = jnp.cumsum(padded)
    pstart = cum_padded - padded
    dest = pstart[e_flat] + rank
    n_blocks = n_assign // bm + N_EXPERTS
    n_rows = n_blocks * bm
    tok_flat = jnp.arange(n_assign, dtype=jnp.int32) // 2
    row_tok = jnp.zeros((n_rows,), jnp.int32).at[dest].set(tok_flat)
    blk_start = jnp.arange(n_blocks, dtype=jnp.int32) * bm
    blk_expert = jnp.minimum(jnp.searchsorted(cum_padded, blk_start, side='right'), N_EXPERTS - 1).astype(jnp.int32)
    nvalid = (cum_padded[-1] // bm).astype(jnp.int32).reshape(1)
    y = _experts(hf, blk_expert, row_tok, nvalid, wg, wu, wd, li)
    return _combine(y, dest.astype(jnp.int32), x, mod, route, l)


def _t5_bucket(dist):
    n = np.maximum(dist, 0)
    max_exact = NUM_BUCKETS // 2
    large = max_exact + (np.log(np.maximum(n, 1).astype(np.float32) / max_exact)
                         / math.log(MAX_DISTANCE / max_exact) * (NUM_BUCKETS - max_exact)).astype(np.int32)
    return np.where(n < max_exact, n, np.minimum(large, NUM_BUCKETS - 1)).astype(np.int32)


def _bias_by_bucket(rb, bucket):
    ids = jnp.asarray(bucket.astype(np.int8))[None]
    out = jnp.zeros((rb.shape[1],) + bucket.shape, F32)
    for k in range(NUM_BUCKETS):
        out = jnp.where(ids == k, rb[k][:, None, None], out)
    return out


def _nsa_tables(rel_bias, s):
    tq = NSA_TQ
    nc = s // CMP_STRIDE
    n_cmp = (s - CMP_BLOCK) // CMP_STRIDE + 1
    rb = rel_bias.astype(F32)
    t = np.arange(s)[:, None]
    cmp_end = np.arange(nc)[None, :] * CMP_STRIDE + CMP_BLOCK - 1
    ok = (cmp_end <= t) & (np.arange(nc)[None, :] < n_cmp)
    tc = jnp.where(jnp.asarray(ok)[None], _bias_by_bucket(rb, _t5_bucket(t - cmp_end)), NEG)
    dist = np.arange(tq)[:, None] - (np.arange(NSA_NEAR)[None, :] - WINDOW)
    tb = _bias_by_bucket(rb, _t5_bucket(dist))
    tsb = jnp.where(jnp.asarray(dist >= 0)[None], tb, NEG)
    twb = jnp.where(jnp.asarray((dist >= 0) & (dist < WINDOW))[None], tb, NEG)
    far_bucket = int(_t5_bucket(np.array([WINDOW]))[0])
    cb = rb[far_bucket]
    n_sel = s // SEL_BLOCK
    cs = np.arange(nc)[None, :] * CMP_STRIDE
    ss = np.arange(128)[:, None] * SEL_BLOCK
    ov = np.clip(np.minimum(cs + CMP_BLOCK, ss + SEL_BLOCK) - np.maximum(cs, ss), 0, None).astype(np.float32) / CMP_BLOCK
    ov = ov * (np.arange(128)[:, None] < n_sel) * (np.arange(nc)[None, :] < n_cmp)
    return tc, jnp.asarray(ov, F32), tsb, twb, cb


def _prep_layer_params(w_in, mlstm_gate_b, fox_f_b, nsa_gate_b):
    depth, d, _ = w_in.shape
    small = [w_in[:, :, 1024:1032], w_in[:, :, 1800:1804], w_in[:, :, 3212:3224]]
    w_p = jnp.concatenate([w_in[:, :, 0:1024], w_in[:, :, 1032:1800], w_in[:, :, 1804:2572], w_in[:, :, 2572:3212]]
                          + small + [jnp.zeros((depth, d, 128 - 24), w_in.dtype)], axis=-1).astype(BF16)
    wgt = jnp.concatenate(small + [jnp.zeros((depth, d, GT_ROWS - 24), w_in.dtype)], axis=-1)
    wgt = jnp.transpose(wgt, (0, 2, 1)).astype(BF16)
    gate_b = jnp.concatenate([mlstm_gate_b, fox_f_b, nsa_gate_b], axis=-1).astype(F32)
    bias_c = jnp.zeros((depth, 1, 128), F32).at[:, 0, :24].set(gate_b)
    bias_r = jnp.zeros((depth, GT_ROWS, 1), F32).at[:, :24, 0].set(gate_b)
    return w_p, wgt, bias_c, bias_r


def kernel(x, c, rel_bias, ada_w, ada_b, norm1_g, norm2_g, w_in, w_out, mlstm_conv_w, mlstm_gate_b, mlstm_out_g,
           fox_f_b, fox_q_g, fox_k_g, nsa_q_g, nsa_k_g, nsa_cmp_pos, nsa_cmp_wk, nsa_cmp_wv, nsa_gate_b,
           ffn_wg, ffn_wu, ffn_wd, moe_router_w, moe_router_b, moe_wg, moe_wu, moe_wd):
    nb, s, d = x.shape
    depth = w_in.shape[0]
    assert d == D_MODEL and s % 512 == 0 and s // SEL_BLOCK <= 128

    w_p, wgt, bias_c, bias_r = _prep_layer_params(w_in, mlstm_gate_b, fox_f_b, nsa_gate_b)
    w_out_b = w_out.astype(BF16)
    n1g = norm1_g.reshape(depth, 1, d)
    n2g = norm2_g.reshape(depth, 1, d)
    m_out_g = mlstm_out_g.reshape(depth, 1, GROUP)
    fq_g = jnp.tile(fox_q_g, (1, HEADS)).reshape(depth, 1, GROUP)
    fk_g = jnp.tile(fox_k_g, (1, HEADS)).reshape(depth, 1, GROUP)
    nq_g = jnp.tile(nsa_q_g, (1, HEADS)).reshape(depth, 1, GROUP)
    nk_g = jnp.zeros((depth, 8, HEAD_DIM), F32).at[:, :3].set(nsa_k_g)
    pos8 = jnp.zeros((depth, 8, CMP_BLOCK * HEAD_DIM), F32).at[:, 0].set(nsa_cmp_pos.reshape(depth, -1))
    ffn_wg_b, ffn_wu_b, ffn_wd_b = ffn_wg.astype(BF16), ffn_wu.astype(BF16), ffn_wd.astype(BF16)
    moe_wg_b, moe_wu_b, moe_wd_b = moe_wg.astype(BF16), moe_wu.astype(BF16), moe_wd.astype(BF16)
    hid = np.arange(GROUP) // HEAD_DIM
    bd = jnp.asarray((hid[:, None] == hid[None, :]).astype(np.float32) / HEAD_DIM)
    tri_l = jnp.asarray(np.tril(np.ones((MLSTM_L, MLSTM_L), np.float32)))
    tri_t = jnp.asarray(np.tril(np.ones((ATT_T, ATT_T), np.float32)))
    after = jnp.asarray(np.tril(np.ones((ATT_T, ATT_T), np.float32), -1), BF16)
    tc, ovt, tsb, twb, cb = _nsa_tables(rel_bias, s)

    mod = _adaln(c, ada_w, ada_b)
    for l in range(depth):
        proj, gt = _inproj(x, mod, l, n1g, w_p, wgt)
        y_a = _mlstm(proj, gt, l, mlstm_conv_w, bias_c, bias_r, m_out_g, tri_l)
        y_b = _fox(proj, gt, l, bias_r, fq_g, fk_g, bd, tri_t)
        y_c = _sb(proj, after)
        a_k = proj[:, :, C_NKV:C_NKV + HEAD_DIM].reshape(nb, s // CMP_STRIDE, CMP_STRIDE * HEAD_DIM)
        a_v = proj[:, :, C_NKV + HEAD_DIM:C_NKV + 2 * HEAD_DIM].reshape(nb, s // CMP_STRIDE, CMP_STRIDE * HEAD_DIM)
        y_d = _nsa(proj, a_k, a_v, l, bias_c, nq_g, nk_g, pos8, nsa_cmp_wk, nsa_cmp_wv, bd, tc, ovt, tsb, twb, cb)
        if l % 2 == 0:
            x, h2 = _outproj(x, (y_a, y_b, y_c, y_d), mod, l, w_out_b, n2g, BF16)
            x = _ffn(h2, x, mod, l, l // 2, ffn_wg_b, ffn_wu_b, ffn_wd_b)
        else:
            x, h2 = _outproj(x, (y_a, y_b, y_c, y_d), mod, l, w_out_b, n2g, F32)
            x = _moe(h2, x, mod, l, l // 2, moe_router_w, moe_router_b, moe_wg_b, moe_wu_b, moe_wd_b)
    return x
```

```python
import functools
import math

import numpy as np
import jax
import jax.numpy as jnp
from jax import lax
from jax.experimental import pallas as pl
from jax.experimental.pallas import tpu as pltpu

F32 = jnp.float32
BF16 = jnp.bfloat16
HI = lax.Precision.HIGHEST

D_MODEL = 1024
HEADS = 4
HEAD_DIM = 64
GROUP = HEADS * HEAD_DIM
NORM_EPS = 1e-6
NEG = -1e30
QK_SCALE = HEAD_DIM ** -0.5
LOG2E = 1.4426950408889634
CMP_BLOCK = 32
CMP_STRIDE = 16
SEL_BLOCK = 64
N_SEL_TOP = 16
WINDOW = 512
NUM_BUCKETS = 32
MAX_DISTANCE = 128
N_EXPERTS = 8
VMEM_LIMIT = 56 * 1024 * 1024

PW = 3328
C_MQK, C_MV, C_MO = 0, 512, 768
C_FQ, C_FK, C_FV = 1024, 1280, 1536
C_SQ, C_SK, C_SV = 1792, 2048, 2304
C_NQ, C_NKV, C_GATES = 2560, 2816, 3200
G_MI, G_MF, G_FF, G_NG = 0, 4, 8, 12
GT_ROWS = 32

MLSTM_TS = 512
MLSTM_L = 128
ATT_T = 256
NSA_TQ = 256
NSA_NEAR = WINDOW + NSA_TQ
MOE_BM = 512
MOE_TF = 512
FFN_TM = 512
FFN_TF = 1408


def _cp(sem, vmem=VMEM_LIMIT):
    return pltpu.CompilerParams(dimension_semantics=sem, vmem_limit_bytes=vmem)


def _dot(a, b):
    return jnp.dot(a.astype(BF16), b.astype(BF16), preferred_element_type=F32)


def _dot_t(a, b):
    return lax.dot_general(a.astype(BF16), b.astype(BF16), (((1,), (1,)), ((), ())), preferred_element_type=F32)


def _dot_hi(a, b):
    return jnp.dot(a, b, precision=HI, preferred_element_type=F32)


def _dot_t_hi(a, b):
    return lax.dot_general(a, b, (((1,), (1,)), ((), ())), precision=HI, preferred_element_type=F32)


def _sigmoid(x):
    return 1.0 / (1.0 + jnp.exp(-x))


def _log_sigmoid(x):
    return jnp.minimum(x, 0.0) - jnp.log1p(jnp.exp(-jnp.abs(x)))


def _rms_lanes(x, g):
    return x * lax.rsqrt(jnp.mean(x * x, axis=-1, keepdims=True) + NORM_EPS) * g


def _split2(a):
    hi = a.astype(BF16)
    return hi, (a - hi.astype(F32)).astype(BF16)


def _dot_2x(a, b):
    hi, lo = _split2(a)
    bb = b.astype(BF16)
    return jnp.dot(hi, bb, preferred_element_type=F32) + jnp.dot(lo, bb, preferred_element_type=F32)


def _dot_t_2x(a, b):
    hi, lo = _split2(a)
    bb = b.astype(BF16)
    dn = (((1,), (1,)), ((), ()))
    return (lax.dot_general(hi, bb, dn, preferred_element_type=F32)
            + lax.dot_general(lo, bb, dn, preferred_element_type=F32))


def _rms_heads(x, bd, g):
    return x * lax.rsqrt(_dot_2x(x * x, bd) + NORM_EPS) * g


def _adaln_kernel(c_ref, w_ref, b_ref, o_ref):
    c = c_ref[...]
    o_ref[...] = _dot_hi(c * _sigmoid(c), w_ref[...]) + b_ref[...]


def _adaln(c, ada_w, ada_b):
    depth, d, six_d = ada_w.shape
    nb = c.shape[0]
    out = pl.pallas_call(
        _adaln_kernel,
        out_shape=jax.ShapeDtypeStruct((depth, nb, six_d), F32),
        grid=(depth, six_d // d),
        in_specs=[pl.BlockSpec((nb, d), lambda l, j: (0, 0)),
                  pl.BlockSpec((None, d, d), lambda l, j: (l, 0, j)),
                  pl.BlockSpec((None, 1, d), lambda l, j: (l, 0, j))],
        out_specs=pl.BlockSpec((None, nb, d), lambda l, j: (l, 0, j)),
        compiler_params=_cp(("arbitrary", "arbitrary")),
        name="adaln",
    )(c, ada_w, ada_b.reshape(depth, 1, six_d))
    return out.reshape(depth, nb, 6, d)


def _inproj_kernel(x_ref, mod_ref, g_ref, w_ref, wgt_ref, o_ref, gt_ref):
    h = _rms_lanes(x_ref[...], g_ref[...]) * (1.0 + mod_ref[1:2, :]) + mod_ref[0:1, :]
    hb = h.astype(BF16)
    o_ref[...] = jnp.dot(hb, w_ref[...], preferred_element_type=F32)
    gt_ref[...] = lax.dot_general(wgt_ref[...], hb, (((1,), (1,)), ((), ())), preferred_element_type=F32)


def _inproj(x, mod, l, norm_g, w_p, wgt):
    nb, s, d = x.shape
    tm = 256
    return pl.pallas_call(
        _inproj_kernel,
        out_shape=(jax.ShapeDtypeStruct((nb, s, PW), F32), jax.ShapeDtypeStruct((nb, GT_ROWS, s), F32)),
        grid=(nb, s // tm),
        in_specs=[pl.BlockSpec((None, tm, d), lambda b, i: (b, i, 0)),
                  pl.BlockSpec((None, None, 6, d), lambda b, i: (l, b, 0, 0)),
                  pl.BlockSpec((None, 1, d), lambda b, i: (l, 0, 0)),
                  pl.BlockSpec((None, d, PW), lambda b, i: (l, 0, 0)),
                  pl.BlockSpec((None, GT_ROWS, d), lambda b, i: (l, 0, 0))],
        out_specs=(pl.BlockSpec((None, tm, PW), lambda b, i: (b, i, 0)),
                   pl.BlockSpec((None, GT_ROWS, tm), lambda b, i: (b, 0, i))),
        compiler_params=_cp(("arbitrary", "arbitrary")),
        name="inproj",
    )(x, mod, norm_g, w_p, wgt)


def _mlstm_kernel(qk_ref, v_ref, op_ref, gc_ref, gr_ref, cw_ref, bc_ref, br_ref, og_ref, tri_ref, y_ref,
                  prev_ref, qs_ref, ks_ref, c_ref, n_ref, m_ref, *, ts, cl):
    @pl.when(pl.program_id(1) == 0)
    def _():
        prev_ref[...] = jnp.zeros_like(prev_ref)
        c_ref[...] = jnp.zeros_like(c_ref)
        n_ref[...] = jnp.zeros_like(n_ref)
        m_ref[...] = jnp.zeros_like(m_ref)

    x = qk_ref[...]
    xc = jnp.concatenate([prev_ref[...], x], axis=0)
    cw = cw_ref[...]
    y = (cw[0:1] * xc[8:8 + ts] + cw[1:2] * xc[7:7 + ts] + cw[2:3] * xc[6:6 + ts] + cw[3:4] * xc[5:5 + ts])
    prev_ref[...] = x[ts - 8:ts]
    y = y * _sigmoid(y)
    qs_ref[...] = y[:, :GROUP]
    ks_ref[...] = y[:, GROUP:] * QK_SCALE

    tri = tri_ref[...]
    lower = lax.broadcasted_iota(jnp.int32, (cl, cl), 0) >= lax.broadcasted_iota(jnp.int32, (cl, cl), 1)

    def chunk(ci, carry):
        r0 = pl.multiple_of(ci * cl, cl)
        gc = gc_ref[pl.ds(r0, cl), :] + bc_ref[...]
        gr = gr_ref[0:8, pl.ds(r0, cl)] + br_ref[0:8, :]
        lf_hi, lf_lo = _split2(_log_sigmoid(gc))
        trib = tri.astype(BF16)
        b_c = (jnp.dot(trib, lf_hi, preferred_element_type=F32)
               + jnp.dot(trib, lf_lo, preferred_element_type=F32))
        b_r = _dot_t_2x(_log_sigmoid(gr), tri)
        q = qs_ref[pl.ds(r0, cl), :]
        k = ks_ref[pl.ds(r0, cl), :]
        v = v_ref[pl.ds(r0, cl), :]
        outs = []
        for h in range(HEADS):
            sl = slice(h * HEAD_DIM, (h + 1) * HEAD_DIM)
            ig_col, b_col = gc[:, G_MI + h:G_MI + h + 1], b_c[:, G_MF + h:G_MF + h + 1]
            ig_row, b_row = gr[G_MI + h:G_MI + h + 1, :], b_r[G_MF + h:G_MF + h + 1, :]
            g = b_row[:, cl - 1:cl]
            m_loc = jnp.max(g - b_row + ig_row, axis=1, keepdims=True)
            ew_col = jnp.exp(g - b_col + ig_col - m_loc)
            qh, kh, vh = q[:, sl], k[:, sl], v[:, sl]
            c_chunk = lax.dot_general((vh * ew_col).astype(BF16), kh.astype(BF16), (((0,), (0,)), ((), ())),
                                      preferred_element_type=F32)
            n_chunk = jnp.sum(ew_col * kh, axis=0, keepdims=True)
            c_prev, n_prev, m_prev = c_ref[h], n_ref[h], m_ref[h][:, 0:1]
            d_log = jnp.where(lower, b_col - b_row + ig_row, NEG)
            m_inter = b_col + m_prev
            m_t = jnp.maximum(jnp.max(d_log, axis=1, keepdims=True), m_inter)
            s = _dot_t(qh, kh) * jnp.exp(d_log - m_t)
            inter_w = jnp.exp(m_inter - m_t)
            num = _dot(s, vh) + inter_w * _dot_t(qh, c_prev)
            den = jnp.sum(s, axis=1, keepdims=True) + inter_w * jnp.sum(qh * n_prev, axis=1, keepdims=True)
            hh = num / jnp.maximum(jnp.abs(den), jnp.exp(-m_t))
            m_new = jnp.maximum(g + m_prev, m_loc)
            a = jnp.exp(g + m_prev - m_new)
            bb = jnp.exp(m_loc - m_new)
            c_ref[h] = a * c_prev + bb * c_chunk
            n_ref[h] = a * n_prev + bb * n_chunk
            m_ref[h] = jnp.broadcast_to(m_new, (1, 128))
            outs.append(_rms_lanes(hh, og_ref[:, sl]))
        hcat = jnp.concatenate(outs, axis=1)
        y_ref[pl.ds(r0, cl), :] = (hcat * _sigmoid(op_ref[pl.ds(r0, cl), :])).astype(y_ref.dtype)
        return carry

    lax.fori_loop(0, ts // cl, chunk, 0)


def _mlstm(proj, gt, l, conv_w, bias_c, bias_r, out_g, tri):
    nb, s, _ = proj.shape
    ts, cl = MLSTM_TS, MLSTM_L
    kern = functools.partial(_mlstm_kernel, ts=ts, cl=cl)
    return pl.pallas_call(
        kern,
        out_shape=jax.ShapeDtypeStruct((nb, s, GROUP), BF16),
        grid=(nb, s // ts),
        in_specs=[pl.BlockSpec((None, ts, 2 * GROUP), lambda b, i: (b, i, C_MQK // (2 * GROUP))),
                  pl.BlockSpec((None, ts, GROUP), lambda b, i: (b, i, C_MV // GROUP)),
                  pl.BlockSpec((None, ts, GROUP), lambda b, i: (b, i, C_MO // GROUP)),
                  pl.BlockSpec((None, ts, 128), lambda b, i: (b, i, C_GATES // 128)),
                  pl.BlockSpec((None, GT_ROWS, ts), lambda b, i: (b, 0, i)),
                  pl.BlockSpec((None, 4, 2 * GROUP), lambda b, i: (l, 0, 0)),
                  pl.BlockSpec((None, 1, 128), lambda b, i: (l, 0, 0)),
                  pl.BlockSpec((None, GT_ROWS, 1), lambda b, i: (l, 0, 0)),
                  pl.BlockSpec((None, 1, GROUP), lambda b, i: (l, 0, 0)),
                  pl.BlockSpec((cl, cl), lambda b, i: (0, 0))],
        out_specs=pl.BlockSpec((None, ts, GROUP), lambda b, i: (b, i, 0)),
        scratch_shapes=[pltpu.VMEM((8, 2 * GROUP), F32),
                        pltpu.VMEM((ts, GROUP), F32),
                        pltpu.VMEM((ts, GROUP), F32),
                        pltpu.VMEM((HEADS, HEAD_DIM, HEAD_DIM), F32),
                        pltpu.VMEM((HEADS, 1, HEAD_DIM), F32),
                        pltpu.VMEM((HEADS, 1, 128), F32)],
        compiler_params=_cp(("arbitrary", "arbitrary")),
        name="mlstm",
    )(proj, proj, proj, proj, gt, conv_w, bias_c, bias_r, out_g, tri)


def _fox_kernel(q_ref, k_ref, v_ref, gr_ref, br_ref, qg_ref, kg_ref, bd_ref, tri_ref, y_ref,
                kn_ref, vx_ref, fn_ref, m_ref, acc_ref, *, t, s_len):
    qi = pl.program_id(1)
    bd = bd_ref[...]

    @pl.when(qi == 0)
    def _():
        kg = kg_ref[...]
        ones = jnp.ones((t, HEAD_DIM), BF16)
        for r in range(s_len // t):
            rows = slice(r * t, (r + 1) * t)
            kn = _rms_heads(k_ref[rows, :], bd, kg)
            vv = v_ref[rows, :]
            for h in range(HEADS):
                sl = slice(h * HEAD_DIM, (h + 1) * HEAD_DIM)
                kn_ref[h, rows, :] = kn[:, sl].astype(BF16)
                vx_ref[h, rows, :] = jnp.concatenate([vv[:, sl].astype(BF16), ones], axis=1)
        carry = jnp.zeros((8, 1), F32)
        for r in range(s_len // t):
            cols = slice(r * t, (r + 1) * t)
            lf = _log_sigmoid(gr_ref[G_FF:G_FF + 8, cols] + br_ref[G_FF:G_FF + 8, :])
            cs = _dot_t_hi(lf, tri_ref[...]) + carry
            fn_ref[:, cols] = -LOG2E * cs
            carry = cs[:, t - 1:t]

    qn = _rms_heads(q_ref[...], bd, qg_ref[...]) * (QK_SCALE * LOG2E)
    qh = [qn[:, h * HEAD_DIM:(h + 1) * HEAD_DIM].astype(BF16) for h in range(HEADS)]
    causal = lax.broadcasted_iota(jnp.int32, (t, t), 1) <= lax.broadcasted_iota(jnp.int32, (t, t), 0)
    m_ref[...] = jnp.full(m_ref.shape, NEG, F32)
    acc_ref[...] = jnp.zeros(acc_ref.shape, F32)

    def step(kb, masked):
        k0 = pl.multiple_of(kb * t, t)
        hs = range(HEADS)
        sc = [_dot_t(qh[h], kn_ref[h, pl.ds(k0, t), :]) + fn_ref[h:h + 1, pl.ds(k0, t)] for h in hs]
        if masked:
            sc = [jnp.where(causal, s, NEG) for s in sc]
        m_old = [m_ref[h] for h in hs]
        m_new = [jnp.maximum(m_old[h], jnp.max(sc[h], axis=1, keepdims=True)) for h in hs]
        p = [jnp.exp2(sc[h] - jnp.concatenate([m_new[h]] * (t // 128), axis=1)).astype(BF16) for h in hs]
        pv = [jnp.dot(p[h], vx_ref[h, pl.ds(k0, t), :], preferred_element_type=F32) for h in hs]
        for h in hs:
            acc_ref[h] = jnp.exp2(m_old[h] - m_new[h]) * acc_ref[h] + pv[h]
            m_ref[h] = m_new[h]

    def body(kp, c):
        step(2 * kp, False)
        step(2 * kp + 1, False)
        return c

    lax.fori_loop(0, qi // 2, body, 0)

    @pl.when(qi % 2 == 1)
    def _():
        step(qi - 1, False)

    step(qi, True)
    outs = []
    for h in range(HEADS):
        a = acc_ref[h]
        outs.append(a[:, :HEAD_DIM] / a[:, HEAD_DIM:])
    y_ref[...] = jnp.concatenate(outs, axis=1).astype(y_ref.dtype)


def _fox(proj, gt, l, bias_r, q_g, k_g, bd, tri):
    nb, s, _ = proj.shape
    t = ATT_T
    kern = functools.partial(_fox_kernel, t=t, s_len=s)
    return pl.pallas_call(
        kern,
        out_shape=jax.ShapeDtypeStruct((nb, s, GROUP), BF16),
        grid=(nb, s // t),
        in_specs=[pl.BlockSpec((None, t, GROUP), lambda b, i: (b, i, C_FQ // GROUP)),
                  pl.BlockSpec((None, s, GROUP), lambda b, i: (b, 0, C_FK // GROUP)),
                  pl.BlockSpec((None, s, GROUP), lambda b, i: (b, 0, C_FV // GROUP)),
                  pl.BlockSpec((None, GT_ROWS, s), lambda b, i: (b, 0, 0)),
                  pl.BlockSpec((None, GT_ROWS, 1), lambda b, i: (l, 0, 0)),
                  pl.BlockSpec((None, 1, GROUP), lambda b, i: (l, 0, 0)),
                  pl.BlockSpec((None, 1, GROUP), lambda b, i: (l, 0, 0)),
                  pl.BlockSpec((GROUP, GROUP), lambda b, i: (0, 0)),
                  pl.BlockSpec((t, t), lambda b, i: (0, 0))],
        out_specs=pl.BlockSpec((None, t, GROUP), lambda b, i: (b, i, 0)),
        scratch_shapes=[pltpu.VMEM((HEADS, s, HEAD_DIM), BF16),
                        pltpu.VMEM((HEADS, s, 2 * HEAD_DIM), BF16),
                        pltpu.VMEM((8, s), F32),
                        pltpu.VMEM((HEADS, t, 128), F32),
                        pltpu.VMEM((HEADS, t, 2 * HEAD_DIM), F32)],
        compiler_params=_cp(("arbitrary", "arbitrary")),
        name="fox",
    )(proj, proj, proj, gt, bias_r, q_g, k_g, bd, tri)


def _sb_kernel(q_ref, k_ref, v_ref, sm_ref, y_ref, kb_ref, vb_ref, rest_ref, acc_ref, *, t, s_len):
    qi = pl.program_id(1)

    @pl.when(qi == 0)
    def _():
        for r in range(s_len // t):
            rows = slice(r * t, (r + 1) * t)
            kk = k_ref[rows, :]
            vv = v_ref[rows, :]
            for h in range(HEADS):
                sl = slice(h * HEAD_DIM, (h + 1) * HEAD_DIM)
                kb_ref[h, rows, :] = kk[:, sl].astype(BF16)
                vb_ref[h, rows, :] = vv[:, sl].astype(BF16)

    qs = q_ref[...] * (QK_SCALE * LOG2E)
    qh = [qs[:, h * HEAD_DIM:(h + 1) * HEAD_DIM].astype(BF16) for h in range(HEADS)]
    strict = lax.broadcasted_iota(jnp.int32, (t, t), 1) < lax.broadcasted_iota(jnp.int32, (t, t), 0)
    from_here = sm_ref[...]
    rest_ref[...] = jnp.zeros(rest_ref.shape, F32)
    acc_ref[...] = jnp.zeros(acc_ref.shape, F32)

    def step(kb, masked):
        k0 = pl.multiple_of(kb * t, t)
        hs = range(HEADS)
        z = [_dot_t(qh[h], kb_ref[h, pl.ds(k0, t), :]) for h in hs]
        u = [jnp.maximum(z[h], 0.0) + jnp.log2(1.0 + jnp.exp2(-jnp.abs(z[h]))) for h in hs]
        if masked:
            u = [jnp.where(strict, v, 0.0) for v in u]
        hi_lo = [_split2(v) for v in u]
        incl = [jnp.dot(hi_lo[h][0], from_here, preferred_element_type=F32)
                + jnp.dot(hi_lo[h][1], from_here, preferred_element_type=F32) for h in hs]
        rest = [rest_ref[h] for h in hs]
        a = [jnp.exp2(z[h] - incl[h] - jnp.concatenate([rest[h]] * (t // 128), axis=1)) for h in hs]
        if masked:
            a = [jnp.where(strict, v, 0.0) for v in a]
        av = [jnp.dot(a[h].astype(BF16), vb_ref[h, pl.ds(k0, t), :], preferred_element_type=F32) for h in hs]
        for h in hs:
            acc_ref[h] += av[h]
            rest_ref[h] = rest[h] + incl[h][:, 0:1]

    step(qi, True)

    @pl.when(qi % 2 == 1)
    def _():
        step(qi - 1, False)

    def body(j, c):
        kb = 2 * (qi // 2 - 1 - j)
        step(kb + 1, False)
        step(kb, False)
        return c

    lax.fori_loop(0, qi // 2, body, 0)
    y_ref[...] = jnp.concatenate([acc_ref[h] for h in range(HEADS)], axis=1).astype(y_ref.dtype)


def _sb(proj, after):
    nb, s, _ = proj.shape
    t = ATT_T
    kern = functools.partial(_sb_kernel, t=t, s_len=s)
    return pl.pallas_call(
        kern,
        out_shape=jax.ShapeDtypeStruct((nb, s, GROUP), BF16),
        grid=(nb, s // t),
        in_specs=[pl.BlockSpec((None, t, GROUP), lambda b, i: (b, i, C_SQ // GROUP)),
                  pl.BlockSpec((None, s, GROUP), lambda b, i: (b, 0, C_SK // GROUP)),
                  pl.BlockSpec((None, s, GROUP), lambda b, i: (b, 0, C_SV // GROUP)),
                  pl.BlockSpec((t, t), lambda b, i: (0, 0))],
        out_specs=pl.BlockSpec((None, t, GROUP), lambda b, i: (b, i, 0)),
        scratch_shapes=[pltpu.VMEM((HEADS, s, HEAD_DIM), BF16),
                        pltpu.VMEM((HEADS, s, HEAD_DIM), BF16),
                        pltpu.VMEM((HEADS, t, 128), F32),
                        pltpu.VMEM((HEADS, t, HEAD_DIM), F32)],
        compiler_params=_cp(("arbitrary", "arbitrary")),
        name="stick_breaking",
    )(proj, proj, proj, after)


def _nsa_kernel(q_ref, kvs_ref, kvw_ref, ak_ref, av_ref, gc_ref, gb_ref, qg_ref, kg_ref, pos_ref,
                wk_ref, wv_ref, bd_ref, tc_ref, ovt_ref, tsb_ref, twb_ref, cb_ref, y_ref,
                ksel_ref, vsel_ref, kwin_ref, vwin_ref, kcmp_ref, vcmp_ref, m_ref, acc_ref,
                *, tq, s_len, n_sel):
    qi = pl.program_id(1)
    nc = s_len // CMP_STRIDE

    @pl.when(qi == 0)
    def _():
        zk = jnp.zeros((WINDOW, HEAD_DIM), BF16)
        zv = jnp.zeros((WINDOW, 2 * HEAD_DIM), BF16)
        ksel_ref[0:WINDOW, :] = zk
        vsel_ref[0:WINDOW, :] = zv
        kwin_ref[0:WINDOW, :] = zk
        vwin_ref[0:WINDOW, :] = zv
        ones = jnp.ones((512, HEAD_DIM), BF16)
        for r in range(s_len // 512):
            rows = slice(r * 512, (r + 1) * 512)
            dst = slice(WINDOW + r * 512, WINDOW + (r + 1) * 512)
            kvs = kvs_ref[rows, :]
            kvw = kvw_ref[rows, :]
            ksel_ref[dst, :] = _rms_lanes(kvs[:, :HEAD_DIM], kg_ref[1:2, :]).astype(BF16)
            kwin_ref[dst, :] = _rms_lanes(kvw[:, :HEAD_DIM], kg_ref[2:3, :]).astype(BF16)
            vsel_ref[dst, :] = jnp.concatenate([kvs[:, HEAD_DIM:].astype(BF16), ones], axis=1)
            vwin_ref[dst, :] = jnp.concatenate([kvw[:, HEAD_DIM:].astype(BF16), ones], axis=1)
        half = CMP_STRIDE * HEAD_DIM
        for a_ref, w_ref, dst_ref, norm in ((ak_ref, wk_ref, kcmp_ref, True), (av_ref, wv_ref, vcmp_ref, False)):
            a = a_ref[...]
            pos_b = _dot_hi(pos_ref[...], w_ref[...])[0:1, :]
            p1 = _dot_hi(a, w_ref[0:half, :])
            p2 = _dot_hi(a, w_ref[half:2 * half, :])
            c = p1 + pltpu.roll(p2, nc - 1, 0) + pos_b
            if norm:
                hi, lo = _split2(_rms_lanes(c, kg_ref[0:1, :]))
                kcmp_ref[0] = hi
                kcmp_ref[1] = lo
            else:
                dst_ref[...] = c.astype(BF16)

    t0 = qi * tq
    qn = _rms_heads(q_ref[...], bd_ref[...], qg_ref[...]) * (QK_SCALE * LOG2E)
    qf = [qn[:, h * HEAD_DIM:(h + 1) * HEAD_DIM] for h in range(HEADS)]
    qh = [q.astype(BF16) for q in qf]
    ql = [(qf[h] - qh[h].astype(F32)).astype(BF16) for h in range(HEADS)]

    hs = range(HEADS)
    k_hi, k_lo = kcmp_ref[0], kcmp_ref[1]
    dn = (((1,), (1,)), ((), ()))
    sc = [lax.dot_general(qh[h], k_hi, dn, preferred_element_type=F32)
          + lax.dot_general(ql[h], k_hi, dn, preferred_element_type=F32)
          + lax.dot_general(qh[h], k_lo, dn, preferred_element_type=F32) + tc_ref[h] for h in hs]
    e = [jnp.exp2(sc[h] - jnp.max(sc[h], axis=1, keepdims=True)) for h in hs]
    p = [jnp.where(tc_ref[h] > 0.5 * NEG, e[h] / jnp.sum(e[h], axis=1, keepdims=True), 0.0) for h in hs]
    o_cmp = [jnp.dot(p[h].astype(BF16), vcmp_ref[...], preferred_element_type=F32) for h in hs]
    ps_hi, ps_lo = _split2(p[0] + p[1] + p[2] + p[3])
    ovt = ovt_ref[...].astype(BF16)
    imp_t = (lax.dot_general(ovt, ps_hi, dn, preferred_element_type=F32)
             + lax.dot_general(ovt, ps_lo, dn, preferred_element_type=F32))

    jj = lax.broadcasted_iota(jnp.int32, (128, tq), 0)
    tt = t0 + lax.broadcasted_iota(jnp.int32, (128, tq), 1)
    cur = tt // SEL_BLOCK
    forced = (jj == 0) | (jj == cur) | (jj == cur - 1)
    score = jnp.where(forced, -NEG, jnp.where(jj * SEL_BLOCK <= tt, imp_t, NEG))
    cnt = jnp.zeros((128, tq), F32)
    for i in range(n_sel):
        ri = score[i:i + 1, :]
        cnt = cnt + jnp.where(jj > i, (ri >= score).astype(F32), (ri > score).astype(F32))
    selb = (cnt < float(min(N_SEL_TOP, n_sel))).astype(F32).T.astype(BF16)

    m_ref[...] = jnp.full(m_ref.shape, NEG, F32)
    acc_ref[...] = jnp.zeros(acc_ref.shape, F32)
    jrow = lax.broadcasted_iota(jnp.int32, (128, tq), 0)
    jcol = lax.broadcasted_iota(jnp.int32, (128, tq), 1) // SEL_BLOCK

    def sel_chunk(kp0, near):
        jb = kp0 // SEL_BLOCK - WINDOW // SEL_BLOCK
        expand = (jrow == jb + jcol).astype(BF16)
        picked = jnp.dot(selb, expand, preferred_element_type=F32) > 0.5
        kblk = ksel_ref[pl.ds(kp0, tq), :]
        vblk = vsel_ref[pl.ds(kp0, tq), :]
        hs = range(HEADS)
        bias = [cb_ref[h] if near is None else tsb_ref[h, :, near * tq:(near + 1) * tq] for h in hs]
        sck = [_dot_t(qh[h], kblk) + jnp.where(picked, bias[h], NEG) for h in hs]
        m_old = [m_ref[h] for h in hs]
        m_new = [jnp.maximum(m_old[h], jnp.max(sck[h], axis=1, keepdims=True)) for h in hs]
        pk = [jnp.exp2(sck[h] - jnp.concatenate([m_new[h]] * (tq // 128), axis=1)).astype(BF16) for h in hs]
        pv = [jnp.dot(pk[h], vblk, preferred_element_type=F32) for h in hs]
        for h in hs:
            acc_ref[h] = jnp.exp2(m_old[h] - m_new[h]) * acc_ref[h] + pv[h]
            m_ref[h] = m_new[h]

    def far(cp, carry):
        sel_chunk(pl.multiple_of(WINDOW + 2 * cp * tq, tq), None)
        sel_chunk(pl.multiple_of(WINDOW + (2 * cp + 1) * tq, tq), None)
        return carry

    n_far = jnp.maximum(qi - WINDOW // tq, 0)
    lax.fori_loop(0, n_far // 2, far, 0)

    @pl.when(n_far % 2 == 1)
    def _():
        sel_chunk(pl.multiple_of(WINDOW + (n_far - 1) * tq, tq), None)
    for r in range(NSA_NEAR // tq):
        @pl.when(qi + r >= WINDOW // tq)
        def _(r=r):
            sel_chunk(pl.multiple_of(t0 + r * tq, tq), r)

    kw = kwin_ref[pl.ds(pl.multiple_of(t0, tq), NSA_NEAR), :]
    vw = vwin_ref[pl.ds(pl.multiple_of(t0, tq), NSA_NEAR), :]
    in_seq = lax.broadcasted_iota(jnp.int32, (tq, NSA_NEAR), 1) >= WINDOW - t0
    g = _sigmoid(gc_ref[...] + gb_ref[...])
    sw = [jnp.where(in_seq, _dot_t(qh[h], kw) + twb_ref[h], NEG) for h in hs]
    pw = [jnp.exp2(sw[h] - jnp.max(sw[h], axis=1, keepdims=True)).astype(BF16) for h in hs]
    ows = [jnp.dot(pw[h], vw, preferred_element_type=F32) for h in hs]
    outs = []
    for h in hs:
        ow = ows[h]
        o_win = ow[:, :HEAD_DIM] / ow[:, HEAD_DIM:]
        a = acc_ref[h]
        o_sel = a[:, :HEAD_DIM] / a[:, HEAD_DIM:]
        c0 = G_NG + 3 * h
        outs.append(g[:, c0:c0 + 1] * o_cmp[h] + g[:, c0 + 1:c0 + 2] * o_sel + g[:, c0 + 2:c0 + 3] * o_win)
    y_ref[...] = jnp.concatenate(outs, axis=1).astype(y_ref.dtype)


def _nsa(proj, a_k, a_v, l, bias_c, q_g, k_g, pos, wk, wv, bd, tc, ovt, tsb, twb, cb):
    nb, s, _ = proj.shape
    tq = NSA_TQ
    nc = s // CMP_STRIDE
    kern = functools.partial(_nsa_kernel, tq=tq, s_len=s, n_sel=s // SEL_BLOCK)
    const2 = lambda b, i: (0, 0)
    const3 = lambda b, i: (0, 0, 0)
    lay3 = lambda b, i: (l, 0, 0)
    return pl.pallas_call(
        kern,
        out_shape=jax.ShapeDtypeStruct((nb, s, GROUP), BF16),
        grid=(nb, s // tq),
        in_specs=[pl.BlockSpec((None, tq, GROUP), lambda b, i: (b, i, C_NQ // GROUP)),
                  pl.BlockSpec((None, s, 128), lambda b, i: (b, 0, C_NKV // 128 + 1)),
                  pl.BlockSpec((None, s, 128), lambda b, i: (b, 0, C_NKV // 128 + 2)),
                  pl.BlockSpec((None, nc, CMP_STRIDE * HEAD_DIM), lambda b, i: (b, 0, 0)),
                  pl.BlockSpec((None, nc, CMP_STRIDE * HEAD_DIM), lambda b, i: (b, 0, 0)),
                  pl.BlockSpec((None, tq, 128), lambda b, i: (b, i, C_GATES // 128)),
                  pl.BlockSpec((None, 1, 128), lay3),
                  pl.BlockSpec((None, 1, GROUP), lay3),
                  pl.BlockSpec((None, 8, HEAD_DIM), lay3),
                  pl.BlockSpec((None, 8, CMP_BLOCK * HEAD_DIM), lay3),
                  pl.BlockSpec((None, CMP_BLOCK * HEAD_DIM, HEAD_DIM), lay3),
                  pl.BlockSpec((None, CMP_BLOCK * HEAD_DIM, HEAD_DIM), lay3),
                  pl.BlockSpec((GROUP, GROUP), const2),
                  pl.BlockSpec((HEADS, tq, nc), lambda b, i: (0, i, 0)),
                  pl.BlockSpec((128, nc), const2),
                  pl.BlockSpec((HEADS, tq, NSA_NEAR), const3),
                  pl.BlockSpec((HEADS, tq, NSA_NEAR), const3),
                  pl.BlockSpec(memory_space=pltpu.SMEM)],
        out_specs=pl.BlockSpec((None, tq, GROUP), lambda b, i: (b, i, 0)),
        scratch_shapes=[pltpu.VMEM((s + WINDOW, HEAD_DIM), BF16),
                        pltpu.VMEM((s + WINDOW, 2 * HEAD_DIM), BF16),
                        pltpu.VMEM((s + WINDOW, HEAD_DIM), BF16),
                        pltpu.VMEM((s + WINDOW, 2 * HEAD_DIM), BF16),
                        pltpu.VMEM((2, nc, HEAD_DIM), BF16),
                        pltpu.VMEM((nc, HEAD_DIM), BF16),
                        pltpu.VMEM((HEADS, tq, 128), F32),
                        pltpu.VMEM((HEADS, tq, 2 * HEAD_DIM), F32)],
        compiler_params=_cp(("arbitrary", "arbitrary")),
        name="nsa",
    )(proj, proj, proj, a_k, a_v, proj, bias_c, q_g, k_g, pos, wk, wv, bd, tc, ovt, tsb, twb, cb)


def _outproj_kernel(x_ref, ya_ref, yb_ref, yc_ref, yd_ref, w_ref, mod_ref, g_ref, xo_ref, h_ref):
    acc = jnp.dot(ya_ref[...], w_ref[0:GROUP, :], preferred_element_type=F32)
    acc += jnp.dot(yb_ref[...], w_ref[GROUP:2 * GROUP, :], preferred_element_type=F32)
    acc += jnp.dot(yc_ref[...], w_ref[2 * GROUP:3 * GROUP, :], preferred_element_type=F32)
    acc += jnp.dot(yd_ref[...], w_ref[3 * GROUP:4 * GROUP, :], preferred_element_type=F32)
    xn = x_ref[...] + mod_ref[2:3, :] * acc
    xo_ref[...] = xn
    h = _rms_lanes(xn, g_ref[...]) * (1.0 + mod_ref[4:5, :]) + mod_ref[3:4, :]
    h_ref[...] = h.astype(h_ref.dtype)


def _outproj(x, ys, mod, l, w_out, norm_g, h_dtype):
    nb, s, d = x.shape
    tm = 512
    yspec = pl.BlockSpec((None, tm, GROUP), lambda b, i: (b, i, 0))
    xspec = pl.BlockSpec((None, tm, d), lambda b, i: (b, i, 0))
    return pl.pallas_call(
        _outproj_kernel,
        out_shape=(jax.ShapeDtypeStruct((nb, s, d), F32), jax.ShapeDtypeStruct((nb, s, d), h_dtype)),
        grid=(nb, s // tm),
        in_specs=[xspec, yspec, yspec, yspec, yspec,
                  pl.BlockSpec((None, d, d), lambda b, i: (l, 0, 0)),
                  pl.BlockSpec((None, None, 6, d), lambda b, i: (l, b, 0, 0)),
                  pl.BlockSpec((None, 1, d), lambda b, i: (l, 0, 0))],
        out_specs=(xspec, xspec),
        compiler_params=_cp(("arbitrary", "arbitrary")),
        name="outproj",
    )(x, *ys, w_out, mod, norm_g)


def _ffn_kernel(h_ref, x_ref, mod_ref, wg_ref, wu_ref, wd_ref, o_ref, acc_ref, *, nf):
    f = pl.program_id(2)

    @pl.when(f == 0)
    def _():
        acc_ref[...] = jnp.zeros_like(acc_ref)

    h = h_ref[...]
    a = jnp.dot(h, wg_ref[...], preferred_element_type=F32)
    u = jnp.dot(h, wu_ref[...], preferred_element_type=F32)
    act = (a * _sigmoid(a) * u).astype(BF16)
    acc_ref[...] += jnp.dot(act, wd_ref[...], preferred_element_type=F32)

    @pl.when(f == nf - 1)
    def _():
        o_ref[...] = x_ref[...] + mod_ref[5:6, :] * acc_ref[...]


def _ffn(h, x, mod, l, li, wg, wu, wd):
    nb, s, d = x.shape
    dff = wg.shape[-1]
    tm, tf = FFN_TM, FFN_TF
    nf = dff // tf
    xspec = pl.BlockSpec((None, tm, d), lambda b, i, f: (b, i, 0))
    return pl.pallas_call(
        functools.partial(_ffn_kernel, nf=nf),
        out_shape=jax.ShapeDtypeStruct((nb, s, d), F32),
        grid=(nb, s // tm, nf),
        in_specs=[xspec, xspec,
                  pl.BlockSpec((None, None, 6, d), lambda b, i, f: (l, b, 0, 0)),
                  pl.BlockSpec((None, d, tf), lambda b, i, f: (li, 0, f)),
                  pl.BlockSpec((None, d, tf), lambda b, i, f: (li, 0, f)),
                  pl.BlockSpec((None, tf, d), lambda b, i, f: (li, f, 0))],
        out_specs=xspec,
        scratch_shapes=[pltpu.VMEM((tm, d), F32)],
        compiler_params=_cp(("arbitrary", "arbitrary", "arbitrary")),
        name="ffn_dense",
    )(h, x, mod, wg, wu, wd)


def _router_kernel(h_ref, w_ref, b_ref, o_ref):
    logits = _dot_hi(h_ref[...], w_ref[...]) + b_ref[...]
    lane = lax.broadcasted_iota(jnp.int32, logits.shape, 1).astype(F32)
    lg = jnp.where(lane < N_EXPERTS, logits, -3e38)
    m1 = jnp.max(lg, axis=1, keepdims=True)
    i1 = jnp.min(jnp.where(lg == m1, lane, 128.0), axis=1, keepdims=True)
    lg2 = jnp.where(lane == i1, -3e38, lg)
    m2 = jnp.max(lg2, axis=1, keepdims=True)
    i2 = jnp.min(jnp.where(lg2 == m2, lane, 128.0), axis=1, keepdims=True)
    e2 = jnp.exp(m2 - m1)
    w1 = 1.0 / (1.0 + e2)
    w2 = e2 / (1.0 + e2)
    o_ref[...] = jnp.where(lane == 0, i1, jnp.where(lane == 1, i2, jnp.where(lane == 2, w1, jnp.where(lane == 3, w2, 0.0))))


def _router(hf, rw, rb):
    t, d = hf.shape
    tm = 1024
    return pl.pallas_call(
        _router_kernel,
        out_shape=jax.ShapeDtypeStruct((t, 128), F32),
        grid=(t // tm,),
        in_specs=[pl.BlockSpec((tm, d), lambda i: (i, 0)),
                  pl.BlockSpec((d, 128), lambda i: (0, 0)),
                  pl.BlockSpec((1, 128), lambda i: (0, 0))],
        out_specs=pl.BlockSpec((tm, 128), lambda i: (i, 0)),
        compiler_params=_cp(("arbitrary",)),
        name="moe_router",
    )(hf, rw, rb)


def _gather_copy(src_hbm, row, dst_buf, slot, r, sem):
    return pltpu.make_async_copy(src_hbm.at[pl.ds(row, 1)], dst_buf.at[slot, pl.ds(r, 1)], sem.at[slot])


def _experts_kernel(be_ref, tok_ref, nv_ref, h_hbm, wg_ref, wu_ref, wd_ref, y_ref,
                    xbuf, xb_ref, acc_ref, sem, *, bm, nf):
    del be_ref
    i = pl.program_id(0)
    f = pl.program_id(1)
    nvalid = nv_ref[0]
    slot = i % 2

    def issue(blk, sl):
        def body(r, c):
            _gather_copy(h_hbm, tok_ref[blk * bm + r], xbuf, sl, r, sem).start()
            return c
        lax.fori_loop(0, bm, body, 0, unroll=8)

    def wait_all(sl):
        def body(r, c):
            _gather_copy(h_hbm, 0, xbuf, sl, r, sem).wait()
            return c
        lax.fori_loop(0, bm, body, 0, unroll=8)

    @pl.when((f == 0) & (i == 0))
    def _():
        issue(0, 0)

    @pl.when((f == 0) & (i < nvalid))
    def _():
        wait_all(slot)

        @pl.when(i + 1 < nvalid)
        def _():
            issue(i + 1, 1 - slot)

        xb_ref[...] = xbuf[slot].astype(BF16)
        acc_ref[...] = jnp.zeros_like(acc_ref)

    @pl.when(i < nvalid)
    def _():
        x = xb_ref[...]
        a = jnp.dot(x, wg_ref[...], preferred_element_type=F32)
        u = jnp.dot(x, wu_ref[...], preferred_element_type=F32)
        act = (a * _sigmoid(a) * u).astype(BF16)
        acc_ref[...] += jnp.dot(act, wd_ref[...], preferred_element_type=F32)

    @pl.when(f == nf - 1)
    def _():
        @pl.when(i < nvalid)
        def _():
            y_ref[...] = acc_ref[...]

        @pl.when(i >= nvalid)
        def _():
            y_ref[...] = jnp.zeros_like(y_ref)


def _experts(hf, blk_expert, row_tok, nvalid, wg, wu, wd, li):
    t, d = hf.shape
    bm, tf = MOE_BM, MOE_TF
    n_rows = row_tok.shape[0]
    n_blocks = n_rows // bm
    dff = wg.shape[-1]
    nf = dff // tf

    def fidx(i, f, nv):
        return jnp.where(i < nv[0], f, nf - 1)

    grid_spec = pltpu.PrefetchScalarGridSpec(
        num_scalar_prefetch=3,
        grid=(n_blocks, nf),
        in_specs=[pl.BlockSpec(memory_space=pl.ANY),
                  pl.BlockSpec((None, None, d, tf), lambda i, f, be, tok, nv: (li, be[i], 0, fidx(i, f, nv))),
                  pl.BlockSpec((None, None, d, tf), lambda i, f, be, tok, nv: (li, be[i], 0, fidx(i, f, nv))),
                  pl.BlockSpec((None, None, tf, d), lambda i, f, be, tok, nv: (li, be[i], fidx(i, f, nv), 0))],
        out_specs=pl.BlockSpec((bm, d), lambda i, f, be, tok, nv: (i, 0)),
        scratch_shapes=[pltpu.VMEM((2, bm, d), F32),
                        pltpu.VMEM((bm, d), BF16),
                        pltpu.VMEM((bm, d), F32),
                        pltpu.SemaphoreType.DMA((2,))],
    )
    return pl.pallas_call(
        functools.partial(_experts_kernel, bm=bm, nf=nf),
        out_shape=jax.ShapeDtypeStruct((n_rows, d), F32),
        grid_spec=grid_spec,
        compiler_params=_cp(("arbitrary", "arbitrary")),
        name="moe_experts",
    )(blk_expert, row_tok, nvalid, hf, wg, wu, wd)


def _combine_kernel(pos_ref, y_hbm, x_ref, mod_ref, rt_ref, o_ref, ybuf, sem, *, tm, s_len):
    b = pl.program_id(0)
    i = pl.program_id(1)
    base = (b * s_len + i * tm) * 2

    def issue(r, c):
        _gather_copy(y_hbm, pos_ref[base + 2 * r], ybuf, 0, r, sem).start()
        _gather_copy(y_hbm, pos_ref[base + 2 * r + 1], ybuf, 1, r, sem).start()
        return c

    def wait(r, c):
        _gather_copy(y_hbm, 0, ybuf, 0, r, sem).wait()
        _gather_copy(y_hbm, 0, ybuf, 1, r, sem).wait()
        return c

    lax.fori_loop(0, tm, issue, 0, unroll=8)
    lax.fori_loop(0, tm, wait, 0, unroll=8)
    rt = rt_ref[...]
    o_ref[...] = x_ref[...] + mod_ref[5:6, :] * (rt[:, 2:3] * ybuf[0] + rt[:, 3:4] * ybuf[1])


def _combine(y, pos, x, mod, route, l):
    nb, s, d = x.shape
    tm = 256
    grid_spec = pltpu.PrefetchScalarGridSpec(
        num_scalar_prefetch=1,
        grid=(nb, s // tm),
        in_specs=[pl.BlockSpec(memory_space=pl.ANY),
                  pl.BlockSpec((None, tm, d), lambda b, i, p: (b, i, 0)),
                  pl.BlockSpec((None, None, 6, d), lambda b, i, p: (l, b, 0, 0)),
                  pl.BlockSpec((None, tm, 128), lambda b, i, p: (b, i, 0))],
        out_specs=pl.BlockSpec((None, tm, d), lambda b, i, p: (b, i, 0)),
        scratch_shapes=[pltpu.VMEM((2, tm, d), F32), pltpu.SemaphoreType.DMA((2,))],
    )
    return pl.pallas_call(
        functools.partial(_combine_kernel, tm=tm, s_len=s),
        out_shape=jax.ShapeDtypeStruct((nb, s, d), F32),
        grid_spec=grid_spec,
        compiler_params=_cp(("arbitrary", "arbitrary")),
        name="moe_combine",
    )(pos, y, x, mod, route.reshape(nb, s, 128))


def _moe(hf32, x, mod, l, li, router_w, router_b, wg, wu, wd):
    nb, s, d = x.shape
    t = nb * s
    bm = MOE_BM
    hf = hf32.reshape(t, d)
    rw = jnp.zeros((d, 128), F32).at[:, :N_EXPERTS].set(router_w[li])
    rb = jnp.zeros((1, 128), F32).at[0, :N_EXPERTS].set(router_b[li])
    route = _router(hf, rw, rb)
    e_flat = route[:, 0:2].astype(jnp.int32).reshape(-1)
    n_assign = 2 * t
    onehot = (e_flat[:, None] == jnp.arange(N_EXPERTS, dtype=jnp.int32)[None, :]).astype(jnp.int32)
    csum = jnp.cumsum(onehot, axis=0)
    rank = jnp.sum(onehot * csum, axis=1) - 1
    counts = csum[-1]
    padded = (counts + bm - 1) // bm * bm
    cum_padded = jnp.cumsum(padded)
    pstart = cum_padded - padded
    dest = pstart[e_flat] + rank
    n_blocks = n_assign // bm + N_EXPERTS
    n_rows = n_blocks * bm
    tok_flat = jnp.arange(n_assign, dtype=jnp.int32) // 2
    row_tok = jnp.zeros((n_rows,), jnp.int32).at[dest].set(tok_flat)
    blk_start = jnp.arange(n_blocks, dtype=jnp.int32) * bm
    blk_expert = jnp.minimum(jnp.searchsorted(cum_padded, blk_start, side='right'), N_EXPERTS - 1).astype(jnp.int32)
    nvalid = (cum_padded[-1] // bm).astype(jnp.int32).reshape(1)
    y = _experts(hf, blk_expert, row_tok, nvalid, wg, wu, wd, li)
    return _combine(y, dest.astype(jnp.int32), x, mod, route, l)


def _t5_bucket(dist):
    n = np.maximum(dist, 0)
    max_exact = NUM_BUCKETS // 2
    large = max_exact + (np.log(np.maximum(n, 1).astype(np.float32) / max_exact)
                         / math.log(MAX_DISTANCE / max_exact) * (NUM_BUCKETS - max_exact)).astype(np.int32)
    return np.where(n < max_exact, n, np.minimum(large, NUM_BUCKETS - 1)).astype(np.int32)


def _bias_by_bucket(rb, bucket):
    ids = jnp.asarray(bucket.astype(np.int8))[None]
    out = jnp.zeros((rb.shape[1],) + bucket.shape, F32)
    for k in range(NUM_BUCKETS):
        out = jnp.where(ids == k, rb[k][:, None, None], out)
    return out


def _nsa_tables(rel_bias, s):
    tq = NSA_TQ
    nc = s // CMP_STRIDE
    n_cmp = (s - CMP_BLOCK) // CMP_STRIDE + 1
    rb = rel_bias.astype(F32) * LOG2E
    t = np.arange(s)[:, None]
    cmp_end = np.arange(nc)[None, :] * CMP_STRIDE + CMP_BLOCK - 1
    ok = (cmp_end <= t) & (np.arange(nc)[None, :] < n_cmp)
    tc = jnp.where(jnp.asarray(ok)[None], _bias_by_bucket(rb, _t5_bucket(t - cmp_end)), NEG)
    dist = np.arange(tq)[:, None] - (np.arange(NSA_NEAR)[None, :] - WINDOW)
    tb = _bias_by_bucket(rb, _t5_bucket(dist))
    tsb = jnp.where(jnp.asarray(dist >= 0)[None], tb, NEG)
    twb = jnp.where(jnp.asarray((dist >= 0) & (dist < WINDOW))[None], tb, NEG)
    far_bucket = int(_t5_bucket(np.array([WINDOW]))[0])
    cb = rb[far_bucket]
    n_sel = s // SEL_BLOCK
    cs = np.arange(nc)[None, :] * CMP_STRIDE
    ss = np.arange(128)[:, None] * SEL_BLOCK
    ov = np.clip(np.minimum(cs + CMP_BLOCK, ss + SEL_BLOCK) - np.maximum(cs, ss), 0, None).astype(np.float32) / CMP_BLOCK
    ov = ov * (np.arange(128)[:, None] < n_sel) * (np.arange(nc)[None, :] < n_cmp)
    return tc, jnp.asarray(ov, F32), tsb, twb, cb


def _prep_layer_params(w_in, mlstm_gate_b, fox_f_b, nsa_gate_b):
    depth, d, _ = w_in.shape
    small = [w_in[:, :, 1024:1032], w_in[:, :, 1800:1804], w_in[:, :, 3212:3224]]
    w_p = jnp.concatenate([w_in[:, :, 0:1024], w_in[:, :, 1032:1800], w_in[:, :, 1804:2572], w_in[:, :, 2572:3212]]
                          + small + [jnp.zeros((depth, d, 128 - 24), w_in.dtype)], axis=-1).astype(BF16)
    wgt = jnp.concatenate(small + [jnp.zeros((depth, d, GT_ROWS - 24), w_in.dtype)], axis=-1)
    wgt = jnp.transpose(wgt, (0, 2, 1)).astype(BF16)
    gate_b = jnp.concatenate([mlstm_gate_b, fox_f_b, nsa_gate_b], axis=-1).astype(F32)
    bias_c = jnp.zeros((depth, 1, 128), F32).at[:, 0, :24].set(gate_b)
    bias_r = jnp.zeros((depth, GT_ROWS, 1), F32).at[:, :24, 0].set(gate_b)
    return w_p, wgt, bias_c, bias_r


def kernel(x, c, rel_bias, ada_w, ada_b, norm1_g, norm2_g, w_in, w_out, mlstm_conv_w, mlstm_gate_b, mlstm_out_g,
           fox_f_b, fox_q_g, fox_k_g, nsa_q_g, nsa_k_g, nsa_cmp_pos, nsa_cmp_wk, nsa_cmp_wv, nsa_gate_b,
           ffn_wg, ffn_wu, ffn_wd, moe_router_w, moe_router_b, moe_wg, moe_wu, moe_wd):
    nb, s, d = x.shape
    depth = w_in.shape[0]
    assert d == D_MODEL and s % 512 == 0 and s // SEL_BLOCK <= 128

    w_p, wgt, bias_c, bias_r = _prep_layer_params(w_in, mlstm_gate_b, fox_f_b, nsa_gate_b)
    w_out_b = w_out.astype(BF16)
    n1g = norm1_g.reshape(depth, 1, d)
    n2g = norm2_g.reshape(depth, 1, d)
    m_out_g = mlstm_out_g.reshape(depth, 1, GROUP)
    fq_g = jnp.tile(fox_q_g, (1, HEADS)).reshape(depth, 1, GROUP)
    fk_g = jnp.tile(fox_k_g, (1, HEADS)).reshape(depth, 1, GROUP)
    nq_g = jnp.tile(nsa_q_g, (1, HEADS)).reshape(depth, 1, GROUP)
    nk_g = jnp.zeros((depth, 8, HEAD_DIM), F32).at[:, :3].set(nsa_k_g)
    pos8 = jnp.zeros((depth, 8, CMP_BLOCK * HEAD_DIM), F32).at[:, 0].set(nsa_cmp_pos.reshape(depth, -1))
    ffn_wg_b, ffn_wu_b, ffn_wd_b = ffn_wg.astype(BF16), ffn_wu.astype(BF16), ffn_wd.astype(BF16)
    moe_wg_b, moe_wu_b, moe_wd_b = moe_wg.astype(BF16), moe_wu.astype(BF16), moe_wd.astype(BF16)
    hid = np.arange(GROUP) // HEAD_DIM
    bd = jnp.asarray((hid[:, None] == hid[None, :]).astype(np.float32) / HEAD_DIM)
    tri_l = jnp.asarray(np.tril(np.ones((MLSTM_L, MLSTM_L), np.float32)))
    tri_t = jnp.asarray(np.tril(np.ones((ATT_T, ATT_T), np.float32)))
    from_here = jnp.asarray(np.tril(np.ones((ATT_T, ATT_T), np.float32)), BF16)
    tc, ovt, tsb, twb, cb = _nsa_tables(rel_bias, s)

    mod = _adaln(c, ada_w, ada_b)
    for l in range(depth):
        proj, gt = _inproj(x, mod, l, n1g, w_p, wgt)
        y_a = _mlstm(proj, gt, l, mlstm_conv_w, bias_c, bias_r, m_out_g, tri_l)
        y_b = _fox(proj, gt, l, bias_r, fq_g, fk_g, bd, tri_t)
        y_c = _sb(proj, from_here)
        a_k = proj[:, :, C_NKV:C_NKV + HEAD_DIM].reshape(nb, s // CMP_STRIDE, CMP_STRIDE * HEAD_DIM)
        a_v = proj[:, :, C_NKV + HEAD_DIM:C_NKV + 2 * HEAD_DIM].reshape(nb, s // CMP_STRIDE, CMP_STRIDE * HEAD_DIM)
        y_d = _nsa(proj, a_k, a_v, l, bias_c, nq_g, nk_g, pos8, nsa_cmp_wk, nsa_cmp_wv, bd, tc, ovt, tsb, twb, cb)
        if l % 2 == 0:
            x, h2 = _outproj(x, (y_a, y_b, y_c, y_d), mod, l, w_out_b, n2g, BF16)
            x = _ffn(h2, x, mod, l, l // 2, ffn_wg_b, ffn_wu_b, ffn_wd_b)
        else:
            x, h2 = _outproj(x, (y_a, y_b, y_c, y_d), mod, l, w_out_b, n2g, F32)
            x = _moe(h2, x, mod, l, l // 2, moe_router_w, moe_router_b, moe_wg_b, moe_wu_b, moe_wd_b)
    return x
```

```python
import functools
import math

import numpy as np
import jax
import jax.numpy as jnp
from jax import lax
from jax.experimental import pallas as pl
from jax.experimental.pallas import tpu as pltpu

F32 = jnp.float32
BF16 = jnp.bfloat16
HI = lax.Precision.HIGHEST

D_MODEL = 1024
HEADS = 4
HEAD_DIM = 64
GROUP = HEADS * HEAD_DIM
NORM_EPS = 1e-6
NEG = -1e30
QK_SCALE = HEAD_DIM ** -0.5
LOG2E = 1.4426950408889634
CMP_BLOCK = 32
CMP_STRIDE = 16
SEL_BLOCK = 64
N_SEL_TOP = 16
WINDOW = 512
NUM_BUCKETS = 32
MAX_DISTANCE = 128
N_EXPERTS = 8
VMEM_LIMIT = 56 * 1024 * 1024

PW = 3328
C_MQK, C_MV, C_MO = 0, 512, 768
C_FQ, C_FK, C_FV = 1024, 1280, 1536
C_SQ, C_SK, C_SV = 1792, 2048, 2304
C_NQ, C_NKV, C_GATES = 2560, 2816, 3200
G_MI, G_MF, G_FF, G_NG = 0, 4, 8, 12
GT_ROWS = 32

MLSTM_TS = 512
MLSTM_L = 128
ATT_T = 256
NSA_TQ = 256
NSA_NEAR = WINDOW + NSA_TQ
MOE_BM = 512
MOE_TF = 512
FFN_TM = 512
FFN_TF = 1408


def _cp(sem, vmem=VMEM_LIMIT):
    return pltpu.CompilerParams(dimension_semantics=sem, vmem_limit_bytes=vmem)


def _dot(a, b):
    return jnp.dot(a.astype(BF16), b.astype(BF16), preferred_element_type=F32)


def _dot_t(a, b):
    return lax.dot_general(a.astype(BF16), b.astype(BF16), (((1,), (1,)), ((), ())), preferred_element_type=F32)


def _dot_hi(a, b):
    return jnp.dot(a, b, precision=HI, preferred_element_type=F32)


def _dot_t_hi(a, b):
    return lax.dot_general(a, b, (((1,), (1,)), ((), ())), precision=HI, preferred_element_type=F32)


def _sigmoid(x):
    return 1.0 / (1.0 + jnp.exp(-x))


def _log_sigmoid(x):
    return jnp.minimum(x, 0.0) - jnp.log1p(jnp.exp(-jnp.abs(x)))


def _rms_lanes(x, g):
    return x * lax.rsqrt(jnp.mean(x * x, axis=-1, keepdims=True) + NORM_EPS) * g


def _split2(a):
    hi = a.astype(BF16)
    return hi, (a - hi.astype(F32)).astype(BF16)


def _dot_2x(a, b):
    hi, lo = _split2(a)
    bb = b.astype(BF16)
    return jnp.dot(hi, bb, preferred_element_type=F32) + jnp.dot(lo, bb, preferred_element_type=F32)


def _dot_t_2x(a, b):
    hi, lo = _split2(a)
    bb = b.astype(BF16)
    dn = (((1,), (1,)), ((), ()))
    return (lax.dot_general(hi, bb, dn, preferred_element_type=F32)
            + lax.dot_general(lo, bb, dn, preferred_element_type=F32))


def _rms_heads(x, bd, g):
    return x * lax.rsqrt(_dot_2x(x * x, bd) + NORM_EPS) * g


def _adaln_kernel(c_ref, w_ref, b_ref, o_ref):
    c = c_ref[...]
    o_ref[...] = _dot_hi(c * _sigmoid(c), w_ref[...]) + b_ref[...]


def _adaln(c, ada_w, ada_b):
    depth, d, six_d = ada_w.shape
    nb = c.shape[0]
    out = pl.pallas_call(
        _adaln_kernel,
        out_shape=jax.ShapeDtypeStruct((depth, nb, six_d), F32),
        grid=(depth, six_d // d),
        in_specs=[pl.BlockSpec((nb, d), lambda l, j: (0, 0)),
                  pl.BlockSpec((None, d, d), lambda l, j: (l, 0, j)),
                  pl.BlockSpec((None, 1, d), lambda l, j: (l, 0, j))],
        out_specs=pl.BlockSpec((None, nb, d), lambda l, j: (l, 0, j)),
        compiler_params=_cp(("arbitrary", "arbitrary")),
        name="adaln",
    )(c, ada_w, ada_b.reshape(depth, 1, six_d))
    return out.reshape(depth, nb, 6, d)


def _inproj_kernel(x_ref, mod_ref, g_ref, w_ref, wgt_ref, o_ref, gt_ref):
    h = _rms_lanes(x_ref[...], g_ref[...]) * (1.0 + mod_ref[1:2, :]) + mod_ref[0:1, :]
    hb = h.astype(BF16)
    o_ref[...] = jnp.dot(hb, w_ref[...], preferred_element_type=F32)
    gt_ref[...] = lax.dot_general(wgt_ref[...], hb, (((1,), (1,)), ((), ())), preferred_element_type=F32)


def _inproj(x, mod, l, norm_g, w_p, wgt):
    nb, s, d = x.shape
    tm = 256
    return pl.pallas_call(
        _inproj_kernel,
        out_shape=(jax.ShapeDtypeStruct((nb, s, PW), F32), jax.ShapeDtypeStruct((nb, GT_ROWS, s), F32)),
        grid=(nb, s // tm),
        in_specs=[pl.BlockSpec((None, tm, d), lambda b, i: (b, i, 0)),
                  pl.BlockSpec((None, None, 6, d), lambda b, i: (l, b, 0, 0)),
                  pl.BlockSpec((None, 1, d), lambda b, i: (l, 0, 0)),
                  pl.BlockSpec((None, d, PW), lambda b, i: (l, 0, 0)),
                  pl.BlockSpec((None, GT_ROWS, d), lambda b, i: (l, 0, 0))],
        out_specs=(pl.BlockSpec((None, tm, PW), lambda b, i: (b, i, 0)),
                   pl.BlockSpec((None, GT_ROWS, tm), lambda b, i: (b, 0, i))),
        compiler_params=_cp(("arbitrary", "arbitrary")),
        name="inproj",
    )(x, mod, norm_g, w_p, wgt)


def _mlstm_kernel(qk_ref, v_ref, op_ref, gc_ref, gr_ref, cw_ref, bc_ref, br_ref, og_ref, tri_ref, y_ref,
                  prev_ref, qs_ref, ks_ref, c_ref, n_ref, m_ref, *, ts, cl):
    @pl.when(pl.program_id(1) == 0)
    def _():
        prev_ref[...] = jnp.zeros_like(prev_ref)
        c_ref[...] = jnp.zeros_like(c_ref)
        n_ref[...] = jnp.zeros_like(n_ref)
        m_ref[...] = jnp.zeros_like(m_ref)

    x = qk_ref[...]
    xc = jnp.concatenate([prev_ref[...], x], axis=0)
    cw = cw_ref[...]
    y = (cw[0:1] * xc[8:8 + ts] + cw[1:2] * xc[7:7 + ts] + cw[2:3] * xc[6:6 + ts] + cw[3:4] * xc[5:5 + ts])
    prev_ref[...] = x[ts - 8:ts]
    y = y * _sigmoid(y)
    qs_ref[...] = y[:, :GROUP]
    ks_ref[...] = y[:, GROUP:] * QK_SCALE

    tri = tri_ref[...]
    lower = lax.broadcasted_iota(jnp.int32, (cl, cl), 0) >= lax.broadcasted_iota(jnp.int32, (cl, cl), 1)

    def chunk(ci, carry):
        r0 = pl.multiple_of(ci * cl, cl)
        gc = gc_ref[pl.ds(r0, cl), :] + bc_ref[...]
        gr = gr_ref[0:8, pl.ds(r0, cl)] + br_ref[0:8, :]
        lf_hi, lf_lo = _split2(_log_sigmoid(gc))
        trib = tri.astype(BF16)
        b_c = (jnp.dot(trib, lf_hi, preferred_element_type=F32)
               + jnp.dot(trib, lf_lo, preferred_element_type=F32))
        b_r = _dot_t_2x(_log_sigmoid(gr), tri)
        q = qs_ref[pl.ds(r0, cl), :]
        k = ks_ref[pl.ds(r0, cl), :]
        v = v_ref[pl.ds(r0, cl), :]
        hs = range(HEADS)
        sls = [slice(h * HEAD_DIM, (h + 1) * HEAD_DIM) for h in hs]
        ig_col = [gc[:, G_MI + h:G_MI + h + 1] for h in hs]
        b_col = [b_c[:, G_MF + h:G_MF + h + 1] for h in hs]
        ig_row = [gr[G_MI + h:G_MI + h + 1, :] for h in hs]
        b_row = [b_r[G_MF + h:G_MF + h + 1, :] for h in hs]
        qh = [q[:, sl] for sl in sls]
        kf = [k[:, sl] for sl in sls]
        kh = [x.astype(BF16) for x in kf]
        vh = [v[:, sl] for sl in sls]
        qb = [x.astype(BF16) for x in qh]
        qk = [lax.dot_general(qb[h], kh[h], (((1,), (1,)), ((), ())), preferred_element_type=F32) for h in hs]
        c_prev = [c_ref[h] for h in hs]
        n_prev = [n_ref[h] for h in hs]
        m_prev = [m_ref[h][:, 0:1] for h in hs]
        qc = [_dot_t(qb[h], c_prev[h]) for h in hs]
        g = [b_row[h][:, cl - 1:cl] for h in hs]
        m_loc = [jnp.max(g[h] - b_row[h] + ig_row[h], axis=1, keepdims=True) for h in hs]
        ew_col = [jnp.exp(g[h] - b_col[h] + ig_col[h] - m_loc[h]) for h in hs]
        c_chunk = [lax.dot_general((vh[h] * ew_col[h]).astype(BF16), kh[h], (((0,), (0,)), ((), ())),
                                   preferred_element_type=F32) for h in hs]
        n_chunk = [jnp.sum(ew_col[h] * kf[h], axis=0, keepdims=True) for h in hs]
        d_log = [jnp.where(lower, b_col[h] - b_row[h] + ig_row[h], NEG) for h in hs]
        m_inter = [b_col[h] + m_prev[h] for h in hs]
        m_t = [jnp.maximum(jnp.max(d_log[h], axis=1, keepdims=True), m_inter[h]) for h in hs]
        s = [qk[h] * jnp.exp(d_log[h] - m_t[h]) for h in hs]
        inter_w = [jnp.exp(m_inter[h] - m_t[h]) for h in hs]
        sv = [_dot(s[h], vh[h]) for h in hs]
        outs = []
        for h in hs:
            num = sv[h] + inter_w[h] * qc[h]
            den = (jnp.sum(s[h], axis=1, keepdims=True)
                   + inter_w[h] * jnp.sum(qh[h] * n_prev[h], axis=1, keepdims=True))
            hh = num / jnp.maximum(jnp.abs(den), jnp.exp(-m_t[h]))
            m_new = jnp.maximum(g[h] + m_prev[h], m_loc[h])
            a = jnp.exp(g[h] + m_prev[h] - m_new)
            bb = jnp.exp(m_loc[h] - m_new)
            c_ref[h] = a * c_prev[h] + bb * c_chunk[h]
            n_ref[h] = a * n_prev[h] + bb * n_chunk[h]
            m_ref[h] = jnp.broadcast_to(m_new, (1, 128))
            outs.append(_rms_lanes(hh, og_ref[:, sls[h]]))
        hcat = jnp.concatenate(outs, axis=1)
        y_ref[pl.ds(r0, cl), :] = (hcat * _sigmoid(op_ref[pl.ds(r0, cl), :])).astype(y_ref.dtype)
        return carry

    lax.fori_loop(0, ts // cl, chunk, 0)


def _mlstm(proj, gt, l, conv_w, bias_c, bias_r, out_g, tri):
    nb, s, _ = proj.shape
    ts, cl = MLSTM_TS, MLSTM_L
    kern = functools.partial(_mlstm_kernel, ts=ts, cl=cl)
    return pl.pallas_call(
        kern,
        out_shape=jax.ShapeDtypeStruct((nb, s, GROUP), BF16),
        grid=(nb, s // ts),
        in_specs=[pl.BlockSpec((None, ts, 2 * GROUP), lambda b, i: (b, i, C_MQK // (2 * GROUP))),
                  pl.BlockSpec((None, ts, GROUP), lambda b, i: (b, i, C_MV // GROUP)),
                  pl.BlockSpec((None, ts, GROUP), lambda b, i: (b, i, C_MO // GROUP)),
                  pl.BlockSpec((None, ts, 128), lambda b, i: (b, i, C_GATES // 128)),
                  pl.BlockSpec((None, GT_ROWS, ts), lambda b, i: (b, 0, i)),
                  pl.BlockSpec((None, 4, 2 * GROUP), lambda b, i: (l, 0, 0)),
                  pl.BlockSpec((None, 1, 128), lambda b, i: (l, 0, 0)),
                  pl.BlockSpec((None, GT_ROWS, 1), lambda b, i: (l, 0, 0)),
                  pl.BlockSpec((None, 1, GROUP), lambda b, i: (l, 0, 0)),
                  pl.BlockSpec((cl, cl), lambda b, i: (0, 0))],
        out_specs=pl.BlockSpec((None, ts, GROUP), lambda b, i: (b, i, 0)),
        scratch_shapes=[pltpu.VMEM((8, 2 * GROUP), F32),
                        pltpu.VMEM((ts, GROUP), F32),
                        pltpu.VMEM((ts, GROUP), F32),
                        pltpu.VMEM((HEADS, HEAD_DIM, HEAD_DIM), F32),
                        pltpu.VMEM((HEADS, 1, HEAD_DIM), F32),
                        pltpu.VMEM((HEADS, 1, 128), F32)],
        compiler_params=_cp(("arbitrary", "arbitrary")),
        name="mlstm",
    )(proj, proj, proj, proj, gt, conv_w, bias_c, bias_r, out_g, tri)


def _fox_kernel(q_ref, k_ref, v_ref, gr_ref, br_ref, qg_ref, kg_ref, bd_ref, tri_ref, y_ref,
                kn_ref, vx_ref, fn_ref, m_ref, acc_ref, *, t, s_len):
    qi = pl.program_id(1)
    bd = bd_ref[...]

    @pl.when(qi == 0)
    def _():
        kg = kg_ref[...]
        ones = jnp.ones((t, HEAD_DIM), BF16)
        for r in range(s_len // t):
            rows = slice(r * t, (r + 1) * t)
            kn = _rms_heads(k_ref[rows, :], bd, kg)
            vv = v_ref[rows, :]
            for h in range(HEADS):
                sl = slice(h * HEAD_DIM, (h + 1) * HEAD_DIM)
                kn_ref[h, rows, :] = kn[:, sl].astype(BF16)
                vx_ref[h, rows, :] = jnp.concatenate([vv[:, sl].astype(BF16), ones], axis=1)
        carry = jnp.zeros((8, 1), F32)
        for r in range(s_len // t):
            cols = slice(r * t, (r + 1) * t)
            lf = _log_sigmoid(gr_ref[G_FF:G_FF + 8, cols] + br_ref[G_FF:G_FF + 8, :])
            cs = _dot_t_hi(lf, tri_ref[...]) + carry
            fn_ref[:, cols] = -LOG2E * cs
            carry = cs[:, t - 1:t]

    qn = _rms_heads(q_ref[...], bd, qg_ref[...]) * (QK_SCALE * LOG2E)
    qh = [qn[:, h * HEAD_DIM:(h + 1) * HEAD_DIM].astype(BF16) for h in range(HEADS)]
    causal = lax.broadcasted_iota(jnp.int32, (t, t), 1) <= lax.broadcasted_iota(jnp.int32, (t, t), 0)
    m_ref[...] = jnp.full(m_ref.shape, NEG, F32)
    acc_ref[...] = jnp.zeros(acc_ref.shape, F32)

    def step(kb, masked):
        k0 = pl.multiple_of(kb * t, t)
        hs = range(HEADS)
        sc = [_dot_t(qh[h], kn_ref[h, pl.ds(k0, t), :]) + fn_ref[h:h + 1, pl.ds(k0, t)] for h in hs]
        if masked:
            sc = [jnp.where(causal, s, NEG) for s in sc]
        m_old = [m_ref[h] for h in hs]
        m_new = [jnp.maximum(m_old[h], jnp.max(sc[h], axis=1, keepdims=True)) for h in hs]
        p = [jnp.exp2(sc[h] - jnp.concatenate([m_new[h]] * (t // 128), axis=1)).astype(BF16) for h in hs]
        pv = [jnp.dot(p[h], vx_ref[h, pl.ds(k0, t), :], preferred_element_type=F32) for h in hs]
        for h in hs:
            acc_ref[h] = jnp.exp2(m_old[h] - m_new[h]) * acc_ref[h] + pv[h]
            m_ref[h] = m_new[h]

    def body(kp, c):
        step(2 * kp, False)
        step(2 * kp + 1, False)
        return c

    lax.fori_loop(0, qi // 2, body, 0)

    @pl.when(qi % 2 == 1)
    def _():
        step(qi - 1, False)

    step(qi, True)
    outs = []
    for h in range(HEADS):
        a = acc_ref[h]
        outs.append(a[:, :HEAD_DIM] / a[:, HEAD_DIM:])
    y_ref[...] = jnp.concatenate(outs, axis=1).astype(y_ref.dtype)


def _fox(proj, gt, l, bias_r, q_g, k_g, bd, tri):
    nb, s, _ = proj.shape
    t = ATT_T
    kern = functools.partial(_fox_kernel, t=t, s_len=s)
    return pl.pallas_call(
        kern,
        out_shape=jax.ShapeDtypeStruct((nb, s, GROUP), BF16),
        grid=(nb, s // t),
        in_specs=[pl.BlockSpec((None, t, GROUP), lambda b, i: (b, i, C_FQ // GROUP)),
                  pl.BlockSpec((None, s, GROUP), lambda b, i: (b, 0, C_FK // GROUP)),
                  pl.BlockSpec((None, s, GROUP), lambda b, i: (b, 0, C_FV // GROUP)),
                  pl.BlockSpec((None, GT_ROWS, s), lambda b, i: (b, 0, 0)),
                  pl.BlockSpec((None, GT_ROWS, 1), lambda b, i: (l, 0, 0)),
                  pl.BlockSpec((None, 1, GROUP), lambda b, i: (l, 0, 0)),
                  pl.BlockSpec((None, 1, GROUP), lambda b, i: (l, 0, 0)),
                  pl.BlockSpec((GROUP, GROUP), lambda b, i: (0, 0)),
                  pl.BlockSpec((t, t), lambda b, i: (0, 0))],
        out_specs=pl.BlockSpec((None, t, GROUP), lambda b, i: (b, i, 0)),
        scratch_shapes=[pltpu.VMEM((HEADS, s, HEAD_DIM), BF16),
                        pltpu.VMEM((HEADS, s, 2 * HEAD_DIM), BF16),
                        pltpu.VMEM((8, s), F32),
                        pltpu.VMEM((HEADS, t, 128), F32),
                        pltpu.VMEM((HEADS, t, 2 * HEAD_DIM), F32)],
        compiler_params=_cp(("arbitrary", "arbitrary")),
        name="fox",
    )(proj, proj, proj, gt, bias_r, q_g, k_g, bd, tri)


def _sb_kernel(q_ref, k_ref, v_ref, sm_ref, y_ref, kb_ref, vb_ref, rest_ref, acc_ref, *, t, s_len):
    qi = pl.program_id(1)

    @pl.when(qi == 0)
    def _():
        for r in range(s_len // t):
            rows = slice(r * t, (r + 1) * t)
            kk = k_ref[rows, :]
            vv = v_ref[rows, :]
            for h in range(HEADS):
                sl = slice(h * HEAD_DIM, (h + 1) * HEAD_DIM)
                kb_ref[h, rows, :] = kk[:, sl].astype(BF16)
                vb_ref[h, rows, :] = vv[:, sl].astype(BF16)

    qs = q_ref[...] * (QK_SCALE * LOG2E)
    qh = [qs[:, h * HEAD_DIM:(h + 1) * HEAD_DIM].astype(BF16) for h in range(HEADS)]
    strict = lax.broadcasted_iota(jnp.int32, (t, t), 1) < lax.broadcasted_iota(jnp.int32, (t, t), 0)
    from_here = sm_ref[...]
    rest_ref[...] = jnp.zeros(rest_ref.shape, F32)
    acc_ref[...] = jnp.zeros(acc_ref.shape, F32)

    def step(kb, masked):
        k0 = pl.multiple_of(kb * t, t)
        hs = range(HEADS)
        z = [_dot_t(qh[h], kb_ref[h, pl.ds(k0, t), :]) for h in hs]
        u = [jnp.maximum(z[h], 0.0) + jnp.log2(1.0 + jnp.exp2(-jnp.abs(z[h]))) for h in hs]
        if masked:
            u = [jnp.where(strict, v, 0.0) for v in u]
        incl = [jnp.dot(u[h].astype(BF16), from_here, preferred_element_type=F32) for h in hs]
        rest = [rest_ref[h] for h in hs]
        a = [jnp.exp2(z[h] - incl[h] - jnp.concatenate([rest[h]] * (t // 128), axis=1)) for h in hs]
        if masked:
            a = [jnp.where(strict, v, 0.0) for v in a]
        av = [jnp.dot(a[h].astype(BF16), vb_ref[h, pl.ds(k0, t), :], preferred_element_type=F32) for h in hs]
        for h in hs:
            acc_ref[h] += av[h]
            rest_ref[h] = rest[h] + incl[h][:, 0:1]

    step(qi, True)

    @pl.when(qi % 2 == 1)
    def _():
        step(qi - 1, False)

    def body(j, c):
        kb = 2 * (qi // 2 - 1 - j)
        step(kb + 1, False)
        step(kb, False)
        return c

    lax.fori_loop(0, qi // 2, body, 0)
    y_ref[...] = jnp.concatenate([acc_ref[h] for h in range(HEADS)], axis=1).astype(y_ref.dtype)


def _sb(proj, after):
    nb, s, _ = proj.shape
    t = ATT_T
    kern = functools.partial(_sb_kernel, t=t, s_len=s)
    return pl.pallas_call(
        kern,
        out_shape=jax.ShapeDtypeStruct((nb, s, GROUP), BF16),
        grid=(nb, s // t),
        in_specs=[pl.BlockSpec((None, t, GROUP), lambda b, i: (b, i, C_SQ // GROUP)),
                  pl.BlockSpec((None, s, GROUP), lambda b, i: (b, 0, C_SK // GROUP)),
                  pl.BlockSpec((None, s, GROUP), lambda b, i: (b, 0, C_SV // GROUP)),
                  pl.BlockSpec((t, t), lambda b, i: (0, 0))],
        out_specs=pl.BlockSpec((None, t, GROUP), lambda b, i: (b, i, 0)),
        scratch_shapes=[pltpu.VMEM((HEADS, s, HEAD_DIM), BF16),
                        pltpu.VMEM((HEADS, s, HEAD_DIM), BF16),
                        pltpu.VMEM((HEADS, t, 128), F32),
                        pltpu.VMEM((HEADS, t, HEAD_DIM), F32)],
        compiler_params=_cp(("arbitrary", "arbitrary")),
        name="stick_breaking",
    )(proj, proj, proj, after)


def _nsa_kernel(q_ref, kvs_ref, kvw_ref, ak_ref, av_ref, gc_ref, gb_ref, qg_ref, kg_ref, pos_ref,
                wk_ref, wv_ref, bd_ref, tc_ref, ovt_ref, tsb_ref, twb_ref, cb_ref, y_ref,
                ksel_ref, vsel_ref, kwin_ref, vwin_ref, kcmp_ref, vcmp_ref, m_ref, acc_ref,
                *, tq, s_len, n_sel):
    qi = pl.program_id(1)
    nc = s_len // CMP_STRIDE

    @pl.when(qi == 0)
    def _():
        zk = jnp.zeros((WINDOW, HEAD_DIM), BF16)
        zv = jnp.zeros((WINDOW, 2 * HEAD_DIM), BF16)
        ksel_ref[0:WINDOW, :] = zk
        vsel_ref[0:WINDOW, :] = zv
        kwin_ref[0:WINDOW, :] = zk
        vwin_ref[0:WINDOW, :] = zv
        ones = jnp.ones((512, HEAD_DIM), BF16)
        for r in range(s_len // 512):
            rows = slice(r * 512, (r + 1) * 512)
            dst = slice(WINDOW + r * 512, WINDOW + (r + 1) * 512)
            kvs = kvs_ref[rows, :]
            kvw = kvw_ref[rows, :]
            ksel_ref[dst, :] = _rms_lanes(kvs[:, :HEAD_DIM], kg_ref[1:2, :]).astype(BF16)
            kwin_ref[dst, :] = _rms_lanes(kvw[:, :HEAD_DIM], kg_ref[2:3, :]).astype(BF16)
            vsel_ref[dst, :] = jnp.concatenate([kvs[:, HEAD_DIM:].astype(BF16), ones], axis=1)
            vwin_ref[dst, :] = jnp.concatenate([kvw[:, HEAD_DIM:].astype(BF16), ones], axis=1)
        half = CMP_STRIDE * HEAD_DIM
        for a_ref, w_ref, dst_ref, norm in ((ak_ref, wk_ref, kcmp_ref, True), (av_ref, wv_ref, vcmp_ref, False)):
            a = a_ref[...]
            pos_b = _dot_hi(pos_ref[...], w_ref[...])[0:1, :]
            p1 = _dot_hi(a, w_ref[0:half, :])
            p2 = _dot_hi(a, w_ref[half:2 * half, :])
            c = p1 + pltpu.roll(p2, nc - 1, 0) + pos_b
            if norm:
                hi, lo = _split2(_rms_lanes(c, kg_ref[0:1, :]))
                kcmp_ref[0] = hi
                kcmp_ref[1] = lo
            else:
                dst_ref[...] = c.astype(BF16)

    t0 = qi * tq
    qn = _rms_heads(q_ref[...], bd_ref[...], qg_ref[...]) * (QK_SCALE * LOG2E)
    qf = [qn[:, h * HEAD_DIM:(h + 1) * HEAD_DIM] for h in range(HEADS)]
    qh = [q.astype(BF16) for q in qf]
    ql = [(qf[h] - qh[h].astype(F32)).astype(BF16) for h in range(HEADS)]

    hs = range(HEADS)
    k_hi, k_lo = kcmp_ref[0], kcmp_ref[1]
    dn = (((1,), (1,)), ((), ()))
    sc = [lax.dot_general(qh[h], k_hi, dn, preferred_element_type=F32)
          + lax.dot_general(ql[h], k_hi, dn, preferred_element_type=F32)
          + lax.dot_general(qh[h], k_lo, dn, preferred_element_type=F32) + tc_ref[h] for h in hs]
    e = [jnp.exp2(sc[h] - jnp.max(sc[h], axis=1, keepdims=True)) for h in hs]
    p = [jnp.where(tc_ref[h] > 0.5 * NEG, e[h] / jnp.sum(e[h], axis=1, keepdims=True), 0.0) for h in hs]
    o_cmp = [jnp.dot(p[h].astype(BF16), vcmp_ref[...], preferred_element_type=F32) for h in hs]
    ps_hi, ps_lo = _split2(p[0] + p[1] + p[2] + p[3])
    ovt = ovt_ref[...].astype(BF16)
    imp_t = (lax.dot_general(ovt, ps_hi, dn, preferred_element_type=F32)
             + lax.dot_general(ovt, ps_lo, dn, preferred_element_type=F32))

    jj = lax.broadcasted_iota(jnp.int32, (128, tq), 0)
    tt = t0 + lax.broadcasted_iota(jnp.int32, (128, tq), 1)
    cur = tt // SEL_BLOCK
    forced = (jj == 0) | (jj == cur) | (jj == cur - 1)
    score = jnp.where(forced, -NEG, jnp.where(jj * SEL_BLOCK <= tt, imp_t, NEG))
    cnt = jnp.zeros((128, tq), F32)
    for i in range(n_sel):
        ri = score[i:i + 1, :]
        cnt = cnt + jnp.where(jj > i, (ri >= score).astype(F32), (ri > score).astype(F32))
    selb = (cnt < float(min(N_SEL_TOP, n_sel))).astype(F32).T.astype(BF16)

    m_ref[...] = jnp.full(m_ref.shape, NEG, F32)
    acc_ref[...] = jnp.zeros(acc_ref.shape, F32)
    jrow = lax.broadcasted_iota(jnp.int32, (128, tq), 0)
    jcol = lax.broadcasted_iota(jnp.int32, (128, tq), 1) // SEL_BLOCK

    def sel_chunk(kp0, near):
        jb = kp0 // SEL_BLOCK - WINDOW // SEL_BLOCK
        expand = (jrow == jb + jcol).astype(BF16)
        picked = jnp.dot(selb, expand, preferred_element_type=F32) > 0.5
        kblk = ksel_ref[pl.ds(kp0, tq), :]
        vblk = vsel_ref[pl.ds(kp0, tq), :]
        hs = range(HEADS)
        bias = [cb_ref[h] if near is None else tsb_ref[h, :, near * tq:(near + 1) * tq] for h in hs]
        sck = [_dot_t(qh[h], kblk) + jnp.where(picked, bias[h], NEG) for h in hs]
        m_old = [m_ref[h] for h in hs]
        m_new = [jnp.maximum(m_old[h], jnp.max(sck[h], axis=1, keepdims=True)) for h in hs]
        pk = [jnp.exp2(sck[h] - jnp.concatenate([m_new[h]] * (tq // 128), axis=1)).astype(BF16) for h in hs]
        pv = [jnp.dot(pk[h], vblk, preferred_element_type=F32) for h in hs]
        for h in hs:
            acc_ref[h] = jnp.exp2(m_old[h] - m_new[h]) * acc_ref[h] + pv[h]
            m_ref[h] = m_new[h]

    def far(cp, carry):
        sel_chunk(pl.multiple_of(WINDOW + 2 * cp * tq, tq), None)
        sel_chunk(pl.multiple_of(WINDOW + (2 * cp + 1) * tq, tq), None)
        return carry

    n_far = jnp.maximum(qi - WINDOW // tq, 0)
    lax.fori_loop(0, n_far // 2, far, 0)

    @pl.when(n_far % 2 == 1)
    def _():
        sel_chunk(pl.multiple_of(WINDOW + (n_far - 1) * tq, tq), None)
    for r in range(NSA_NEAR // tq):
        @pl.when(qi + r >= WINDOW // tq)
        def _(r=r):
            sel_chunk(pl.multiple_of(t0 + r * tq, tq), r)

    kw = kwin_ref[pl.ds(pl.multiple_of(t0, tq), NSA_NEAR), :]
    vw = vwin_ref[pl.ds(pl.multiple_of(t0, tq), NSA_NEAR), :]
    in_seq = lax.broadcasted_iota(jnp.int32, (tq, NSA_NEAR), 1) >= WINDOW - t0
    g = _sigmoid(gc_ref[...] + gb_ref[...])
    sw = [jnp.where(in_seq, _dot_t(qh[h], kw) + twb_ref[h], NEG) for h in hs]
    pw = [jnp.exp2(sw[h] - jnp.max(sw[h], axis=1, keepdims=True)).astype(BF16) for h in hs]
    ows = [jnp.dot(pw[h], vw, preferred_element_type=F32) for h in hs]
    outs = []
    for h in hs:
        ow = ows[h]
        o_win = ow[:, :HEAD_DIM] / ow[:, HEAD_DIM:]
        a = acc_ref[h]
        o_sel = a[:, :HEAD_DIM] / a[:, HEAD_DIM:]
        c0 = G_NG + 3 * h
        outs.append(g[:, c0:c0 + 1] * o_cmp[h] + g[:, c0 + 1:c0 + 2] * o_sel + g[:, c0 + 2:c0 + 3] * o_win)
    y_ref[...] = jnp.concatenate(outs, axis=1).astype(y_ref.dtype)


def _nsa(proj, a_k, a_v, l, bias_c, q_g, k_g, pos, wk, wv, bd, tc, ovt, tsb, twb, cb):
    nb, s, _ = proj.shape
    tq = NSA_TQ
    nc = s // CMP_STRIDE
    kern = functools.partial(_nsa_kernel, tq=tq, s_len=s, n_sel=s // SEL_BLOCK)
    const2 = lambda b, i: (0, 0)
    const3 = lambda b, i: (0, 0, 0)
    lay3 = lambda b, i: (l, 0, 0)
    return pl.pallas_call(
        kern,
        out_shape=jax.ShapeDtypeStruct((nb, s, GROUP), BF16),
        grid=(nb, s // tq),
        in_specs=[pl.BlockSpec((None, tq, GROUP), lambda b, i: (b, i, C_NQ // GROUP)),
                  pl.BlockSpec((None, s, 128), lambda b, i: (b, 0, C_NKV // 128 + 1)),
                  pl.BlockSpec((None, s, 128), lambda b, i: (b, 0, C_NKV // 128 + 2)),
                  pl.BlockSpec((None, nc, CMP_STRIDE * HEAD_DIM), lambda b, i: (b, 0, 0)),
                  pl.BlockSpec((None, nc, CMP_STRIDE * HEAD_DIM), lambda b, i: (b, 0, 0)),
                  pl.BlockSpec((None, tq, 128), lambda b, i: (b, i, C_GATES // 128)),
                  pl.BlockSpec((None, 1, 128), lay3),
                  pl.BlockSpec((None, 1, GROUP), lay3),
                  pl.BlockSpec((None, 8, HEAD_DIM), lay3),
                  pl.BlockSpec((None, 8, CMP_BLOCK * HEAD_DIM), lay3),
                  pl.BlockSpec((None, CMP_BLOCK * HEAD_DIM, HEAD_DIM), lay3),
                  pl.BlockSpec((None, CMP_BLOCK * HEAD_DIM, HEAD_DIM), lay3),
                  pl.BlockSpec((GROUP, GROUP), const2),
                  pl.BlockSpec((HEADS, tq, nc), lambda b, i: (0, i, 0)),
                  pl.BlockSpec((128, nc), const2),
                  pl.BlockSpec((HEADS, tq, NSA_NEAR), const3),
                  pl.BlockSpec((HEADS, tq, NSA_NEAR), const3),
                  pl.BlockSpec(memory_space=pltpu.SMEM)],
        out_specs=pl.BlockSpec((None, tq, GROUP), lambda b, i: (b, i, 0)),
        scratch_shapes=[pltpu.VMEM((s + WINDOW, HEAD_DIM), BF16),
                        pltpu.VMEM((s + WINDOW, 2 * HEAD_DIM), BF16),
                        pltpu.VMEM((s + WINDOW, HEAD_DIM), BF16),
                        pltpu.VMEM((s + WINDOW, 2 * HEAD_DIM), BF16),
                        pltpu.VMEM((2, nc, HEAD_DIM), BF16),
                        pltpu.VMEM((nc, HEAD_DIM), BF16),
                        pltpu.VMEM((HEADS, tq, 128), F32),
                        pltpu.VMEM((HEADS, tq, 2 * HEAD_DIM), F32)],
        compiler_params=_cp(("arbitrary", "arbitrary")),
        name="nsa",
    )(proj, proj, proj, a_k, a_v, proj, bias_c, q_g, k_g, pos, wk, wv, bd, tc, ovt, tsb, twb, cb)


def _outproj_kernel(x_ref, ya_ref, yb_ref, yc_ref, yd_ref, w_ref, mod_ref, g_ref, xo_ref, h_ref):
    acc = jnp.dot(ya_ref[...], w_ref[0:GROUP, :], preferred_element_type=F32)
    acc += jnp.dot(yb_ref[...], w_ref[GROUP:2 * GROUP, :], preferred_element_type=F32)
    acc += jnp.dot(yc_ref[...], w_ref[2 * GROUP:3 * GROUP, :], preferred_element_type=F32)
    acc += jnp.dot(yd_ref[...], w_ref[3 * GROUP:4 * GROUP, :], preferred_element_type=F32)
    xn = x_ref[...] + mod_ref[2:3, :] * acc
    xo_ref[...] = xn
    h = _rms_lanes(xn, g_ref[...]) * (1.0 + mod_ref[4:5, :]) + mod_ref[3:4, :]
    h_ref[...] = h.astype(h_ref.dtype)


def _outproj(x, ys, mod, l, w_out, norm_g, h_dtype):
    nb, s, d = x.shape
    tm = 512
    yspec = pl.BlockSpec((None, tm, GROUP), lambda b, i: (b, i, 0))
    xspec = pl.BlockSpec((None, tm, d), lambda b, i: (b, i, 0))
    return pl.pallas_call(
        _outproj_kernel,
        out_shape=(jax.ShapeDtypeStruct((nb, s, d), F32), jax.ShapeDtypeStruct((nb, s, d), h_dtype)),
        grid=(nb, s // tm),
        in_specs=[xspec, yspec, yspec, yspec, yspec,
                  pl.BlockSpec((None, d, d), lambda b, i: (l, 0, 0)),
                  pl.BlockSpec((None, None, 6, d), lambda b, i: (l, b, 0, 0)),
                  pl.BlockSpec((None, 1, d), lambda b, i: (l, 0, 0))],
        out_specs=(xspec, xspec),
        compiler_params=_cp(("arbitrary", "arbitrary")),
        name="outproj",
    )(x, *ys, w_out, mod, norm_g)


def _ffn_kernel(h_ref, x_ref, mod_ref, wg_ref, wu_ref, wd_ref, o_ref, acc_ref, *, nf):
    f = pl.program_id(2)

    @pl.when(f == 0)
    def _():
        acc_ref[...] = jnp.zeros_like(acc_ref)

    h = h_ref[...]
    a = jnp.dot(h, wg_ref[...], preferred_element_type=F32)
    u = jnp.dot(h, wu_ref[...], preferred_element_type=F32)
    act = (a * _sigmoid(a) * u).astype(BF16)
    acc_ref[...] += jnp.dot(act, wd_ref[...], preferred_element_type=F32)

    @pl.when(f == nf - 1)
    def _():
        o_ref[...] = x_ref[...] + mod_ref[5:6, :] * acc_ref[...]


def _ffn(h, x, mod, l, li, wg, wu, wd):
    nb, s, d = x.shape
    dff = wg.shape[-1]
    tm, tf = FFN_TM, FFN_TF
    nf = dff // tf
    xspec = pl.BlockSpec((None, tm, d), lambda b, i, f: (b, i, 0))
    return pl.pallas_call(
        functools.partial(_ffn_kernel, nf=nf),
        out_shape=jax.ShapeDtypeStruct((nb, s, d), F32),
        grid=(nb, s // tm, nf),
        in_specs=[xspec, xspec,
                  pl.BlockSpec((None, None, 6, d), lambda b, i, f: (l, b, 0, 0)),
                  pl.BlockSpec((None, d, tf), lambda b, i, f: (li, 0, f)),
                  pl.BlockSpec((None, d, tf), lambda b, i, f: (li, 0, f)),
                  pl.BlockSpec((None, tf, d), lambda b, i, f: (li, f, 0))],
        out_specs=xspec,
        scratch_shapes=[pltpu.VMEM((tm, d), F32)],
        compiler_params=_cp(("arbitrary", "arbitrary", "arbitrary")),
        name="ffn_dense",
    )(h, x, mod, wg, wu, wd)


def _router_kernel(h_ref, w_ref, b_ref, o_ref):
    logits = _dot_hi(h_ref[...], w_ref[...]) + b_ref[...]
    lane = lax.broadcasted_iota(jnp.int32, logits.shape, 1).astype(F32)
    lg = jnp.where(lane < N_EXPERTS, logits, -3e38)
    m1 = jnp.max(lg, axis=1, keepdims=True)
    i1 = jnp.min(jnp.where(lg == m1, lane, 128.0), axis=1, keepdims=True)
    lg2 = jnp.where(lane == i1, -3e38, lg)
    m2 = jnp.max(lg2, axis=1, keepdims=True)
    i2 = jnp.min(jnp.where(lg2 == m2, lane, 128.0), axis=1, keepdims=True)
    e2 = jnp.exp(m2 - m1)
    w1 = 1.0 / (1.0 + e2)
    w2 = e2 / (1.0 + e2)
    o_ref[...] = jnp.where(lane == 0, i1, jnp.where(lane == 1, i2, jnp.where(lane == 2, w1, jnp.where(lane == 3, w2, 0.0))))


def _router(hf, rw, rb):
    t, d = hf.shape
    tm = 1024
    return pl.pallas_call(
        _router_kernel,
        out_shape=jax.ShapeDtypeStruct((t, 128), F32),
        grid=(t // tm,),
        in_specs=[pl.BlockSpec((tm, d), lambda i: (i, 0)),
                  pl.BlockSpec((d, 128), lambda i: (0, 0)),
                  pl.BlockSpec((1, 128), lambda i: (0, 0))],
        out_specs=pl.BlockSpec((tm, 128), lambda i: (i, 0)),
        compiler_params=_cp(("arbitrary",)),
        name="moe_router",
    )(hf, rw, rb)


def _gather_copy(src_hbm, row, dst_buf, slot, r, sem):
    return pltpu.make_async_copy(src_hbm.at[pl.ds(row, 1)], dst_buf.at[slot, pl.ds(r, 1)], sem.at[slot])


def _experts_kernel(be_ref, tok_ref, nv_ref, h_hbm, wg_ref, wu_ref, wd_ref, y_ref,
                    xbuf, xb_ref, acc_ref, sem, *, bm, nf):
    del be_ref
    i = pl.program_id(0)
    f = pl.program_id(1)
    nvalid = nv_ref[0]
    slot = i % 2

    def issue(blk, sl):
        def body(r, c):
            _gather_copy(h_hbm, tok_ref[blk * bm + r], xbuf, sl, r, sem).start()
            return c
        lax.fori_loop(0, bm, body, 0, unroll=8)

    def wait_all(sl):
        def body(r, c):
            _gather_copy(h_hbm, 0, xbuf, sl, r, sem).wait()
            return c
        lax.fori_loop(0, bm, body, 0, unroll=8)

    @pl.when((f == 0) & (i == 0))
    def _():
        issue(0, 0)

    @pl.when((f == 0) & (i < nvalid))
    def _():
        wait_all(slot)

        @pl.when(i + 1 < nvalid)
        def _():
            issue(i + 1, 1 - slot)

        xb_ref[...] = xbuf[slot].astype(BF16)
        acc_ref[...] = jnp.zeros_like(acc_ref)

    @pl.when(i < nvalid)
    def _():
        x = xb_ref[...]
        a = jnp.dot(x, wg_ref[...], preferred_element_type=F32)
        u = jnp.dot(x, wu_ref[...], preferred_element_type=F32)
        act = (a * _sigmoid(a) * u).astype(BF16)
        acc_ref[...] += jnp.dot(act, wd_ref[...], preferred_element_type=F32)

    @pl.when(f == nf - 1)
    def _():
        @pl.when(i < nvalid)
        def _():
            y_ref[...] = acc_ref[...]

        @pl.when(i >= nvalid)
        def _():
            y_ref[...] = jnp.zeros_like(y_ref)


def _experts(hf, blk_expert, row_tok, nvalid, wg, wu, wd, li):
    t, d = hf.shape
    bm, tf = MOE_BM, MOE_TF
    n_rows = row_tok.shape[0]
    n_blocks = n_rows // bm
    dff = wg.shape[-1]
    nf = dff // tf

    def fidx(i, f, nv):
        return jnp.where(i < nv[0], f, nf - 1)

    grid_spec = pltpu.PrefetchScalarGridSpec(
        num_scalar_prefetch=3,
        grid=(n_blocks, nf),
        in_specs=[pl.BlockSpec(memory_space=pl.ANY),
                  pl.BlockSpec((None, None, d, tf), lambda i, f, be, tok, nv: (li, be[i], 0, fidx(i, f, nv))),
                  pl.BlockSpec((None, None, d, tf), lambda i, f, be, tok, nv: (li, be[i], 0, fidx(i, f, nv))),
                  pl.BlockSpec((None, None, tf, d), lambda i, f, be, tok, nv: (li, be[i], fidx(i, f, nv), 0))],
        out_specs=pl.BlockSpec((bm, d), lambda i, f, be, tok, nv: (i, 0)),
        scratch_shapes=[pltpu.VMEM((2, bm, d), F32),
                        pltpu.VMEM((bm, d), BF16),
                        pltpu.VMEM((bm, d), F32),
                        pltpu.SemaphoreType.DMA((2,))],
    )
    return pl.pallas_call(
        functools.partial(_experts_kernel, bm=bm, nf=nf),
        out_shape=jax.ShapeDtypeStruct((n_rows, d), F32),
        grid_spec=grid_spec,
        compiler_params=_cp(("arbitrary", "arbitrary")),
        name="moe_experts",
    )(blk_expert, row_tok, nvalid, hf, wg, wu, wd)


def _combine_kernel(pos_ref, y_hbm, x_ref, mod_ref, rt_ref, o_ref, ybuf, sem, *, tm, s_len):
    b = pl.program_id(0)
    i = pl.program_id(1)
    base = (b * s_len + i * tm) * 2

    def issue(r, c):
        _gather_copy(y_hbm, pos_ref[base + 2 * r], ybuf, 0, r, sem).start()
        _gather_copy(y_hbm, pos_ref[base + 2 * r + 1], ybuf, 1, r, sem).start()
        return c

    def wait(r, c):
        _gather_copy(y_hbm, 0, ybuf, 0, r, sem).wait()
        _gather_copy(y_hbm, 0, ybuf, 1, r, sem).wait()
        return c

    lax.fori_loop(0, tm, issue, 0, unroll=8)
    lax.fori_loop(0, tm, wait, 0, unroll=8)
    rt = rt_ref[...]
    o_ref[...] = x_ref[...] + mod_ref[5:6, :] * (rt[:, 2:3] * ybuf[0] + rt[:, 3:4] * ybuf[1])


def _combine(y, pos, x, mod, route, l):
    nb, s, d = x.shape
    tm = 256
    grid_spec = pltpu.PrefetchScalarGridSpec(
        num_scalar_prefetch=1,
        grid=(nb, s // tm),
        in_specs=[pl.BlockSpec(memory_space=pl.ANY),
                  pl.BlockSpec((None, tm, d), lambda b, i, p: (b, i, 0)),
                  pl.BlockSpec((None, None, 6, d), lambda b, i, p: (l, b, 0, 0)),
                  pl.BlockSpec((None, tm, 128), lambda b, i, p: (b, i, 0))],
        out_specs=pl.BlockSpec((None, tm, d), lambda b, i, p: (b, i, 0)),
        scratch_shapes=[pltpu.VMEM((2, tm, d), F32), pltpu.SemaphoreType.DMA((2,))],
    )
    return pl.pallas_call(
        functools.partial(_combine_kernel, tm=tm, s_len=s),
        out_shape=jax.ShapeDtypeStruct((nb, s, d), F32),
        grid_spec=grid_spec,
        compiler_params=_cp(("arbitrary", "arbitrary")),
        name="moe_combine",
    )(pos, y, x, mod, route.reshape(nb, s, 128))


def _moe(hf32, x, mod, l, li, router_w, router_b, wg, wu, wd):
    nb, s, d = x.shape
    t = nb * s
    bm = MOE_BM
    hf = hf32.reshape(t, d)
    rw = jnp.zeros((d, 128), F32).at[:, :N_EXPERTS].set(router_w[li])
    rb = jnp.zeros((1, 128), F32).at[0, :N_EXPERTS].set(router_b[li])
    route = _router(hf, rw, rb)
    e_flat = route[:, 0:2].astype(jnp.int32).reshape(-1)
    n_assign = 2 * t
    onehot = (e_flat[:, None] == jnp.arange(N_EXPERTS, dtype=jnp.int32)[None, :]).astype(jnp.int32)
    csum = jnp.cumsum(onehot, axis=0)
    rank = jnp.sum(onehot * csum, axis=1) - 1
    counts = csum[-1]
    padded = (counts + bm - 1) // bm * bm
    cum_padded = jnp.cumsum(padded)
    pstart = cum_padded - padded
    dest = pstart[e_flat] + rank
    n_blocks = n_assign // bm + N_EXPERTS
    n_rows = n_blocks * bm
    tok_flat = jnp.arange(n_assign, dtype=jnp.int32) // 2
    row_tok = jnp.zeros((n_rows,), jnp.int32).at[dest].set(tok_flat)
    blk_start = jnp.arange(n_blocks, dtype=jnp.int32) * bm
    blk_expert = jnp.minimum(jnp.searchsorted(cum_padded, blk_start, side='right'), N_EXPERTS - 1).astype(jnp.int32)
    nvalid = (cum_padded[-1] // bm).astype(jnp.int32).reshape(1)
    y = _experts(hf, blk_expert, row_tok, nvalid, wg, wu, wd, li)
    return _combine(y, dest.astype(jnp.int32), x, mod, route, l)


def _t5_bucket(dist):
    n = np.maximum(dist, 0)
    max_exact = NUM_BUCKETS // 2
    large = max_exact + (np.log(np.maximum(n, 1).astype(np.float32) / max_exact)
                         / math.log(MAX_DISTANCE / max_exact) * (NUM_BUCKETS - max_exact)).astype(np.int32)
    return np.where(n < max_exact, n, np.minimum(large, NUM_BUCKETS - 1)).astype(np.int32)


def _bias_by_bucket(rb, bucket):
    ids = jnp.asarray(bucket.astype(np.int8))[None]
    out = jnp.zeros((rb.shape[1],) + bucket.shape, F32)
    for k in range(NUM_BUCKETS):
        out = jnp.where(ids == k, rb[k][:, None, None], out)
    return out


def _nsa_tables(rel_bias, s):
    tq = NSA_TQ
    nc = s // CMP_STRIDE
    n_cmp = (s - CMP_BLOCK) // CMP_STRIDE + 1
    rb = rel_bias.astype(F32) * LOG2E
    t = np.arange(s)[:, None]
    cmp_end = np.arange(nc)[None, :] * CMP_STRIDE + CMP_BLOCK - 1
    ok = (cmp_end <= t) & (np.arange(nc)[None, :] < n_cmp)
    tc = jnp.where(jnp.asarray(ok)[None], _bias_by_bucket(rb, _t5_bucket(t - cmp_end)), NEG)
    dist = np.arange(tq)[:, None] - (np.arange(NSA_NEAR)[None, :] - WINDOW)
    tb = _bias_by_bucket(rb, _t5_bucket(dist))
    tsb = jnp.where(jnp.asarray(dist >= 0)[None], tb, NEG)
    twb = jnp.where(jnp.asarray((dist >= 0) & (dist < WINDOW))[None], tb, NEG)
    far_bucket = int(_t5_bucket(np.array([WINDOW]))[0])
    cb = rb[far_bucket]
    n_sel = s // SEL_BLOCK
    cs = np.arange(nc)[None, :] * CMP_STRIDE
    ss = np.arange(128)[:, None] * SEL_BLOCK
    ov = np.clip(np.minimum(cs + CMP_BLOCK, ss + SEL_BLOCK) - np.maximum(cs, ss), 0, None).astype(np.float32) / CMP_BLOCK
    ov = ov * (np.arange(128)[:, None] < n_sel) * (np.arange(nc)[None, :] < n_cmp)
    return tc, jnp.asarray(ov, F32), tsb, twb, cb


def _prep_layer_params(w_in, mlstm_gate_b, fox_f_b, nsa_gate_b):
    depth, d, _ = w_in.shape
    small = [w_in[:, :, 1024:1032], w_in[:, :, 1800:1804], w_in[:, :, 3212:3224]]
    w_p = jnp.concatenate([w_in[:, :, 0:1024], w_in[:, :, 1032:1800], w_in[:, :, 1804:2572], w_in[:, :, 2572:3212]]
                          + small + [jnp.zeros((depth, d, 128 - 24), w_in.dtype)], axis=-1).astype(BF16)
    wgt = jnp.concatenate(small + [jnp.zeros((depth, d, GT_ROWS - 24), w_in.dtype)], axis=-1)
    wgt = jnp.transpose(wgt, (0, 2, 1)).astype(BF16)
    gate_b = jnp.concatenate([mlstm_gate_b, fox_f_b, nsa_gate_b], axis=-1).astype(F32)
    bias_c = jnp.zeros((depth, 1, 128), F32).at[:, 0, :24].set(gate_b)
    bias_r = jnp.zeros((depth, GT_ROWS, 1), F32).at[:, :24, 0].set(gate_b)
    return w_p, wgt, bias_c, bias_r


def kernel(x, c, rel_bias, ada_w, ada_b, norm1_g, norm2_g, w_in, w_out, mlstm_conv_w, mlstm_gate_b, mlstm_out_g,
           fox_f_b, fox_q_g, fox_k_g, nsa_q_g, nsa_k_g, nsa_cmp_pos, nsa_cmp_wk, nsa_cmp_wv, nsa_gate_b,
           ffn_wg, ffn_wu, ffn_wd, moe_router_w, moe_router_b, moe_wg, moe_wu, moe_wd):
    nb, s, d = x.shape
    depth = w_in.shape[0]
    assert d == D_MODEL and s % 512 == 0 and s // SEL_BLOCK <= 128

    w_p, wgt, bias_c, bias_r = _prep_layer_params(w_in, mlstm_gate_b, fox_f_b, nsa_gate_b)
    w_out_b = w_out.astype(BF16)
    n1g = norm1_g.reshape(depth, 1, d)
    n2g = norm2_g.reshape(depth, 1, d)
    m_out_g = mlstm_out_g.reshape(depth, 1, GROUP)
    fq_g = jnp.tile(fox_q_g, (1, HEADS)).reshape(depth, 1, GROUP)
    fk_g = jnp.tile(fox_k_g, (1, HEADS)).reshape(depth, 1, GROUP)
    nq_g = jnp.tile(nsa_q_g, (1, HEADS)).reshape(depth, 1, GROUP)
    nk_g = jnp.zeros((depth, 8, HEAD_DIM), F32).at[:, :3].set(nsa_k_g)
    pos8 = jnp.zeros((depth, 8, CMP_BLOCK * HEAD_DIM), F32).at[:, 0].set(nsa_cmp_pos.reshape(depth, -1))
    ffn_wg_b, ffn_wu_b, ffn_wd_b = ffn_wg.astype(BF16), ffn_wu.astype(BF16), ffn_wd.astype(BF16)
    moe_wg_b, moe_wu_b, moe_wd_b = moe_wg.astype(BF16), moe_wu.astype(BF16), moe_wd.astype(BF16)
    hid = np.arange(GROUP) // HEAD_DIM
    bd = jnp.asarray((hid[:, None] == hid[None, :]).astype(np.float32) / HEAD_DIM)
    tri_l = jnp.asarray(np.tril(np.ones((MLSTM_L, MLSTM_L), np.float32)))
    tri_t = jnp.asarray(np.tril(np.ones((ATT_T, ATT_T), np.float32)))
    from_here = jnp.asarray(np.tril(np.ones((ATT_T, ATT_T), np.float32)), BF16)
    tc, ovt, tsb, twb, cb = _nsa_tables(rel_bias, s)

    mod = _adaln(c, ada_w, ada_b)
    for l in range(depth):
        proj, gt = _inproj(x, mod, l, n1g, w_p, wgt)
        y_a = _mlstm(proj, gt, l, mlstm_conv_w, bias_c, bias_r, m_out_g, tri_l)
        y_b = _fox(proj, gt, l, bias_r, fq_g, fk_g, bd, tri_t)
        y_c = _sb(proj, from_here)
        a_k = proj[:, :, C_NKV:C_NKV + HEAD_DIM].reshape(nb, s // CMP_STRIDE, CMP_STRIDE * HEAD_DIM)
        a_v = proj[:, :, C_NKV + HEAD_DIM:C_NKV + 2 * HEAD_DIM].reshape(nb, s // CMP_STRIDE, CMP_STRIDE * HEAD_DIM)
        y_d = _nsa(proj, a_k, a_v, l, bias_c, nq_g, nk_g, pos8, nsa_cmp_wk, nsa_cmp_wv, bd, tc, ovt, tsb, twb, cb)
        if l % 2 == 0:
            x, h2 = _outproj(x, (y_a, y_b, y_c, y_d), mod, l, w_out_b, n2g, BF16)
            x = _ffn(h2, x, mod, l, l // 2, ffn_wg_b, ffn_wu_b, ffn_wd_b)
        else:
            x, h2 = _outproj(x, (y_a, y_b, y_c, y_d), mod, l, w_out_b, n2g, F32)
            x = _moe(h2, x, mod, l, l // 2, moe_router_w, moe_router_b, moe_wg_b, moe_wu_b, moe_wd_b)
    return x
```

```python
import functools
import math

import numpy as np
import jax
import jax.numpy as jnp
from jax import lax
from jax.experimental import pallas as pl
from jax.experimental.pallas import tpu as pltpu

F32 = jnp.float32
BF16 = jnp.bfloat16
HI = lax.Precision.HIGHEST

D_MODEL = 1024
HEADS = 4
HEAD_DIM = 64
GROUP = HEADS * HEAD_DIM
NORM_EPS = 1e-6
NEG = -1e30
QK_SCALE = HEAD_DIM ** -0.5
LOG2E = 1.4426950408889634
CMP_BLOCK = 32
CMP_STRIDE = 16
SEL_BLOCK = 64
N_SEL_TOP = 16
WINDOW = 512
NUM_BUCKETS = 32
MAX_DISTANCE = 128
N_EXPERTS = 8
VMEM_LIMIT = 56 * 1024 * 1024

PW = 3328
C_MQK, C_MV, C_MO = 0, 512, 768
C_FQ, C_FK, C_FV = 1024, 1280, 1536
C_SQ, C_SK, C_SV = 1792, 2048, 2304
C_NQ, C_NKV, C_GATES = 2560, 2816, 3200
G_MI, G_MF, G_FF, G_NG = 0, 4, 8, 12
GT_ROWS = 32

MLSTM_TS = 512
MLSTM_L = 128
ATT_T = 256
NSA_TQ = 256
NSA_NEAR = WINDOW + NSA_TQ
MOE_BM = 512
MOE_TF = 896
FFN_TM = 512
FFN_TF = 1408


def _cp(sem, vmem=VMEM_LIMIT):
    return pltpu.CompilerParams(dimension_semantics=sem, vmem_limit_bytes=vmem)


def _dot(a, b):
    return jnp.dot(a.astype(BF16), b.astype(BF16), preferred_element_type=F32)


def _dot_t(a, b):
    return lax.dot_general(a.astype(BF16), b.astype(BF16), (((1,), (1,)), ((), ())), preferred_element_type=F32)


def _dot_hi(a, b):
    return jnp.dot(a, b, precision=HI, preferred_element_type=F32)


def _dot_t_hi(a, b):
    return lax.dot_general(a, b, (((1,), (1,)), ((), ())), precision=HI, preferred_element_type=F32)


def _sigmoid(x):
    return 1.0 / (1.0 + jnp.exp(-x))


def _log_sigmoid(x):
    return jnp.minimum(x, 0.0) - jnp.log1p(jnp.exp(-jnp.abs(x)))


def _rms_lanes(x, g):
    return x * lax.rsqrt(jnp.mean(x * x, axis=-1, keepdims=True) + NORM_EPS) * g


def _split2(a):
    hi = a.astype(BF16)
    return hi, (a - hi.astype(F32)).astype(BF16)


def _dot_2x(a, b):
    hi, lo = _split2(a)
    bb = b.astype(BF16)
    return jnp.dot(hi, bb, preferred_element_type=F32) + jnp.dot(lo, bb, preferred_element_type=F32)


def _dot_t_2x(a, b):
    hi, lo = _split2(a)
    bb = b.astype(BF16)
    dn = (((1,), (1,)), ((), ()))
    return (lax.dot_general(hi, bb, dn, preferred_element_type=F32)
            + lax.dot_general(lo, bb, dn, preferred_element_type=F32))


def _rms_heads(x, bd, g):
    return x * lax.rsqrt(_dot_2x(x * x, bd) + NORM_EPS) * g


def _adaln_kernel(c_ref, w_ref, b_ref, o_ref):
    c = c_ref[...]
    o_ref[...] = _dot_hi(c * _sigmoid(c), w_ref[...]) + b_ref[...]


def _adaln(c, ada_w, ada_b):
    depth, d, six_d = ada_w.shape
    nb = c.shape[0]
    out = pl.pallas_call(
        _adaln_kernel,
        out_shape=jax.ShapeDtypeStruct((depth, nb, six_d), F32),
        grid=(depth, six_d // d),
        in_specs=[pl.BlockSpec((nb, d), lambda l, j: (0, 0)),
                  pl.BlockSpec((None, d, d), lambda l, j: (l, 0, j)),
                  pl.BlockSpec((None, 1, d), lambda l, j: (l, 0, j))],
        out_specs=pl.BlockSpec((None, nb, d), lambda l, j: (l, 0, j)),
        compiler_params=_cp(("arbitrary", "arbitrary")),
        name="adaln",
    )(c, ada_w, ada_b.reshape(depth, 1, six_d))
    return out.reshape(depth, nb, 6, d)


def _inproj_kernel(x_ref, mod_ref, g_ref, w_ref, wgt_ref, o_ref, gt_ref):
    h = _rms_lanes(x_ref[...], g_ref[...]) * (1.0 + mod_ref[1:2, :]) + mod_ref[0:1, :]
    hb = h.astype(BF16)
    o_ref[...] = jnp.dot(hb, w_ref[...], preferred_element_type=F32)
    gt_ref[...] = lax.dot_general(wgt_ref[...], hb, (((1,), (1,)), ((), ())), preferred_element_type=F32)


def _inproj(x, mod, l, norm_g, w_p, wgt):
    nb, s, d = x.shape
    tm = 256
    return pl.pallas_call(
        _inproj_kernel,
        out_shape=(jax.ShapeDtypeStruct((nb, s, PW), F32), jax.ShapeDtypeStruct((nb, GT_ROWS, s), F32)),
        grid=(nb, s // tm),
        in_specs=[pl.BlockSpec((None, tm, d), lambda b, i: (b, i, 0)),
                  pl.BlockSpec((None, None, 6, d), lambda b, i: (l, b, 0, 0)),
                  pl.BlockSpec((None, 1, d), lambda b, i: (l, 0, 0)),
                  pl.BlockSpec((None, d, PW), lambda b, i: (l, 0, 0)),
                  pl.BlockSpec((None, GT_ROWS, d), lambda b, i: (l, 0, 0))],
        out_specs=(pl.BlockSpec((None, tm, PW), lambda b, i: (b, i, 0)),
                   pl.BlockSpec((None, GT_ROWS, tm), lambda b, i: (b, 0, i))),
        compiler_params=_cp(("arbitrary", "arbitrary")),
        name="inproj",
    )(x, mod, norm_g, w_p, wgt)


def _mlstm_kernel(qk_ref, v_ref, op_ref, gc_ref, gr_ref, cw_ref, bc_ref, br_ref, og_ref, tri_ref, y_ref,
                  prev_ref, qs_ref, ks_ref, c_ref, n_ref, m_ref, *, ts, cl):
    @pl.when(pl.program_id(1) == 0)
    def _():
        prev_ref[...] = jnp.zeros_like(prev_ref)
        c_ref[...] = jnp.zeros_like(c_ref)
        n_ref[...] = jnp.zeros_like(n_ref)
        m_ref[...] = jnp.zeros_like(m_ref)

    x = qk_ref[...]
    xc = jnp.concatenate([prev_ref[...], x], axis=0)
    cw = cw_ref[...]
    y = (cw[0:1] * xc[8:8 + ts] + cw[1:2] * xc[7:7 + ts] + cw[2:3] * xc[6:6 + ts] + cw[3:4] * xc[5:5 + ts])
    prev_ref[...] = x[ts - 8:ts]
    y = y * _sigmoid(y)
    qs_ref[...] = y[:, :GROUP]
    ks_ref[...] = y[:, GROUP:] * QK_SCALE

    tri = tri_ref[...]
    lower = lax.broadcasted_iota(jnp.int32, (cl, cl), 0) >= lax.broadcasted_iota(jnp.int32, (cl, cl), 1)

    def chunk(ci, carry):
        r0 = pl.multiple_of(ci * cl, cl)
        gc = gc_ref[pl.ds(r0, cl), :] + bc_ref[...]
        gr = gr_ref[0:8, pl.ds(r0, cl)] + br_ref[0:8, :]
        lf_hi, lf_lo = _split2(_log_sigmoid(gc))
        trib = tri.astype(BF16)
        b_c = (jnp.dot(trib, lf_hi, preferred_element_type=F32)
               + jnp.dot(trib, lf_lo, preferred_element_type=F32))
        b_r = _dot_t_2x(_log_sigmoid(gr), tri)
        q = qs_ref[pl.ds(r0, cl), :]
        k = ks_ref[pl.ds(r0, cl), :]
        v = v_ref[pl.ds(r0, cl), :]
        hs = range(HEADS)
        sls = [slice(h * HEAD_DIM, (h + 1) * HEAD_DIM) for h in hs]
        ig_col = [gc[:, G_MI + h:G_MI + h + 1] for h in hs]
        b_col = [b_c[:, G_MF + h:G_MF + h + 1] for h in hs]
        ig_row = [gr[G_MI + h:G_MI + h + 1, :] for h in hs]
        b_row = [b_r[G_MF + h:G_MF + h + 1, :] for h in hs]
        qh = [q[:, sl] for sl in sls]
        kf = [k[:, sl] for sl in sls]
        kh = [x.astype(BF16) for x in kf]
        vh = [v[:, sl] for sl in sls]
        qb = [x.astype(BF16) for x in qh]
        qk = [lax.dot_general(qb[h], kh[h], (((1,), (1,)), ((), ())), preferred_element_type=F32) for h in hs]
        c_prev = [c_ref[h] for h in hs]
        n_prev = [n_ref[h] for h in hs]
        m_prev = [m_ref[h][:, 0:1] for h in hs]
        qc = [_dot_t(qb[h], c_prev[h]) for h in hs]
        g = [b_row[h][:, cl - 1:cl] for h in hs]
        m_loc = [jnp.max(g[h] - b_row[h] + ig_row[h], axis=1, keepdims=True) for h in hs]
        ew_col = [jnp.exp(g[h] - b_col[h] + ig_col[h] - m_loc[h]) for h in hs]
        c_chunk = [lax.dot_general((vh[h] * ew_col[h]).astype(BF16), kh[h], (((0,), (0,)), ((), ())),
                                   preferred_element_type=F32) for h in hs]
        n_chunk = [jnp.sum(ew_col[h] * kf[h], axis=0, keepdims=True) for h in hs]
        d_log = [jnp.where(lower, b_col[h] - b_row[h] + ig_row[h], NEG) for h in hs]
        m_inter = [b_col[h] + m_prev[h] for h in hs]
        m_t = [jnp.maximum(jnp.max(d_log[h], axis=1, keepdims=True), m_inter[h]) for h in hs]
        s = [qk[h] * jnp.exp(d_log[h] - m_t[h]) for h in hs]
        inter_w = [jnp.exp(m_inter[h] - m_t[h]) for h in hs]
        sv = [_dot(s[h], vh[h]) for h in hs]
        outs = []
        for h in hs:
            num = sv[h] + inter_w[h] * qc[h]
            den = (jnp.sum(s[h], axis=1, keepdims=True)
                   + inter_w[h] * jnp.sum(qh[h] * n_prev[h], axis=1, keepdims=True))
            hh = num / jnp.maximum(jnp.abs(den), jnp.exp(-m_t[h]))
            m_new = jnp.maximum(g[h] + m_prev[h], m_loc[h])
            a = jnp.exp(g[h] + m_prev[h] - m_new)
            bb = jnp.exp(m_loc[h] - m_new)
            c_ref[h] = a * c_prev[h] + bb * c_chunk[h]
            n_ref[h] = a * n_prev[h] + bb * n_chunk[h]
            m_ref[h] = jnp.broadcast_to(m_new, (1, 128))
            outs.append(_rms_lanes(hh, og_ref[:, sls[h]]))
        hcat = jnp.concatenate(outs, axis=1)
        y_ref[pl.ds(r0, cl), :] = (hcat * _sigmoid(op_ref[pl.ds(r0, cl), :])).astype(y_ref.dtype)
        return carry

    lax.fori_loop(0, ts // cl, chunk, 0)


def _mlstm(proj, gt, l, conv_w, bias_c, bias_r, out_g, tri):
    nb, s, _ = proj.shape
    ts, cl = MLSTM_TS, MLSTM_L
    kern = functools.partial(_mlstm_kernel, ts=ts, cl=cl)
    return pl.pallas_call(
        kern,
        out_shape=jax.ShapeDtypeStruct((nb, s, GROUP), BF16),
        grid=(nb, s // ts),
        in_specs=[pl.BlockSpec((None, ts, 2 * GROUP), lambda b, i: (b, i, C_MQK // (2 * GROUP))),
                  pl.BlockSpec((None, ts, GROUP), lambda b, i: (b, i, C_MV // GROUP)),
                  pl.BlockSpec((None, ts, GROUP), lambda b, i: (b, i, C_MO // GROUP)),
                  pl.BlockSpec((None, ts, 128), lambda b, i: (b, i, C_GATES // 128)),
                  pl.BlockSpec((None, GT_ROWS, ts), lambda b, i: (b, 0, i)),
                  pl.BlockSpec((None, 4, 2 * GROUP), lambda b, i: (l, 0, 0)),
                  pl.BlockSpec((None, 1, 128), lambda b, i: (l, 0, 0)),
                  pl.BlockSpec((None, GT_ROWS, 1), lambda b, i: (l, 0, 0)),
                  pl.BlockSpec((None, 1, GROUP), lambda b, i: (l, 0, 0)),
                  pl.BlockSpec((cl, cl), lambda b, i: (0, 0))],
        out_specs=pl.BlockSpec((None, ts, GROUP), lambda b, i: (b, i, 0)),
        scratch_shapes=[pltpu.VMEM((8, 2 * GROUP), F32),
                        pltpu.VMEM((ts, GROUP), F32),
                        pltpu.VMEM((ts, GROUP), F32),
                        pltpu.VMEM((HEADS, HEAD_DIM, HEAD_DIM), F32),
                        pltpu.VMEM((HEADS, 1, HEAD_DIM), F32),
                        pltpu.VMEM((HEADS, 1, 128), F32)],
        compiler_params=_cp(("arbitrary", "arbitrary")),
        name="mlstm",
    )(proj, proj, proj, proj, gt, conv_w, bias_c, bias_r, out_g, tri)


def _fox_kernel(q_ref, k_ref, v_ref, gr_ref, br_ref, qg_ref, kg_ref, bd_ref, tri_ref, y_ref,
                kn_ref, vx_ref, fn_ref, m_ref, acc_ref, *, t, s_len):
    qi = pl.program_id(1)
    bd = bd_ref[...]

    @pl.when(qi == 0)
    def _():
        kg = kg_ref[...]
        ones = jnp.ones((t, HEAD_DIM), BF16)
        for r in range(s_len // t):
            rows = slice(r * t, (r + 1) * t)
            kn = _rms_heads(k_ref[rows, :], bd, kg)
            vv = v_ref[rows, :]
            for h in range(HEADS):
                sl = slice(h * HEAD_DIM, (h + 1) * HEAD_DIM)
                kn_ref[h, rows, :] = kn[:, sl].astype(BF16)
                vx_ref[h, rows, :] = jnp.concatenate([vv[:, sl].astype(BF16), ones], axis=1)
        carry = jnp.zeros((8, 1), F32)
        for r in range(s_len // t):
            cols = slice(r * t, (r + 1) * t)
            lf = _log_sigmoid(gr_ref[G_FF:G_FF + 8, cols] + br_ref[G_FF:G_FF + 8, :])
            cs = _dot_t_hi(lf, tri_ref[...]) + carry
            fn_ref[:, cols] = -LOG2E * cs
            carry = cs[:, t - 1:t]

    qn = _rms_heads(q_ref[...], bd, qg_ref[...]) * (QK_SCALE * LOG2E)
    qh = [qn[:, h * HEAD_DIM:(h + 1) * HEAD_DIM].astype(BF16) for h in range(HEADS)]
    causal = lax.broadcasted_iota(jnp.int32, (t, t), 1) <= lax.broadcasted_iota(jnp.int32, (t, t), 0)
    m_ref[...] = jnp.full(m_ref.shape, NEG, F32)
    acc_ref[...] = jnp.zeros(acc_ref.shape, F32)

    def step(kb, masked):
        k0 = pl.multiple_of(kb * t, t)
        hs = range(HEADS)
        sc = [_dot_t(qh[h], kn_ref[h, pl.ds(k0, t), :]) + fn_ref[h:h + 1, pl.ds(k0, t)] for h in hs]
        if masked:
            sc = [jnp.where(causal, s, NEG) for s in sc]
        m_old = [m_ref[h] for h in hs]
        m_new = [jnp.maximum(m_old[h], jnp.max(sc[h], axis=1, keepdims=True)) for h in hs]
        p = [jnp.exp2(sc[h] - jnp.concatenate([m_new[h]] * (t // 128), axis=1)).astype(BF16) for h in hs]
        pv = [jnp.dot(p[h], vx_ref[h, pl.ds(k0, t), :], preferred_element_type=F32) for h in hs]
        for h in hs:
            acc_ref[h] = jnp.exp2(m_old[h] - m_new[h]) * acc_ref[h] + pv[h]
            m_ref[h] = m_new[h]

    def body(kp, c):
        step(2 * kp, False)
        step(2 * kp + 1, False)
        return c

    lax.fori_loop(0, qi // 2, body, 0)

    @pl.when(qi % 2 == 1)
    def _():
        step(qi - 1, False)

    step(qi, True)
    outs = []
    for h in range(HEADS):
        a = acc_ref[h]
        outs.append(a[:, :HEAD_DIM] / a[:, HEAD_DIM:])
    y_ref[...] = jnp.concatenate(outs, axis=1).astype(y_ref.dtype)


def _fox(proj, gt, l, bias_r, q_g, k_g, bd, tri):
    nb, s, _ = proj.shape
    t = ATT_T
    kern = functools.partial(_fox_kernel, t=t, s_len=s)
    return pl.pallas_call(
        kern,
        out_shape=jax.ShapeDtypeStruct((nb, s, GROUP), BF16),
        grid=(nb, s // t),
        in_specs=[pl.BlockSpec((None, t, GROUP), lambda b, i: (b, i, C_FQ // GROUP)),
                  pl.BlockSpec((None, s, GROUP), lambda b, i: (b, 0, C_FK // GROUP)),
                  pl.BlockSpec((None, s, GROUP), lambda b, i: (b, 0, C_FV // GROUP)),
                  pl.BlockSpec((None, GT_ROWS, s), lambda b, i: (b, 0, 0)),
                  pl.BlockSpec((None, GT_ROWS, 1), lambda b, i: (l, 0, 0)),
                  pl.BlockSpec((None, 1, GROUP), lambda b, i: (l, 0, 0)),
                  pl.BlockSpec((None, 1, GROUP), lambda b, i: (l, 0, 0)),
                  pl.BlockSpec((GROUP, GROUP), lambda b, i: (0, 0)),
                  pl.BlockSpec((t, t), lambda b, i: (0, 0))],
        out_specs=pl.BlockSpec((None, t, GROUP), lambda b, i: (b, i, 0)),
        scratch_shapes=[pltpu.VMEM((HEADS, s, HEAD_DIM), BF16),
                        pltpu.VMEM((HEADS, s, 2 * HEAD_DIM), BF16),
                        pltpu.VMEM((8, s), F32),
                        pltpu.VMEM((HEADS, t, 128), F32),
                        pltpu.VMEM((HEADS, t, 2 * HEAD_DIM), F32)],
        compiler_params=_cp(("arbitrary", "arbitrary")),
        name="fox",
    )(proj, proj, proj, gt, bias_r, q_g, k_g, bd, tri)


def _sb_kernel(q_ref, k_ref, v_ref, sm_ref, y_ref, kb_ref, vb_ref, rest_ref, acc_ref, *, t, s_len):
    qi = pl.program_id(1)

    @pl.when(qi == 0)
    def _():
        for r in range(s_len // t):
            rows = slice(r * t, (r + 1) * t)
            kk = k_ref[rows, :]
            vv = v_ref[rows, :]
            for h in range(HEADS):
                sl = slice(h * HEAD_DIM, (h + 1) * HEAD_DIM)
                kb_ref[h, rows, :] = kk[:, sl].astype(BF16)
                vb_ref[h, rows, :] = vv[:, sl].astype(BF16)

    qs = q_ref[...] * (QK_SCALE * LOG2E)
    qh = [qs[:, h * HEAD_DIM:(h + 1) * HEAD_DIM].astype(BF16) for h in range(HEADS)]
    strict = lax.broadcasted_iota(jnp.int32, (t, t), 1) < lax.broadcasted_iota(jnp.int32, (t, t), 0)
    from_here = sm_ref[...]
    rest_ref[...] = jnp.zeros(rest_ref.shape, F32)
    acc_ref[...] = jnp.zeros(acc_ref.shape, F32)

    def step(kb, masked):
        k0 = pl.multiple_of(kb * t, t)
        hs = range(HEADS)
        z = [_dot_t(qh[h], kb_ref[h, pl.ds(k0, t), :]) for h in hs]
        u = [jnp.maximum(z[h], 0.0) + jnp.log2(1.0 + jnp.exp2(-jnp.abs(z[h]))) for h in hs]
        if masked:
            u = [jnp.where(strict, v, 0.0) for v in u]
        incl = [jnp.dot(u[h].astype(BF16), from_here, preferred_element_type=F32) for h in hs]
        rest = [rest_ref[h] for h in hs]
        a = [jnp.exp2(z[h] - incl[h] - jnp.concatenate([rest[h]] * (t // 128), axis=1)) for h in hs]
        if masked:
            a = [jnp.where(strict, v, 0.0) for v in a]
        av = [jnp.dot(a[h].astype(BF16), vb_ref[h, pl.ds(k0, t), :], preferred_element_type=F32) for h in hs]
        for h in hs:
            acc_ref[h] += av[h]
            rest_ref[h] = rest[h] + incl[h][:, 0:1]

    step(qi, True)

    @pl.when(qi % 2 == 1)
    def _():
        step(qi - 1, False)

    def body(j, c):
        kb = 2 * (qi // 2 - 1 - j)
        step(kb + 1, False)
        step(kb, False)
        return c

    lax.fori_loop(0, qi // 2, body, 0)
    y_ref[...] = jnp.concatenate([acc_ref[h] for h in range(HEADS)], axis=1).astype(y_ref.dtype)


def _sb(proj, after):
    nb, s, _ = proj.shape
    t = ATT_T
    kern = functools.partial(_sb_kernel, t=t, s_len=s)
    return pl.pallas_call(
        kern,
        out_shape=jax.ShapeDtypeStruct((nb, s, GROUP), BF16),
        grid=(nb, s // t),
        in_specs=[pl.BlockSpec((None, t, GROUP), lambda b, i: (b, i, C_SQ // GROUP)),
                  pl.BlockSpec((None, s, GROUP), lambda b, i: (b, 0, C_SK // GROUP)),
                  pl.BlockSpec((None, s, GROUP), lambda b, i: (b, 0, C_SV // GROUP)),
                  pl.BlockSpec((t, t), lambda b, i: (0, 0))],
        out_specs=pl.BlockSpec((None, t, GROUP), lambda b, i: (b, i, 0)),
        scratch_shapes=[pltpu.VMEM((HEADS, s, HEAD_DIM), BF16),
                        pltpu.VMEM((HEADS, s, HEAD_DIM), BF16),
                        pltpu.VMEM((HEADS, t, 128), F32),
                        pltpu.VMEM((HEADS, t, HEAD_DIM), F32)],
        compiler_params=_cp(("arbitrary", "arbitrary")),
        name="stick_breaking",
    )(proj, proj, proj, after)


def _nsa_kernel(q_ref, kvs_ref, kvw_ref, ak_ref, av_ref, gc_ref, gb_ref, qg_ref, kg_ref, pos_ref,
                wk_ref, wv_ref, bd_ref, tc_ref, ovt_ref, tsb_ref, twb_ref, cb_ref, y_ref,
                ksel_ref, vsel_ref, kwin_ref, vwin_ref, kcmp_ref, vcmp_ref, m_ref, acc_ref,
                *, tq, s_len, n_sel):
    qi = pl.program_id(1)
    nc = s_len // CMP_STRIDE

    @pl.when(qi == 0)
    def _():
        zk = jnp.zeros((WINDOW, HEAD_DIM), BF16)
        zv = jnp.zeros((WINDOW, 2 * HEAD_DIM), BF16)
        ksel_ref[0:WINDOW, :] = zk
        vsel_ref[0:WINDOW, :] = zv
        kwin_ref[0:WINDOW, :] = zk
        vwin_ref[0:WINDOW, :] = zv
        ones = jnp.ones((512, HEAD_DIM), BF16)
        for r in range(s_len // 512):
            rows = slice(r * 512, (r + 1) * 512)
            dst = slice(WINDOW + r * 512, WINDOW + (r + 1) * 512)
            kvs = kvs_ref[rows, :]
            kvw = kvw_ref[rows, :]
            ksel_ref[dst, :] = _rms_lanes(kvs[:, :HEAD_DIM], kg_ref[1:2, :]).astype(BF16)
            kwin_ref[dst, :] = _rms_lanes(kvw[:, :HEAD_DIM], kg_ref[2:3, :]).astype(BF16)
            vsel_ref[dst, :] = jnp.concatenate([kvs[:, HEAD_DIM:].astype(BF16), ones], axis=1)
            vwin_ref[dst, :] = jnp.concatenate([kvw[:, HEAD_DIM:].astype(BF16), ones], axis=1)
        half = CMP_STRIDE * HEAD_DIM
        for a_ref, w_ref, dst_ref, norm in ((ak_ref, wk_ref, kcmp_ref, True), (av_ref, wv_ref, vcmp_ref, False)):
            a = a_ref[...]
            pos_b = _dot_hi(pos_ref[...], w_ref[...])[0:1, :]
            p1 = _dot_hi(a, w_ref[0:half, :])
            p2 = _dot_hi(a, w_ref[half:2 * half, :])
            c = p1 + pltpu.roll(p2, nc - 1, 0) + pos_b
            if norm:
                hi, lo = _split2(_rms_lanes(c, kg_ref[0:1, :]))
                kcmp_ref[0] = hi
                kcmp_ref[1] = lo
            else:
                dst_ref[...] = c.astype(BF16)

    t0 = qi * tq
    qn = _rms_heads(q_ref[...], bd_ref[...], qg_ref[...]) * (QK_SCALE * LOG2E)
    qf = [qn[:, h * HEAD_DIM:(h + 1) * HEAD_DIM] for h in range(HEADS)]
    qh = [q.astype(BF16) for q in qf]
    ql = [(qf[h] - qh[h].astype(F32)).astype(BF16) for h in range(HEADS)]

    hs = range(HEADS)
    k_hi, k_lo = kcmp_ref[0], kcmp_ref[1]
    dn = (((1,), (1,)), ((), ()))
    sc = [lax.dot_general(qh[h], k_hi, dn, preferred_element_type=F32)
          + lax.dot_general(ql[h], k_hi, dn, preferred_element_type=F32)
          + lax.dot_general(qh[h], k_lo, dn, preferred_element_type=F32) + tc_ref[h] for h in hs]
    e = [jnp.exp2(sc[h] - jnp.max(sc[h], axis=1, keepdims=True)) for h in hs]
    p = [jnp.where(tc_ref[h] > 0.5 * NEG, e[h] / jnp.sum(e[h], axis=1, keepdims=True), 0.0) for h in hs]
    o_cmp = [jnp.dot(p[h].astype(BF16), vcmp_ref[...], preferred_element_type=F32) for h in hs]
    ps_hi, ps_lo = _split2(p[0] + p[1] + p[2] + p[3])
    ovt = ovt_ref[...].astype(BF16)
    imp_t = (lax.dot_general(ovt, ps_hi, dn, preferred_element_type=F32)
             + lax.dot_general(ovt, ps_lo, dn, preferred_element_type=F32))

    jj = lax.broadcasted_iota(jnp.int32, (128, tq), 0)
    tt = t0 + lax.broadcasted_iota(jnp.int32, (128, tq), 1)
    cur = tt // SEL_BLOCK
    forced = (jj == 0) | (jj == cur) | (jj == cur - 1)
    score = jnp.where(forced, -NEG, jnp.where(jj * SEL_BLOCK <= tt, imp_t, NEG))
    cnt = jnp.zeros((128, tq), F32)
    for i in range(n_sel):
        ri = score[i:i + 1, :]
        cnt = cnt + jnp.where(jj > i, (ri >= score).astype(F32), (ri > score).astype(F32))
    selb = (cnt < float(min(N_SEL_TOP, n_sel))).astype(F32).T.astype(BF16)

    m_ref[...] = jnp.full(m_ref.shape, NEG, F32)
    acc_ref[...] = jnp.zeros(acc_ref.shape, F32)
    jrow = lax.broadcasted_iota(jnp.int32, (128, tq), 0)
    jcol = lax.broadcasted_iota(jnp.int32, (128, tq), 1) // SEL_BLOCK

    def sel_chunk(kp0, near):
        jb = kp0 // SEL_BLOCK - WINDOW // SEL_BLOCK
        expand = (jrow == jb + jcol).astype(BF16)
        picked = jnp.dot(selb, expand, preferred_element_type=F32) > 0.5
        kblk = ksel_ref[pl.ds(kp0, tq), :]
        vblk = vsel_ref[pl.ds(kp0, tq), :]
        hs = range(HEADS)
        bias = [cb_ref[h] if near is None else tsb_ref[h, :, near * tq:(near + 1) * tq] for h in hs]
        sck = [_dot_t(qh[h], kblk) + jnp.where(picked, bias[h], NEG) for h in hs]
        m_old = [m_ref[h] for h in hs]
        m_new = [jnp.maximum(m_old[h], jnp.max(sck[h], axis=1, keepdims=True)) for h in hs]
        pk = [jnp.exp2(sck[h] - jnp.concatenate([m_new[h]] * (tq // 128), axis=1)).astype(BF16) for h in hs]
        pv = [jnp.dot(pk[h], vblk, preferred_element_type=F32) for h in hs]
        for h in hs:
            acc_ref[h] = jnp.exp2(m_old[h] - m_new[h]) * acc_ref[h] + pv[h]
            m_ref[h] = m_new[h]

    def far(cp, carry):
        sel_chunk(pl.multiple_of(WINDOW + 2 * cp * tq, tq), None)
        sel_chunk(pl.multiple_of(WINDOW + (2 * cp + 1) * tq, tq), None)
        return carry

    n_far = jnp.maximum(qi - WINDOW // tq, 0)
    lax.fori_loop(0, n_far // 2, far, 0)

    @pl.when(n_far % 2 == 1)
    def _():
        sel_chunk(pl.multiple_of(WINDOW + (n_far - 1) * tq, tq), None)
    for r in range(NSA_NEAR // tq):
        @pl.when(qi + r >= WINDOW // tq)
        def _(r=r):
            sel_chunk(pl.multiple_of(t0 + r * tq, tq), r)

    kw = kwin_ref[pl.ds(pl.multiple_of(t0, tq), NSA_NEAR), :]
    vw = vwin_ref[pl.ds(pl.multiple_of(t0, tq), NSA_NEAR), :]
    in_seq = lax.broadcasted_iota(jnp.int32, (tq, NSA_NEAR), 1) >= WINDOW - t0
    g = _sigmoid(gc_ref[...] + gb_ref[...])
    sw = [jnp.where(in_seq, _dot_t(qh[h], kw) + twb_ref[h], NEG) for h in hs]
    pw = [jnp.exp2(sw[h] - jnp.max(sw[h], axis=1, keepdims=True)).astype(BF16) for h in hs]
    ows = [jnp.dot(pw[h], vw, preferred_element_type=F32) for h in hs]
    outs = []
    for h in hs:
        ow = ows[h]
        o_win = ow[:, :HEAD_DIM] / ow[:, HEAD_DIM:]
        a = acc_ref[h]
        o_sel = a[:, :HEAD_DIM] / a[:, HEAD_DIM:]
        c0 = G_NG + 3 * h
        outs.append(g[:, c0:c0 + 1] * o_cmp[h] + g[:, c0 + 1:c0 + 2] * o_sel + g[:, c0 + 2:c0 + 3] * o_win)
    y_ref[...] = jnp.concatenate(outs, axis=1).astype(y_ref.dtype)


def _nsa(proj, a_k, a_v, l, bias_c, q_g, k_g, pos, wk, wv, bd, tc, ovt, tsb, twb, cb):
    nb, s, _ = proj.shape
    tq = NSA_TQ
    nc = s // CMP_STRIDE
    kern = functools.partial(_nsa_kernel, tq=tq, s_len=s, n_sel=s // SEL_BLOCK)
    const2 = lambda b, i: (0, 0)
    const3 = lambda b, i: (0, 0, 0)
    lay3 = lambda b, i: (l, 0, 0)
    return pl.pallas_call(
        kern,
        out_shape=jax.ShapeDtypeStruct((nb, s, GROUP), BF16),
        grid=(nb, s // tq),
        in_specs=[pl.BlockSpec((None, tq, GROUP), lambda b, i: (b, i, C_NQ // GROUP)),
                  pl.BlockSpec((None, s, 128), lambda b, i: (b, 0, C_NKV // 128 + 1)),
                  pl.BlockSpec((None, s, 128), lambda b, i: (b, 0, C_NKV // 128 + 2)),
                  pl.BlockSpec((None, nc, CMP_STRIDE * HEAD_DIM), lambda b, i: (b, 0, 0)),
                  pl.BlockSpec((None, nc, CMP_STRIDE * HEAD_DIM), lambda b, i: (b, 0, 0)),
                  pl.BlockSpec((None, tq, 128), lambda b, i: (b, i, C_GATES // 128)),
                  pl.BlockSpec((None, 1, 128), lay3),
                  pl.BlockSpec((None, 1, GROUP), lay3),
                  pl.BlockSpec((None, 8, HEAD_DIM), lay3),
                  pl.BlockSpec((None, 8, CMP_BLOCK * HEAD_DIM), lay3),
                  pl.BlockSpec((None, CMP_BLOCK * HEAD_DIM, HEAD_DIM), lay3),
                  pl.BlockSpec((None, CMP_BLOCK * HEAD_DIM, HEAD_DIM), lay3),
                  pl.BlockSpec((GROUP, GROUP), const2),
                  pl.BlockSpec((HEADS, tq, nc), lambda b, i: (0, i, 0)),
                  pl.BlockSpec((128, nc), const2),
                  pl.BlockSpec((HEADS, tq, NSA_NEAR), const3),
                  pl.BlockSpec((HEADS, tq, NSA_NEAR), const3),
                  pl.BlockSpec(memory_space=pltpu.SMEM)],
        out_specs=pl.BlockSpec((None, tq, GROUP), lambda b, i: (b, i, 0)),
        scratch_shapes=[pltpu.VMEM((s + WINDOW, HEAD_DIM), BF16),
                        pltpu.VMEM((s + WINDOW, 2 * HEAD_DIM), BF16),
                        pltpu.VMEM((s + WINDOW, HEAD_DIM), BF16),
                        pltpu.VMEM((s + WINDOW, 2 * HEAD_DIM), BF16),
                        pltpu.VMEM((2, nc, HEAD_DIM), BF16),
                        pltpu.VMEM((nc, HEAD_DIM), BF16),
                        pltpu.VMEM((HEADS, tq, 128), F32),
                        pltpu.VMEM((HEADS, tq, 2 * HEAD_DIM), F32)],
        compiler_params=_cp(("arbitrary", "arbitrary")),
        name="nsa",
    )(proj, proj, proj, a_k, a_v, proj, bias_c, q_g, k_g, pos, wk, wv, bd, tc, ovt, tsb, twb, cb)


def _outproj_kernel(x_ref, ya_ref, yb_ref, yc_ref, yd_ref, w_ref, mod_ref, g_ref, xo_ref, h_ref):
    acc = jnp.dot(ya_ref[...], w_ref[0:GROUP, :], preferred_element_type=F32)
    acc += jnp.dot(yb_ref[...], w_ref[GROUP:2 * GROUP, :], preferred_element_type=F32)
    acc += jnp.dot(yc_ref[...], w_ref[2 * GROUP:3 * GROUP, :], preferred_element_type=F32)
    acc += jnp.dot(yd_ref[...], w_ref[3 * GROUP:4 * GROUP, :], preferred_element_type=F32)
    xn = x_ref[...] + mod_ref[2:3, :] * acc
    xo_ref[...] = xn
    h = _rms_lanes(xn, g_ref[...]) * (1.0 + mod_ref[4:5, :]) + mod_ref[3:4, :]
    h_ref[...] = h.astype(h_ref.dtype)


def _outproj(x, ys, mod, l, w_out, norm_g, h_dtype):
    nb, s, d = x.shape
    tm = 512
    yspec = pl.BlockSpec((None, tm, GROUP), lambda b, i: (b, i, 0))
    xspec = pl.BlockSpec((None, tm, d), lambda b, i: (b, i, 0))
    return pl.pallas_call(
        _outproj_kernel,
        out_shape=(jax.ShapeDtypeStruct((nb, s, d), F32), jax.ShapeDtypeStruct((nb, s, d), h_dtype)),
        grid=(nb, s // tm),
        in_specs=[xspec, yspec, yspec, yspec, yspec,
                  pl.BlockSpec((None, d, d), lambda b, i: (l, 0, 0)),
                  pl.BlockSpec((None, None, 6, d), lambda b, i: (l, b, 0, 0)),
                  pl.BlockSpec((None, 1, d), lambda b, i: (l, 0, 0))],
        out_specs=(xspec, xspec),
        compiler_params=_cp(("arbitrary", "arbitrary")),
        name="outproj",
    )(x, *ys, w_out, mod, norm_g)


def _ffn_kernel(h_ref, x_ref, mod_ref, wg_ref, wu_ref, wd_ref, o_ref, acc_ref, *, nf):
    f = pl.program_id(2)

    @pl.when(f == 0)
    def _():
        acc_ref[...] = jnp.zeros_like(acc_ref)

    h = h_ref[...]
    a = jnp.dot(h, wg_ref[...], preferred_element_type=F32)
    u = jnp.dot(h, wu_ref[...], preferred_element_type=F32)
    act = (a * _sigmoid(a) * u).astype(BF16)
    acc_ref[...] += jnp.dot(act, wd_ref[...], preferred_element_type=F32)

    @pl.when(f == nf - 1)
    def _():
        o_ref[...] = x_ref[...] + mod_ref[5:6, :] * acc_ref[...]


def _ffn(h, x, mod, l, li, wg, wu, wd):
    nb, s, d = x.shape
    dff = wg.shape[-1]
    tm, tf = FFN_TM, FFN_TF
    nf = dff // tf
    xspec = pl.BlockSpec((None, tm, d), lambda b, i, f: (b, i, 0))
    return pl.pallas_call(
        functools.partial(_ffn_kernel, nf=nf),
        out_shape=jax.ShapeDtypeStruct((nb, s, d), F32),
        grid=(nb, s // tm, nf),
        in_specs=[xspec, xspec,
                  pl.BlockSpec((None, None, 6, d), lambda b, i, f: (l, b, 0, 0)),
                  pl.BlockSpec((None, d, tf), lambda b, i, f: (li, 0, f)),
                  pl.BlockSpec((None, d, tf), lambda b, i, f: (li, 0, f)),
                  pl.BlockSpec((None, tf, d), lambda b, i, f: (li, f, 0))],
        out_specs=xspec,
        scratch_shapes=[pltpu.VMEM((tm, d), F32)],
        compiler_params=_cp(("arbitrary", "arbitrary", "arbitrary")),
        name="ffn_dense",
    )(h, x, mod, wg, wu, wd)


def _router_kernel(h_ref, w_ref, b_ref, o_ref):
    logits = _dot_hi(h_ref[...], w_ref[...]) + b_ref[...]
    lane = lax.broadcasted_iota(jnp.int32, logits.shape, 1).astype(F32)
    lg = jnp.where(lane < N_EXPERTS, logits, -3e38)
    m1 = jnp.max(lg, axis=1, keepdims=True)
    i1 = jnp.min(jnp.where(lg == m1, lane, 128.0), axis=1, keepdims=True)
    lg2 = jnp.where(lane == i1, -3e38, lg)
    m2 = jnp.max(lg2, axis=1, keepdims=True)
    i2 = jnp.min(jnp.where(lg2 == m2, lane, 128.0), axis=1, keepdims=True)
    e2 = jnp.exp(m2 - m1)
    w1 = 1.0 / (1.0 + e2)
    w2 = e2 / (1.0 + e2)
    o_ref[...] = jnp.where(lane == 0, i1, jnp.where(lane == 1, i2, jnp.where(lane == 2, w1, jnp.where(lane == 3, w2, 0.0))))


def _router(hf, rw, rb):
    t, d = hf.shape
    tm = 1024
    return pl.pallas_call(
        _router_kernel,
        out_shape=jax.ShapeDtypeStruct((t, 128), F32),
        grid=(t // tm,),
        in_specs=[pl.BlockSpec((tm, d), lambda i: (i, 0)),
                  pl.BlockSpec((d, 128), lambda i: (0, 0)),
                  pl.BlockSpec((1, 128), lambda i: (0, 0))],
        out_specs=pl.BlockSpec((tm, 128), lambda i: (i, 0)),
        compiler_params=_cp(("arbitrary",)),
        name="moe_router",
    )(hf, rw, rb)


def _experts_kernel(be_ref, dst_ref, nv_ref, h_hbm, wg_ref, wu_ref, wd_ref, o_hbm,
                    buf, xb_ref, gsem, ssem, *, bm, nf, n_tok):
    del be_ref
    i = pl.program_id(0)
    f = pl.program_id(1)
    nvalid = nv_ref[0]
    slot = i % 2
    chunk = bm // nf

    ACC, XB = 2, 3
    d = buf.shape[2]
    ncol = 4
    cw = d // ncol

    def gather(blk, sl, r):
        tok = jnp.minimum(dst_ref[(blk + 1) * bm + r] >> 1, n_tok - 1)
        return pltpu.make_async_copy(h_hbm.at[pl.ds(tok, 1)], buf.at[XB + sl, pl.ds(r, 1)], gsem.at[sl])

    def scatter(blk, sl, r):
        row = dst_ref[(blk + 1) * bm + r]
        return pltpu.make_async_copy(buf.at[sl, pl.ds(r, 1)], o_hbm.at[pl.ds(row, 1)], ssem.at[sl])

    def for_rows(fn):
        def body(r, c):
            fn(r)
            return c
        lax.fori_loop(0, bm, body, 0, unroll=8)

    @pl.when((f == 0) & (i == 0))
    def _():
        for_rows(lambda r: gather(0, 0, r).start())
        buf[1] = jnp.zeros((bm, d), F32)

    @pl.when((f == 0) & (i < nvalid))
    def _():
        for_rows(lambda r: gather(i, slot, r).wait())
        xb_ref[...] = buf[XB + slot].astype(BF16)
        buf[ACC] = jnp.zeros((bm, d), F32)

    def compute(prefetch):
        x = xb_ref[...]
        a = jnp.dot(x, wg_ref[...], preferred_element_type=F32)
        u = jnp.dot(x, wu_ref[...], preferred_element_type=F32)
        act = (a * _sigmoid(a) * u).astype(BF16)
        per = chunk // ncol
        for c in range(ncol):
            for j in range(c * per, (c + 1) * per):
                if prefetch:
                    gather(i + 1, 1 - slot, f * chunk + j).start()
                scatter(i - 1, 1 - slot, f * chunk + j).start()
            cols = slice(c * cw, (c + 1) * cw)
            buf[ACC, :, cols] += jnp.dot(act, wd_ref[:, cols], preferred_element_type=F32)

    @pl.when(i + 1 < nvalid)
    def _():
        compute(True)

    @pl.when(i + 1 == nvalid)
    def _():
        compute(False)

    @pl.when((f == 0) & (i >= nvalid))
    def _():
        buf[ACC] = jnp.zeros((bm, d), F32)
        fill = pltpu.make_async_copy(buf.at[ACC], o_hbm.at[pl.ds(pl.multiple_of(i * bm, bm), bm)], ssem.at[0])
        fill.start()
        fill.wait()

    @pl.when((f == nf - 1) & (i < nvalid))
    def _():
        @pl.when(i >= 1)
        def _():
            for_rows(lambda r: scatter(i - 2, slot, r).wait())

        buf[slot] = buf[ACC]

        @pl.when(i + 1 == nvalid)
        def _():
            for_rows(lambda r: scatter(i, slot, r).start())
            for_rows(lambda r: scatter(i - 1, 1 - slot, r).wait())
            for_rows(lambda r: scatter(i, slot, r).wait())


def _experts(hf, blk_expert, row_dst, nvalid, wg, wu, wd, li):
    t, d = hf.shape
    bm, tf = MOE_BM, MOE_TF
    n_rows = row_dst.shape[0]
    n_blocks = n_rows // bm - 1
    dff = wg.shape[-1]
    nf = dff // tf

    def fidx(i, f, nv):
        return jnp.where(i < nv[0], f, nf - 1)

    grid_spec = pltpu.PrefetchScalarGridSpec(
        num_scalar_prefetch=3,
        grid=(n_blocks, nf),
        in_specs=[pl.BlockSpec(memory_space=pl.ANY),
                  pl.BlockSpec((None, None, d, tf), lambda i, f, be, tok, nv: (li, be[i], 0, fidx(i, f, nv))),
                  pl.BlockSpec((None, None, d, tf), lambda i, f, be, tok, nv: (li, be[i], 0, fidx(i, f, nv))),
                  pl.BlockSpec((None, None, tf, d), lambda i, f, be, tok, nv: (li, be[i], fidx(i, f, nv), 0))],
        out_specs=pl.BlockSpec(memory_space=pl.ANY),
        scratch_shapes=[pltpu.VMEM((5, bm, d), F32),
                        pltpu.VMEM((bm, d), BF16),
                        pltpu.SemaphoreType.DMA((2,)),
                        pltpu.SemaphoreType.DMA((2,))],
    )
    return pl.pallas_call(
        functools.partial(_experts_kernel, bm=bm, nf=nf, n_tok=t),
        out_shape=jax.ShapeDtypeStruct((n_rows, d), F32),
        grid_spec=grid_spec,
        compiler_params=_cp(("arbitrary", "arbitrary")),
        name="moe_experts",
    )(blk_expert, row_dst, nvalid, hf, wg, wu, wd)


def _combine_kernel(y_ref, x_ref, mod_ref, rt_ref, o_ref):
    d = x_ref.shape[-1]
    rt = rt_ref[...]
    y = y_ref[...]
    o_ref[...] = x_ref[...] + mod_ref[5:6, :] * (rt[:, 2:3] * y[:, :d] + rt[:, 3:4] * y[:, d:])


def _combine(y2, x, mod, route, l):
    nb, s, d = x.shape
    tm = 512
    nt = s // tm
    return pl.pallas_call(
        _combine_kernel,
        out_shape=jax.ShapeDtypeStruct((nb, s, d), F32),
        grid=(nb, nt),
        in_specs=[pl.BlockSpec((tm, 2 * d), lambda b, i: (b * nt + i, 0)),
                  pl.BlockSpec((None, tm, d), lambda b, i: (b, i, 0)),
                  pl.BlockSpec((None, None, 6, d), lambda b, i: (l, b, 0, 0)),
                  pl.BlockSpec((None, tm, 128), lambda b, i: (b, i, 0))],
        out_specs=pl.BlockSpec((None, tm, d), lambda b, i: (b, i, 0)),
        compiler_params=_cp(("arbitrary", "arbitrary")),
        name="moe_combine",
    )(y2, x, mod, route.reshape(nb, s, 128))


def _moe(hf32, x, mod, l, li, router_w, router_b, wg, wu, wd):
    nb, s, d = x.shape
    t = nb * s
    bm = MOE_BM
    hf = hf32.reshape(t, d)
    rw = jnp.zeros((d, 128), F32).at[:, :N_EXPERTS].set(router_w[li])
    rb = jnp.zeros((1, 128), F32).at[0, :N_EXPERTS].set(router_b[li])
    route = _router(hf, rw, rb)
    e_flat = route[:, 0:2].astype(jnp.int32).reshape(-1)
    n_assign = 2 * t
    onehot = (e_flat[:, None] == jnp.arange(N_EXPERTS, dtype=jnp.int32)[None, :]).astype(jnp.int32)
    csum = jnp.cumsum(onehot, axis=0)
    rank = jnp.sum(onehot * csum, axis=1) - 1
    counts = csum[-1]
    padded = (counts + bm - 1) // bm * bm
    cum_padded = jnp.cumsum(padded)
    pstart = cum_padded - padded
    dest = pstart[e_flat] + rank
    n_blocks = n_assign // bm + N_EXPERTS
    n_rows = n_blocks * bm
    row_asg = jnp.full((n_rows,), -1, jnp.int32).at[dest].set(jnp.arange(n_assign, dtype=jnp.int32))
    spare = n_assign + jnp.cumsum((row_asg < 0).astype(jnp.int32)) - 1
    row_dst = jnp.where(row_asg >= 0, row_asg, spare)
    row_dst = jnp.concatenate([n_rows + jnp.arange(bm, dtype=jnp.int32), row_dst])
    blk_start = jnp.arange(n_blocks, dtype=jnp.int32) * bm
    blk_expert = jnp.minimum(jnp.searchsorted(cum_padded, blk_start, side='right'), N_EXPERTS - 1).astype(jnp.int32)
    nvalid = (cum_padded[-1] // bm).astype(jnp.int32).reshape(1)
    y = _experts(hf, blk_expert, row_dst, nvalid, wg, wu, wd, li)
    return _combine(y.reshape((n_rows + bm) // 2, 2 * d), x, mod, route, l)


def _t5_bucket(dist):
    n = np.maximum(dist, 0)
    max_exact = NUM_BUCKETS // 2
    large = max_exact + (np.log(np.maximum(n, 1).astype(np.float32) / max_exact)
                         / math.log(MAX_DISTANCE / max_exact) * (NUM_BUCKETS - max_exact)).astype(np.int32)
    return np.where(n < max_exact, n, np.minimum(large, NUM_BUCKETS - 1)).astype(np.int32)


def _bias_by_bucket(rb, bucket):
    ids = jnp.asarray(bucket.astype(np.int8))[None]
    out = jnp.zeros((rb.shape[1],) + bucket.shape, F32)
    for k in range(NUM_BUCKETS):
        out = jnp.where(ids == k, rb[k][:, None, None], out)
    return out


def _nsa_tables(rel_bias, s):
    tq = NSA_TQ
    nc = s // CMP_STRIDE
    n_cmp = (s - CMP_BLOCK) // CMP_STRIDE + 1
    rb = rel_bias.astype(F32) * LOG2E
    t = np.arange(s)[:, None]
    cmp_end = np.arange(nc)[None, :] * CMP_STRIDE + CMP_BLOCK - 1
    ok = (cmp_end <= t) & (np.arange(nc)[None, :] < n_cmp)
    tc = jnp.where(jnp.asarray(ok)[None], _bias_by_bucket(rb, _t5_bucket(t - cmp_end)), NEG)
    dist = np.arange(tq)[:, None] - (np.arange(NSA_NEAR)[None, :] - WINDOW)
    tb = _bias_by_bucket(rb, _t5_bucket(dist))
    tsb = jnp.where(jnp.asarray(dist >= 0)[None], tb, NEG)
    twb = jnp.where(jnp.asarray((dist >= 0) & (dist < WINDOW))[None], tb, NEG)
    far_bucket = int(_t5_bucket(np.array([WINDOW]))[0])
    cb = rb[far_bucket]
    n_sel = s // SEL_BLOCK
    cs = np.arange(nc)[None, :] * CMP_STRIDE
    ss = np.arange(128)[:, None] * SEL_BLOCK
    ov = np.clip(np.minimum(cs + CMP_BLOCK, ss + SEL_BLOCK) - np.maximum(cs, ss), 0, None).astype(np.float32) / CMP_BLOCK
    ov = ov * (np.arange(128)[:, None] < n_sel) * (np.arange(nc)[None, :] < n_cmp)
    return tc, jnp.asarray(ov, F32), tsb, twb, cb


def _prep_layer_params(w_in, mlstm_gate_b, fox_f_b, nsa_gate_b):
    depth, d, _ = w_in.shape
    small = [w_in[:, :, 1024:1032], w_in[:, :, 1800:1804], w_in[:, :, 3212:3224]]
    w_p = jnp.concatenate([w_in[:, :, 0:1024], w_in[:, :, 1032:1800], w_in[:, :, 1804:2572], w_in[:, :, 2572:3212]]
                          + small + [jnp.zeros((depth, d, 128 - 24), w_in.dtype)], axis=-1).astype(BF16)
    wgt = jnp.concatenate(small + [jnp.zeros((depth, d, GT_ROWS - 24), w_in.dtype)], axis=-1)
    wgt = jnp.transpose(wgt, (0, 2, 1)).astype(BF16)
    gate_b = jnp.concatenate([mlstm_gate_b, fox_f_b, nsa_gate_b], axis=-1).astype(F32)
    bias_c = jnp.zeros((depth, 1, 128), F32).at[:, 0, :24].set(gate_b)
    bias_r = jnp.zeros((depth, GT_ROWS, 1), F32).at[:, :24, 0].set(gate_b)
    return w_p, wgt, bias_c, bias_r


def kernel(x, c, rel_bias, ada_w, ada_b, norm1_g, norm2_g, w_in, w_out, mlstm_conv_w, mlstm_gate_b, mlstm_out_g,
           fox_f_b, fox_q_g, fox_k_g, nsa_q_g, nsa_k_g, nsa_cmp_pos, nsa_cmp_wk, nsa_cmp_wv, nsa_gate_b,
           ffn_wg, ffn_wu, ffn_wd, moe_router_w, moe_router_b, moe_wg, moe_wu, moe_wd):
    nb, s, d = x.shape
    depth = w_in.shape[0]
    assert d == D_MODEL and s % 512 == 0 and s // SEL_BLOCK <= 128

    w_p, wgt, bias_c, bias_r = _prep_layer_params(w_in, mlstm_gate_b, fox_f_b, nsa_gate_b)
    w_out_b = w_out.astype(BF16)
    n1g = norm1_g.reshape(depth, 1, d)
    n2g = norm2_g.reshape(depth, 1, d)
    m_out_g = mlstm_out_g.reshape(depth, 1, GROUP)
    fq_g = jnp.tile(fox_q_g, (1, HEADS)).reshape(depth, 1, GROUP)
    fk_g = jnp.tile(fox_k_g, (1, HEADS)).reshape(depth, 1, GROUP)
    nq_g = jnp.tile(nsa_q_g, (1, HEADS)).reshape(depth, 1, GROUP)
    nk_g = jnp.zeros((depth, 8, HEAD_DIM), F32).at[:, :3].set(nsa_k_g)
    pos8 = jnp.zeros((depth, 8, CMP_BLOCK * HEAD_DIM), F32).at[:, 0].set(nsa_cmp_pos.reshape(depth, -1))
    ffn_wg_b, ffn_wu_b, ffn_wd_b = ffn_wg.astype(BF16), ffn_wu.astype(BF16), ffn_wd.astype(BF16)
    moe_wg_b, moe_wu_b, moe_wd_b = moe_wg.astype(BF16), moe_wu.astype(BF16), moe_wd.astype(BF16)
    hid = np.arange(GROUP) // HEAD_DIM
    bd = jnp.asarray((hid[:, None] == hid[None, :]).astype(np.float32) / HEAD_DIM)
    tri_l = jnp.asarray(np.tril(np.ones((MLSTM_L, MLSTM_L), np.float32)))
    tri_t = jnp.asarray(np.tril(np.ones((ATT_T, ATT_T), np.float32)))
    from_here = jnp.asarray(np.tril(np.ones((ATT_T, ATT_T), np.float32)), BF16)
    tc, ovt, tsb, twb, cb = _nsa_tables(rel_bias, s)

    mod = _adaln(c, ada_w, ada_b)
    for l in range(depth):
        proj, gt = _inproj(x, mod, l, n1g, w_p, wgt)
        y_a = _mlstm(proj, gt, l, mlstm_conv_w, bias_c, bias_r, m_out_g, tri_l)
        y_b = _fox(proj, gt, l, bias_r, fq_g, fk_g, bd, tri_t)
        y_c = _sb(proj, from_here)
        a_k = proj[:, :, C_NKV:C_NKV + HEAD_DIM].reshape(nb, s // CMP_STRIDE, CMP_STRIDE * HEAD_DIM)
        a_v = proj[:, :, C_NKV + HEAD_DIM:C_NKV + 2 * HEAD_DIM].reshape(nb, s // CMP_STRIDE, CMP_STRIDE * HEAD_DIM)
        y_d = _nsa(proj, a_k, a_v, l, bias_c, nq_g, nk_g, pos8, nsa_cmp_wk, nsa_cmp_wv, bd, tc, ovt, tsb, twb, cb)
        if l % 2 == 0:
            x, h2 = _outproj(x, (y_a, y_b, y_c, y_d), mod, l, w_out_b, n2g, BF16)
            x = _ffn(h2, x, mod, l, l // 2, ffn_wg_b, ffn_wu_b, ffn_wd_b)
        else:
            x, h2 = _outproj(x, (y_a, y_b, y_c, y_d), mod, l, w_out_b, n2g, F32)
            x = _moe(h2, x, mod, l, l // 2, moe_router_w, moe_router_b, moe_wg_b, moe_wu_b, moe_wd_b)
    return x
```

```python
import functools
import math

import numpy as np
import jax
import jax.numpy as jnp
from jax import lax
from jax.experimental import pallas as pl
from jax.experimental.pallas import tpu as pltpu

F32 = jnp.float32
BF16 = jnp.bfloat16
HI = lax.Precision.HIGHEST

D_MODEL = 1024
HEADS = 4
HEAD_DIM = 64
GROUP = HEADS * HEAD_DIM
NORM_EPS = 1e-6
NEG = -1e30
QK_SCALE = HEAD_DIM ** -0.5
LOG2E = 1.4426950408889634
CMP_BLOCK = 32
CMP_STRIDE = 16
SEL_BLOCK = 64
N_SEL_TOP = 16
WINDOW = 512
NUM_BUCKETS = 32
MAX_DISTANCE = 128
N_EXPERTS = 8
VMEM_LIMIT = 56 * 1024 * 1024

PW = 3328
C_MQK, C_MV, C_MO = 0, 512, 768
C_FQ, C_FK, C_FV = 1024, 1280, 1536
C_SQ, C_SK, C_SV = 1792, 2048, 2304
C_NQ, C_NKV, C_GATES = 2560, 2816, 3200
G_MI, G_MF, G_FF, G_NG = 0, 4, 8, 12
GT_ROWS = 32

MLSTM_TS = 512
MLSTM_L = 128
ATT_T = 256
NSA_TQ = 256
NSA_NEAR = WINDOW + NSA_TQ
MOE_BM = 512
MOE_TF = 896
FFN_TM = 512
FFN_TF = 1408


def _cp(sem, vmem=VMEM_LIMIT):
    return pltpu.CompilerParams(dimension_semantics=sem, vmem_limit_bytes=vmem)


def _dot(a, b):
    return jnp.dot(a.astype(BF16), b.astype(BF16), preferred_element_type=F32)


def _dot_t(a, b):
    return lax.dot_general(a.astype(BF16), b.astype(BF16), (((1,), (1,)), ((), ())), preferred_element_type=F32)


def _dot_hi(a, b):
    return jnp.dot(a, b, precision=HI, preferred_element_type=F32)


def _dot_t_hi(a, b):
    return lax.dot_general(a, b, (((1,), (1,)), ((), ())), precision=HI, preferred_element_type=F32)


def _sigmoid(x):
    return 1.0 / (1.0 + jnp.exp(-x))


def _log_sigmoid(x):
    return jnp.minimum(x, 0.0) - jnp.log1p(jnp.exp(-jnp.abs(x)))


def _rms_lanes(x, g):
    return x * lax.rsqrt(jnp.mean(x * x, axis=-1, keepdims=True) + NORM_EPS) * g


def _split2(a):
    hi = a.astype(BF16)
    return hi, (a - hi.astype(F32)).astype(BF16)


def _dot_2x(a, b):
    hi, lo = _split2(a)
    bb = b.astype(BF16)
    return jnp.dot(hi, bb, preferred_element_type=F32) + jnp.dot(lo, bb, preferred_element_type=F32)


def _dot_t_2x(a, b):
    hi, lo = _split2(a)
    bb = b.astype(BF16)
    dn = (((1,), (1,)), ((), ()))
    return (lax.dot_general(hi, bb, dn, preferred_element_type=F32)
            + lax.dot_general(lo, bb, dn, preferred_element_type=F32))


def _rms_heads(x, bd, g):
    return x * lax.rsqrt(_dot_2x(x * x, bd) + NORM_EPS) * g


def _adaln_kernel(c_ref, w_ref, b_ref, o_ref):
    c = c_ref[...]
    o_ref[...] = _dot_hi(c * _sigmoid(c), w_ref[...]) + b_ref[...]


def _adaln(c, ada_w, ada_b):
    depth, d, six_d = ada_w.shape
    nb = c.shape[0]
    out = pl.pallas_call(
        _adaln_kernel,
        out_shape=jax.ShapeDtypeStruct((depth, nb, six_d), F32),
        grid=(depth, six_d // d),
        in_specs=[pl.BlockSpec((nb, d), lambda l, j: (0, 0)),
                  pl.BlockSpec((None, d, d), lambda l, j: (l, 0, j)),
                  pl.BlockSpec((None, 1, d), lambda l, j: (l, 0, j))],
        out_specs=pl.BlockSpec((None, nb, d), lambda l, j: (l, 0, j)),
        compiler_params=_cp(("arbitrary", "arbitrary")),
        name="adaln",
    )(c, ada_w, ada_b.reshape(depth, 1, six_d))
    return out.reshape(depth, nb, 6, d)


def _inproj_kernel(x_ref, mod_ref, g_ref, w_ref, wgt_ref, o_ref, gt_ref):
    h = _rms_lanes(x_ref[...], g_ref[...]) * (1.0 + mod_ref[1:2, :]) + mod_ref[0:1, :]
    hb = h.astype(BF16)
    o_ref[...] = jnp.dot(hb, w_ref[...], preferred_element_type=F32)
    gt_ref[...] = lax.dot_general(wgt_ref[...], hb, (((1,), (1,)), ((), ())), preferred_element_type=F32)


def _inproj(x, mod, l, norm_g, w_p, wgt):
    nb, s, d = x.shape
    tm = 256
    return pl.pallas_call(
        _inproj_kernel,
        out_shape=(jax.ShapeDtypeStruct((nb, s, PW), F32), jax.ShapeDtypeStruct((nb, GT_ROWS, s), F32)),
        grid=(nb, s // tm),
        in_specs=[pl.BlockSpec((None, tm, d), lambda b, i: (b, i, 0)),
                  pl.BlockSpec((None, None, 6, d), lambda b, i: (l, b, 0, 0)),
                  pl.BlockSpec((None, 1, d), lambda b, i: (l, 0, 0)),
                  pl.BlockSpec((None, d, PW), lambda b, i: (l, 0, 0)),
                  pl.BlockSpec((None, GT_ROWS, d), lambda b, i: (l, 0, 0))],
        out_specs=(pl.BlockSpec((None, tm, PW), lambda b, i: (b, i, 0)),
                   pl.BlockSpec((None, GT_ROWS, tm), lambda b, i: (b, 0, i))),
        compiler_params=_cp(("arbitrary", "arbitrary")),
        name="inproj",
    )(x, mod, norm_g, w_p, wgt)


def _mlstm_kernel(qk_ref, v_ref, op_ref, gc_ref, gr_ref, cw_ref, bc_ref, br_ref, og_ref, tri_ref, y_ref,
                  prev_ref, qs_ref, ks_ref, c_ref, n_ref, m_ref, *, ts, cl):
    @pl.when(pl.program_id(1) == 0)
    def _():
        prev_ref[...] = jnp.zeros_like(prev_ref)
        c_ref[...] = jnp.zeros_like(c_ref)
        n_ref[...] = jnp.zeros_like(n_ref)
        m_ref[...] = jnp.zeros_like(m_ref)

    x = qk_ref[...]
    xc = jnp.concatenate([prev_ref[...], x], axis=0)
    cw = cw_ref[...]
    y = (cw[0:1] * xc[8:8 + ts] + cw[1:2] * xc[7:7 + ts] + cw[2:3] * xc[6:6 + ts] + cw[3:4] * xc[5:5 + ts])
    prev_ref[...] = x[ts - 8:ts]
    y = y * _sigmoid(y)
    qs_ref[...] = y[:, :GROUP]
    ks_ref[...] = y[:, GROUP:] * QK_SCALE

    tri = tri_ref[...]
    lower = lax.broadcasted_iota(jnp.int32, (cl, cl), 0) >= lax.broadcasted_iota(jnp.int32, (cl, cl), 1)

    def chunk(ci, carry):
        r0 = pl.multiple_of(ci * cl, cl)
        gc = gc_ref[pl.ds(r0, cl), :] + bc_ref[...]
        gr = gr_ref[0:8, pl.ds(r0, cl)] + br_ref[0:8, :]
        lf_hi, lf_lo = _split2(_log_sigmoid(gc))
        trib = tri.astype(BF16)
        b_c = (jnp.dot(trib, lf_hi, preferred_element_type=F32)
               + jnp.dot(trib, lf_lo, preferred_element_type=F32))
        b_r = _dot_t_2x(_log_sigmoid(gr), tri)
        q = qs_ref[pl.ds(r0, cl), :]
        k = ks_ref[pl.ds(r0, cl), :]
        v = v_ref[pl.ds(r0, cl), :]
        hs = range(HEADS)
        sls = [slice(h * HEAD_DIM, (h + 1) * HEAD_DIM) for h in hs]
        ig_col = [gc[:, G_MI + h:G_MI + h + 1] for h in hs]
        b_col = [b_c[:, G_MF + h:G_MF + h + 1] for h in hs]
        ig_row = [gr[G_MI + h:G_MI + h + 1, :] for h in hs]
        b_row = [b_r[G_MF + h:G_MF + h + 1, :] for h in hs]
        qh = [q[:, sl] for sl in sls]
        kf = [k[:, sl] for sl in sls]
        kh = [x.astype(BF16) for x in kf]
        vh = [v[:, sl] for sl in sls]
        qb = [x.astype(BF16) for x in qh]
        qk = [lax.dot_general(qb[h], kh[h], (((1,), (1,)), ((), ())), preferred_element_type=F32) for h in hs]
        c_prev = [c_ref[h] for h in hs]
        n_prev = [n_ref[h] for h in hs]
        m_prev = [m_ref[h][:, 0:1] for h in hs]
        qc = [_dot_t(qb[h], c_prev[h]) for h in hs]
        g = [b_row[h][:, cl - 1:cl] for h in hs]
        m_loc = [jnp.max(g[h] - b_row[h] + ig_row[h], axis=1, keepdims=True) for h in hs]
        ew_col = [jnp.exp(g[h] - b_col[h] + ig_col[h] - m_loc[h]) for h in hs]
        c_chunk = [lax.dot_general((vh[h] * ew_col[h]).astype(BF16), kh[h], (((0,), (0,)), ((), ())),
                                   preferred_element_type=F32) for h in hs]
        n_chunk = [jnp.sum(ew_col[h] * kf[h], axis=0, keepdims=True) for h in hs]
        d_log = [jnp.where(lower, b_col[h] - b_row[h] + ig_row[h], NEG) for h in hs]
        m_inter = [b_col[h] + m_prev[h] for h in hs]
        m_t = [jnp.maximum(jnp.max(d_log[h], axis=1, keepdims=True), m_inter[h]) for h in hs]
        s = [qk[h] * jnp.exp(d_log[h] - m_t[h]) for h in hs]
        inter_w = [jnp.exp(m_inter[h] - m_t[h]) for h in hs]
        sv = [_dot(s[h], vh[h]) for h in hs]
        outs = []
        for h in hs:
            num = sv[h] + inter_w[h] * qc[h]
            den = (jnp.sum(s[h], axis=1, keepdims=True)
                   + inter_w[h] * jnp.sum(qh[h] * n_prev[h], axis=1, keepdims=True))
            hh = num / jnp.maximum(jnp.abs(den), jnp.exp(-m_t[h]))
            m_new = jnp.maximum(g[h] + m_prev[h], m_loc[h])
            a = jnp.exp(g[h] + m_prev[h] - m_new)
            bb = jnp.exp(m_loc[h] - m_new)
            c_ref[h] = a * c_prev[h] + bb * c_chunk[h]
            n_ref[h] = a * n_prev[h] + bb * n_chunk[h]
            m_ref[h] = jnp.broadcast_to(m_new, (1, 128))
            outs.append(_rms_lanes(hh, og_ref[:, sls[h]]))
        hcat = jnp.concatenate(outs, axis=1)
        y_ref[pl.ds(r0, cl), :] = (hcat * _sigmoid(op_ref[pl.ds(r0, cl), :])).astype(y_ref.dtype)
        return carry

    lax.fori_loop(0, ts // cl, chunk, 0)


def _mlstm(proj, gt, l, conv_w, bias_c, bias_r, out_g, tri):
    nb, s, _ = proj.shape
    ts, cl = MLSTM_TS, MLSTM_L
    kern = functools.partial(_mlstm_kernel, ts=ts, cl=cl)
    return pl.pallas_call(
        kern,
        out_shape=jax.ShapeDtypeStruct((nb, s, GROUP), BF16),
        grid=(nb, s // ts),
        in_specs=[pl.BlockSpec((None, ts, 2 * GROUP), lambda b, i: (b, i, C_MQK // (2 * GROUP))),
                  pl.BlockSpec((None, ts, GROUP), lambda b, i: (b, i, C_MV // GROUP)),
                  pl.BlockSpec((None, ts, GROUP), lambda b, i: (b, i, C_MO // GROUP)),
                  pl.BlockSpec((None, ts, 128), lambda b, i: (b, i, C_GATES // 128)),
                  pl.BlockSpec((None, GT_ROWS, ts), lambda b, i: (b, 0, i)),
                  pl.BlockSpec((None, 4, 2 * GROUP), lambda b, i: (l, 0, 0)),
                  pl.BlockSpec((None, 1, 128), lambda b, i: (l, 0, 0)),
                  pl.BlockSpec((None, GT_ROWS, 1), lambda b, i: (l, 0, 0)),
                  pl.BlockSpec((None, 1, GROUP), lambda b, i: (l, 0, 0)),
                  pl.BlockSpec((cl, cl), lambda b, i: (0, 0))],
        out_specs=pl.BlockSpec((None, ts, GROUP), lambda b, i: (b, i, 0)),
        scratch_shapes=[pltpu.VMEM((8, 2 * GROUP), F32),
                        pltpu.VMEM((ts, GROUP), F32),
                        pltpu.VMEM((ts, GROUP), F32),
                        pltpu.VMEM((HEADS, HEAD_DIM, HEAD_DIM), F32),
                        pltpu.VMEM((HEADS, 1, HEAD_DIM), F32),
                        pltpu.VMEM((HEADS, 1, 128), F32)],
        compiler_params=_cp(("arbitrary", "arbitrary")),
        name="mlstm",
    )(proj, proj, proj, proj, gt, conv_w, bias_c, bias_r, out_g, tri)


def _fox_kernel(q_ref, k_ref, v_ref, gr_ref, br_ref, qg_ref, kg_ref, bd_ref, tri_ref, y_ref,
                kn_ref, vx_ref, fn_ref, m_ref, acc_ref, *, t, s_len):
    qi = pl.program_id(1)
    bd = bd_ref[...]

    @pl.when(qi == 0)
    def _():
        kg = kg_ref[...]
        ones = jnp.ones((t, HEAD_DIM), BF16)
        for r in range(s_len // t):
            rows = slice(r * t, (r + 1) * t)
            kn = _rms_heads(k_ref[rows, :], bd, kg)
            vv = v_ref[rows, :]
            for h in range(HEADS):
                sl = slice(h * HEAD_DIM, (h + 1) * HEAD_DIM)
                kn_ref[h, rows, :] = kn[:, sl].astype(BF16)
                vx_ref[h, rows, :] = jnp.concatenate([vv[:, sl].astype(BF16), ones], axis=1)
        carry = jnp.zeros((8, 1), F32)
        for r in range(s_len // t):
            cols = slice(r * t, (r + 1) * t)
            lf = _log_sigmoid(gr_ref[G_FF:G_FF + 8, cols] + br_ref[G_FF:G_FF + 8, :])
            cs = _dot_t_hi(lf, tri_ref[...]) + carry
            fn_ref[:, cols] = -LOG2E * cs
            carry = cs[:, t - 1:t]

    qn = _rms_heads(q_ref[...], bd, qg_ref[...]) * (QK_SCALE * LOG2E)
    qh = [qn[:, h * HEAD_DIM:(h + 1) * HEAD_DIM].astype(BF16) for h in range(HEADS)]
    causal = lax.broadcasted_iota(jnp.int32, (t, t), 1) <= lax.broadcasted_iota(jnp.int32, (t, t), 0)
    m_ref[...] = jnp.full(m_ref.shape, NEG, F32)
    acc_ref[...] = jnp.zeros(acc_ref.shape, F32)

    def step(kb, masked):
        k0 = pl.multiple_of(kb * t, t)
        hs = range(HEADS)
        sc = [_dot_t(qh[h], kn_ref[h, pl.ds(k0, t), :]) + fn_ref[h:h + 1, pl.ds(k0, t)] for h in hs]
        if masked:
            sc = [jnp.where(causal, s, NEG) for s in sc]
        m_old = [m_ref[h] for h in hs]
        m_new = [jnp.maximum(m_old[h], jnp.max(sc[h], axis=1, keepdims=True)) for h in hs]
        p = [jnp.exp2(sc[h] - jnp.concatenate([m_new[h]] * (t // 128), axis=1)).astype(BF16) for h in hs]
        pv = [jnp.dot(p[h], vx_ref[h, pl.ds(k0, t), :], preferred_element_type=F32) for h in hs]
        for h in hs:
            acc_ref[h] = jnp.exp2(m_old[h] - m_new[h]) * acc_ref[h] + pv[h]
            m_ref[h] = m_new[h]

    def body(kp, c):
        step(2 * kp, False)
        step(2 * kp + 1, False)
        return c

    lax.fori_loop(0, qi // 2, body, 0)

    @pl.when(qi % 2 == 1)
    def _():
        step(qi - 1, False)

    step(qi, True)
    outs = []
    for h in range(HEADS):
        a = acc_ref[h]
        outs.append(a[:, :HEAD_DIM] / a[:, HEAD_DIM:])
    y_ref[...] = jnp.concatenate(outs, axis=1).astype(y_ref.dtype)


def _fox(proj, gt, l, bias_r, q_g, k_g, bd, tri):
    nb, s, _ = proj.shape
    t = ATT_T
    kern = functools.partial(_fox_kernel, t=t, s_len=s)
    return pl.pallas_call(
        kern,
        out_shape=jax.ShapeDtypeStruct((nb, s, GROUP), BF16),
        grid=(nb, s // t),
        in_specs=[pl.BlockSpec((None, t, GROUP), lambda b, i: (b, i, C_FQ // GROUP)),
                  pl.BlockSpec((None, s, GROUP), lambda b, i: (b, 0, C_FK // GROUP)),
                  pl.BlockSpec((None, s, GROUP), lambda b, i: (b, 0, C_FV // GROUP)),
                  pl.BlockSpec((None, GT_ROWS, s), lambda b, i: (b, 0, 0)),
                  pl.BlockSpec((None, GT_ROWS, 1), lambda b, i: (l, 0, 0)),
                  pl.BlockSpec((None, 1, GROUP), lambda b, i: (l, 0, 0)),
                  pl.BlockSpec((None, 1, GROUP), lambda b, i: (l, 0, 0)),
                  pl.BlockSpec((GROUP, GROUP), lambda b, i: (0, 0)),
                  pl.BlockSpec((t, t), lambda b, i: (0, 0))],
        out_specs=pl.BlockSpec((None, t, GROUP), lambda b, i: (b, i, 0)),
        scratch_shapes=[pltpu.VMEM((HEADS, s, HEAD_DIM), BF16),
                        pltpu.VMEM((HEADS, s, 2 * HEAD_DIM), BF16),
                        pltpu.VMEM((8, s), F32),
                        pltpu.VMEM((HEADS, t, 128), F32),
                        pltpu.VMEM((HEADS, t, 2 * HEAD_DIM), F32)],
        compiler_params=_cp(("arbitrary", "arbitrary")),
        name="fox",
    )(proj, proj, proj, gt, bias_r, q_g, k_g, bd, tri)


def _sb_kernel(q_ref, k_ref, v_ref, sm_ref, y_ref, kb_ref, vb_ref, rest_ref, acc_ref, *, t, s_len):
    qi = pl.program_id(1)

    @pl.when(qi == 0)
    def _():
        for r in range(s_len // t):
            rows = slice(r * t, (r + 1) * t)
            kk = k_ref[rows, :]
            vv = v_ref[rows, :]
            for h in range(HEADS):
                sl = slice(h * HEAD_DIM, (h + 1) * HEAD_DIM)
                kb_ref[h, rows, :] = kk[:, sl].astype(BF16)
                vb_ref[h, rows, :] = vv[:, sl].astype(BF16)

    qs = q_ref[...] * (QK_SCALE * LOG2E)
    qh = [qs[:, h * HEAD_DIM:(h + 1) * HEAD_DIM].astype(BF16) for h in range(HEADS)]
    strict = lax.broadcasted_iota(jnp.int32, (t, t), 1) < lax.broadcasted_iota(jnp.int32, (t, t), 0)
    from_here = sm_ref[...]
    rest_ref[...] = jnp.zeros(rest_ref.shape, F32)
    acc_ref[...] = jnp.zeros(acc_ref.shape, F32)

    def step(kb, masked):
        k0 = pl.multiple_of(kb * t, t)
        hs = range(HEADS)
        z = [_dot_t(qh[h], kb_ref[h, pl.ds(k0, t), :]) for h in hs]
        u = [jnp.maximum(z[h], 0.0) + jnp.log2(1.0 + jnp.exp2(-jnp.abs(z[h]))) for h in hs]
        if masked:
            u = [jnp.where(strict, v, 0.0) for v in u]
        incl = [jnp.dot(u[h].astype(BF16), from_here, preferred_element_type=F32) for h in hs]
        rest = [rest_ref[h] for h in hs]
        a = [jnp.exp2(z[h] - incl[h] - jnp.concatenate([rest[h]] * (t // 128), axis=1)) for h in hs]
        if masked:
            a = [jnp.where(strict, v, 0.0) for v in a]
        av = [jnp.dot(a[h].astype(BF16), vb_ref[h, pl.ds(k0, t), :], preferred_element_type=F32) for h in hs]
        for h in hs:
            acc_ref[h] += av[h]
            rest_ref[h] = rest[h] + incl[h][:, 0:1]

    step(qi, True)

    @pl.when(qi % 2 == 1)
    def _():
        step(qi - 1, False)

    def body(j, c):
        kb = 2 * (qi // 2 - 1 - j)
        step(kb + 1, False)
        step(kb, False)
        return c

    lax.fori_loop(0, qi // 2, body, 0)
    y_ref[...] = jnp.concatenate([acc_ref[h] for h in range(HEADS)], axis=1).astype(y_ref.dtype)


def _sb(proj, after):
    nb, s, _ = proj.shape
    t = ATT_T
    kern = functools.partial(_sb_kernel, t=t, s_len=s)
    return pl.pallas_call(
        kern,
        out_shape=jax.ShapeDtypeStruct((nb, s, GROUP), BF16),
        grid=(nb, s // t),
        in_specs=[pl.BlockSpec((None, t, GROUP), lambda b, i: (b, i, C_SQ // GROUP)),
                  pl.BlockSpec((None, s, GROUP), lambda b, i: (b, 0, C_SK // GROUP)),
                  pl.BlockSpec((None, s, GROUP), lambda b, i: (b, 0, C_SV // GROUP)),
                  pl.BlockSpec((t, t), lambda b, i: (0, 0))],
        out_specs=pl.BlockSpec((None, t, GROUP), lambda b, i: (b, i, 0)),
        scratch_shapes=[pltpu.VMEM((HEADS, s, HEAD_DIM), BF16),
                        pltpu.VMEM((HEADS, s, HEAD_DIM), BF16),
                        pltpu.VMEM((HEADS, t, 128), F32),
                        pltpu.VMEM((HEADS, t, HEAD_DIM), F32)],
        compiler_params=_cp(("arbitrary", "arbitrary")),
        name="stick_breaking",
    )(proj, proj, proj, after)


def _nsa_kernel(q_ref, kvs_ref, kvw_ref, ak_ref, av_ref, gc_ref, gb_ref, qg_ref, kg_ref, pos_ref,
                wk_ref, wv_ref, bd_ref, tc_ref, ovt_ref, tsb_ref, twb_ref, cb_ref, y_ref,
                ksel_ref, vsel_ref, kwin_ref, vwin_ref, kcmp_ref, vcmp_ref, m_ref, acc_ref,
                *, tq, s_len, n_sel):
    qi = pl.program_id(1)
    nc = s_len // CMP_STRIDE

    @pl.when(qi == 0)
    def _():
        zk = jnp.zeros((WINDOW, HEAD_DIM), BF16)
        zv = jnp.zeros((WINDOW, 2 * HEAD_DIM), BF16)
        ksel_ref[0:WINDOW, :] = zk
        vsel_ref[0:WINDOW, :] = zv
        kwin_ref[0:WINDOW, :] = zk
        vwin_ref[0:WINDOW, :] = zv
        ones = jnp.ones((512, HEAD_DIM), BF16)
        for r in range(s_len // 512):
            rows = slice(r * 512, (r + 1) * 512)
            dst = slice(WINDOW + r * 512, WINDOW + (r + 1) * 512)
            kvs = kvs_ref[rows, :]
            kvw = kvw_ref[rows, :]
            ksel_ref[dst, :] = _rms_lanes(kvs[:, :HEAD_DIM], kg_ref[1:2, :]).astype(BF16)
            kwin_ref[dst, :] = _rms_lanes(kvw[:, :HEAD_DIM], kg_ref[2:3, :]).astype(BF16)
            vsel_ref[dst, :] = jnp.concatenate([kvs[:, HEAD_DIM:].astype(BF16), ones], axis=1)
            vwin_ref[dst, :] = jnp.concatenate([kvw[:, HEAD_DIM:].astype(BF16), ones], axis=1)
        half = CMP_STRIDE * HEAD_DIM
        for a_ref, w_ref, dst_ref, norm in ((ak_ref, wk_ref, kcmp_ref, True), (av_ref, wv_ref, vcmp_ref, False)):
            a = a_ref[...]
            pos_b = _dot_hi(pos_ref[...], w_ref[...])[0:1, :]
            p1 = _dot_hi(a, w_ref[0:half, :])
            p2 = _dot_hi(a, w_ref[half:2 * half, :])
            c = p1 + pltpu.roll(p2, nc - 1, 0) + pos_b
            if norm:
                hi, lo = _split2(_rms_lanes(c, kg_ref[0:1, :]))
                kcmp_ref[0] = hi
                kcmp_ref[1] = lo
            else:
                dst_ref[...] = c.astype(BF16)

    t0 = qi * tq
    qn = _rms_heads(q_ref[...], bd_ref[...], qg_ref[...]) * (QK_SCALE * LOG2E)
    qf = [qn[:, h * HEAD_DIM:(h + 1) * HEAD_DIM] for h in range(HEADS)]
    qh = [q.astype(BF16) for q in qf]
    ql = [(qf[h] - qh[h].astype(F32)).astype(BF16) for h in range(HEADS)]

    hs = range(HEADS)
    k_hi, k_lo = kcmp_ref[0], kcmp_ref[1]
    dn = (((1,), (1,)), ((), ()))
    sc = [lax.dot_general(qh[h], k_hi, dn, preferred_element_type=F32)
          + lax.dot_general(ql[h], k_hi, dn, preferred_element_type=F32)
          + lax.dot_general(qh[h], k_lo, dn, preferred_element_type=F32) + tc_ref[h] for h in hs]
    e = [jnp.exp2(sc[h] - jnp.max(sc[h], axis=1, keepdims=True)) for h in hs]
    p = [jnp.where(tc_ref[h] > 0.5 * NEG, e[h] / jnp.sum(e[h], axis=1, keepdims=True), 0.0) for h in hs]
    o_cmp = [jnp.dot(p[h].astype(BF16), vcmp_ref[...], preferred_element_type=F32) for h in hs]
    ps_hi, ps_lo = _split2(p[0] + p[1] + p[2] + p[3])
    ovt = ovt_ref[...].astype(BF16)
    imp_t = (lax.dot_general(ovt, ps_hi, dn, preferred_element_type=F32)
             + lax.dot_general(ovt, ps_lo, dn, preferred_element_type=F32))

    nr = -(-n_sel // 8) * 8
    jj = lax.broadcasted_iota(jnp.int32, (nr, tq), 0)
    tt = t0 + lax.broadcasted_iota(jnp.int32, (nr, tq), 1)
    cur = tt // SEL_BLOCK
    forced = (jj == 0) | (jj == cur) | (jj == cur - 1)
    score = jnp.where(forced, -NEG, jnp.where(jj * SEL_BLOCK <= tt, imp_t[:nr], NEG))
    cnt = jnp.zeros((nr, tq), F32)
    for i in range(n_sel):
        ri = score[i:i + 1, :]
        cnt = cnt + jnp.where(jj > i, (ri >= score).astype(F32), (ri > score).astype(F32))
    sel_t = (cnt < float(min(N_SEL_TOP, n_sel))).astype(F32)
    if nr < 128:
        sel_t = jnp.concatenate([sel_t, jnp.zeros((128 - nr, tq), F32)], axis=0)
    selb = sel_t.T.astype(BF16)

    m_ref[...] = jnp.full(m_ref.shape, NEG, F32)
    acc_ref[...] = jnp.zeros(acc_ref.shape, F32)
    jrow = lax.broadcasted_iota(jnp.int32, (128, tq), 0)
    jcol = lax.broadcasted_iota(jnp.int32, (128, tq), 1) // SEL_BLOCK

    def sel_chunk(kp0, near):
        jb = kp0 // SEL_BLOCK - WINDOW // SEL_BLOCK
        expand = (jrow == jb + jcol).astype(BF16)
        picked = jnp.dot(selb, expand, preferred_element_type=F32) > 0.5
        kblk = ksel_ref[pl.ds(kp0, tq), :]
        vblk = vsel_ref[pl.ds(kp0, tq), :]
        hs = range(HEADS)
        bias = [cb_ref[h] if near is None else tsb_ref[h, :, near * tq:(near + 1) * tq] for h in hs]
        sck = [_dot_t(qh[h], kblk) + jnp.where(picked, bias[h], NEG) for h in hs]
        m_old = [m_ref[h] for h in hs]
        m_new = [jnp.maximum(m_old[h], jnp.max(sck[h], axis=1, keepdims=True)) for h in hs]
        pk = [jnp.exp2(sck[h] - jnp.concatenate([m_new[h]] * (tq // 128), axis=1)).astype(BF16) for h in hs]
        pv = [jnp.dot(pk[h], vblk, preferred_element_type=F32) for h in hs]
        for h in hs:
            acc_ref[h] = jnp.exp2(m_old[h] - m_new[h]) * acc_ref[h] + pv[h]
            m_ref[h] = m_new[h]

    def far(cp, carry):
        sel_chunk(pl.multiple_of(WINDOW + 2 * cp * tq, tq), None)
        sel_chunk(pl.multiple_of(WINDOW + (2 * cp + 1) * tq, tq), None)
        return carry

    n_far = jnp.maximum(qi - WINDOW // tq, 0)
    lax.fori_loop(0, n_far // 2, far, 0)

    @pl.when(n_far % 2 == 1)
    def _():
        sel_chunk(pl.multiple_of(WINDOW + (n_far - 1) * tq, tq), None)
    for r in range(NSA_NEAR // tq):
        @pl.when(qi + r >= WINDOW // tq)
        def _(r=r):
            sel_chunk(pl.multiple_of(t0 + r * tq, tq), r)

    kw = kwin_ref[pl.ds(pl.multiple_of(t0, tq), NSA_NEAR), :]
    vw = vwin_ref[pl.ds(pl.multiple_of(t0, tq), NSA_NEAR), :]
    in_seq = lax.broadcasted_iota(jnp.int32, (tq, NSA_NEAR), 1) >= WINDOW - t0
    g = _sigmoid(gc_ref[...] + gb_ref[...])
    sw = [jnp.where(in_seq, _dot_t(qh[h], kw) + twb_ref[h], NEG) for h in hs]
    pw = [jnp.exp2(sw[h] - jnp.max(sw[h], axis=1, keepdims=True)).astype(BF16) for h in hs]
    ows = [jnp.dot(pw[h], vw, preferred_element_type=F32) for h in hs]
    outs = []
    for h in hs:
        ow = ows[h]
        o_win = ow[:, :HEAD_DIM] / ow[:, HEAD_DIM:]
        a = acc_ref[h]
        o_sel = a[:, :HEAD_DIM] / a[:, HEAD_DIM:]
        c0 = G_NG + 3 * h
        outs.append(g[:, c0:c0 + 1] * o_cmp[h] + g[:, c0 + 1:c0 + 2] * o_sel + g[:, c0 + 2:c0 + 3] * o_win)
    y_ref[...] = jnp.concatenate(outs, axis=1).astype(y_ref.dtype)


def _nsa(proj, a_k, a_v, l, bias_c, q_g, k_g, pos, wk, wv, bd, tc, ovt, tsb, twb, cb):
    nb, s, _ = proj.shape
    tq = NSA_TQ
    nc = s // CMP_STRIDE
    kern = functools.partial(_nsa_kernel, tq=tq, s_len=s, n_sel=s // SEL_BLOCK)
    const2 = lambda b, i: (0, 0)
    const3 = lambda b, i: (0, 0, 0)
    lay3 = lambda b, i: (l, 0, 0)
    return pl.pallas_call(
        kern,
        out_shape=jax.ShapeDtypeStruct((nb, s, GROUP), BF16),
        grid=(nb, s // tq),
        in_specs=[pl.BlockSpec((None, tq, GROUP), lambda b, i: (b, i, C_NQ // GROUP)),
                  pl.BlockSpec((None, s, 128), lambda b, i: (b, 0, C_NKV // 128 + 1)),
                  pl.BlockSpec((None, s, 128), lambda b, i: (b, 0, C_NKV // 128 + 2)),
                  pl.BlockSpec((None, nc, CMP_STRIDE * HEAD_DIM), lambda b, i: (b, 0, 0)),
                  pl.BlockSpec((None, nc, CMP_STRIDE * HEAD_DIM), lambda b, i: (b, 0, 0)),
                  pl.BlockSpec((None, tq, 128), lambda b, i: (b, i, C_GATES // 128)),
                  pl.BlockSpec((None, 1, 128), lay3),
                  pl.BlockSpec((None, 1, GROUP), lay3),
                  pl.BlockSpec((None, 8, HEAD_DIM), lay3),
                  pl.BlockSpec((None, 8, CMP_BLOCK * HEAD_DIM), lay3),
                  pl.BlockSpec((None, CMP_BLOCK * HEAD_DIM, HEAD_DIM), lay3),
                  pl.BlockSpec((None, CMP_BLOCK * HEAD_DIM, HEAD_DIM), lay3),
                  pl.BlockSpec((GROUP, GROUP), const2),
                  pl.BlockSpec((HEADS, tq, nc), lambda b, i: (0, i, 0)),
                  pl.BlockSpec((128, nc), const2),
                  pl.BlockSpec((HEADS, tq, NSA_NEAR), const3),
                  pl.BlockSpec((HEADS, tq, NSA_NEAR), const3),
                  pl.BlockSpec(memory_space=pltpu.SMEM)],
        out_specs=pl.BlockSpec((None, tq, GROUP), lambda b, i: (b, i, 0)),
        scratch_shapes=[pltpu.VMEM((s + WINDOW, HEAD_DIM), BF16),
                        pltpu.VMEM((s + WINDOW, 2 * HEAD_DIM), BF16),
                        pltpu.VMEM((s + WINDOW, HEAD_DIM), BF16),
                        pltpu.VMEM((s + WINDOW, 2 * HEAD_DIM), BF16),
                        pltpu.VMEM((2, nc, HEAD_DIM), BF16),
                        pltpu.VMEM((nc, HEAD_DIM), BF16),
                        pltpu.VMEM((HEADS, tq, 128), F32),
                        pltpu.VMEM((HEADS, tq, 2 * HEAD_DIM), F32)],
        compiler_params=_cp(("arbitrary", "arbitrary")),
        name="nsa",
    )(proj, proj, proj, a_k, a_v, proj, bias_c, q_g, k_g, pos, wk, wv, bd, tc, ovt, tsb, twb, cb)


def _outproj_kernel(x_ref, ya_ref, yb_ref, yc_ref, yd_ref, w_ref, mod_ref, g_ref, xo_ref, h_ref):
    acc = jnp.dot(ya_ref[...], w_ref[0:GROUP, :], preferred_element_type=F32)
    acc += jnp.dot(yb_ref[...], w_ref[GROUP:2 * GROUP, :], preferred_element_type=F32)
    acc += jnp.dot(yc_ref[...], w_ref[2 * GROUP:3 * GROUP, :], preferred_element_type=F32)
    acc += jnp.dot(yd_ref[...], w_ref[3 * GROUP:4 * GROUP, :], preferred_element_type=F32)
    xn = x_ref[...] + mod_ref[2:3, :] * acc
    xo_ref[...] = xn
    h = _rms_lanes(xn, g_ref[...]) * (1.0 + mod_ref[4:5, :]) + mod_ref[3:4, :]
    h_ref[...] = h.astype(h_ref.dtype)


def _outproj(x, ys, mod, l, w_out, norm_g, h_dtype):
    nb, s, d = x.shape
    tm = 512
    yspec = pl.BlockSpec((None, tm, GROUP), lambda b, i: (b, i, 0))
    xspec = pl.BlockSpec((None, tm, d), lambda b, i: (b, i, 0))
    return pl.pallas_call(
        _outproj_kernel,
        out_shape=(jax.ShapeDtypeStruct((nb, s, d), F32), jax.ShapeDtypeStruct((nb, s, d), h_dtype)),
        grid=(nb, s // tm),
        in_specs=[xspec, yspec, yspec, yspec, yspec,
                  pl.BlockSpec((None, d, d), lambda b, i: (l, 0, 0)),
                  pl.BlockSpec((None, None, 6, d), lambda b, i: (l, b, 0, 0)),
                  pl.BlockSpec((None, 1, d), lambda b, i: (l, 0, 0))],
        out_specs=(xspec, xspec),
        compiler_params=_cp(("arbitrary", "arbitrary")),
        name="outproj",
    )(x, *ys, w_out, mod, norm_g)


def _ffn_kernel(h_ref, x_ref, mod_ref, wg_ref, wu_ref, wd_ref, o_ref, acc_ref, *, nf):
    f = pl.program_id(2)

    @pl.when(f == 0)
    def _():
        acc_ref[...] = jnp.zeros_like(acc_ref)

    h = h_ref[...]
    a = jnp.dot(h, wg_ref[...], preferred_element_type=F32)
    u = jnp.dot(h, wu_ref[...], preferred_element_type=F32)
    act = (a * _sigmoid(a) * u).astype(BF16)
    acc_ref[...] += jnp.dot(act, wd_ref[...], preferred_element_type=F32)

    @pl.when(f == nf - 1)
    def _():
        o_ref[...] = x_ref[...] + mod_ref[5:6, :] * acc_ref[...]


def _ffn(h, x, mod, l, li, wg, wu, wd):
    nb, s, d = x.shape
    dff = wg.shape[-1]
    tm, tf = FFN_TM, FFN_TF
    nf = dff // tf
    xspec = pl.BlockSpec((None, tm, d), lambda b, i, f: (b, i, 0))
    return pl.pallas_call(
        functools.partial(_ffn_kernel, nf=nf),
        out_shape=jax.ShapeDtypeStruct((nb, s, d), F32),
        grid=(nb, s // tm, nf),
        in_specs=[xspec, xspec,
                  pl.BlockSpec((None, None, 6, d), lambda b, i, f: (l, b, 0, 0)),
                  pl.BlockSpec((None, d, tf), lambda b, i, f: (li, 0, f)),
                  pl.BlockSpec((None, d, tf), lambda b, i, f: (li, 0, f)),
                  pl.BlockSpec((None, tf, d), lambda b, i, f: (li, f, 0))],
        out_specs=xspec,
        scratch_shapes=[pltpu.VMEM((tm, d), F32)],
        compiler_params=_cp(("arbitrary", "arbitrary", "arbitrary")),
        name="ffn_dense",
    )(h, x, mod, wg, wu, wd)


def _router_kernel(h_ref, w_ref, b_ref, o_ref):
    logits = _dot_hi(h_ref[...], w_ref[...]) + b_ref[...]
    lane = lax.broadcasted_iota(jnp.int32, logits.shape, 1).astype(F32)
    lg = jnp.where(lane < N_EXPERTS, logits, -3e38)
    m1 = jnp.max(lg, axis=1, keepdims=True)
    i1 = jnp.min(jnp.where(lg == m1, lane, 128.0), axis=1, keepdims=True)
    lg2 = jnp.where(lane == i1, -3e38, lg)
    m2 = jnp.max(lg2, axis=1, keepdims=True)
    i2 = jnp.min(jnp.where(lg2 == m2, lane, 128.0), axis=1, keepdims=True)
    e2 = jnp.exp(m2 - m1)
    w1 = 1.0 / (1.0 + e2)
    w2 = e2 / (1.0 + e2)
    o_ref[...] = jnp.where(lane == 0, i1, jnp.where(lane == 1, i2, jnp.where(lane == 2, w1, jnp.where(lane == 3, w2, 0.0))))


def _router(hf, rw, rb):
    t, d = hf.shape
    tm = 1024
    return pl.pallas_call(
        _router_kernel,
        out_shape=jax.ShapeDtypeStruct((t, 128), F32),
        grid=(t // tm,),
        in_specs=[pl.BlockSpec((tm, d), lambda i: (i, 0)),
                  pl.BlockSpec((d, 128), lambda i: (0, 0)),
                  pl.BlockSpec((1, 128), lambda i: (0, 0))],
        out_specs=pl.BlockSpec((tm, 128), lambda i: (i, 0)),
        compiler_params=_cp(("arbitrary",)),
        name="moe_router",
    )(hf, rw, rb)


def _gather_copy(src_hbm, row, dst_buf, slot, r, sem):
    return pltpu.make_async_copy(src_hbm.at[pl.ds(row, 1)], dst_buf.at[slot, pl.ds(r, 1)], sem.at[slot])


def _experts_kernel(be_ref, tok_ref, nv_ref, h_hbm, wg_ref, wu_ref, wd_ref, y_ref,
                    xbuf, xb_ref, acc_ref, sem, *, bm, nf):
    del be_ref
    i = pl.program_id(0)
    f = pl.program_id(1)
    nvalid = nv_ref[0]
    slot = i % 2
    chunk = bm // nf

    def for_rows(fn):
        def body(r, c):
            fn(r)
            return c
        lax.fori_loop(0, bm, body, 0, unroll=8)

    @pl.when((f == 0) & (i == 0))
    def _():
        for_rows(lambda r: _gather_copy(h_hbm, tok_ref[r], xbuf, 0, r, sem).start())

    @pl.when((f == 0) & (i < nvalid))
    def _():
        for_rows(lambda r: _gather_copy(h_hbm, 0, xbuf, slot, r, sem).wait())
        xb_ref[...] = xbuf[slot].astype(BF16)
        acc_ref[...] = jnp.zeros_like(acc_ref)

    def compute(prefetch):
        third = -(-chunk // 3)

        def issue_rows(lo, hi):
            if prefetch:
                for j in range(lo, min(hi, chunk)):
                    r = f * chunk + j
                    _gather_copy(h_hbm, tok_ref[(i + 1) * bm + r], xbuf, 1 - slot, r, sem).start()

        x = xb_ref[...]
        issue_rows(0, third)
        a = jnp.dot(x, wg_ref[...], preferred_element_type=F32)
        issue_rows(third, 2 * third)
        u = jnp.dot(x, wu_ref[...], preferred_element_type=F32)
        issue_rows(2 * third, 3 * third)
        act = (a * _sigmoid(a) * u).astype(BF16)
        acc_ref[...] += jnp.dot(act, wd_ref[...], preferred_element_type=F32)

    @pl.when(i + 1 < nvalid)
    def _():
        compute(True)

    @pl.when(i + 1 == nvalid)
    def _():
        compute(False)

    @pl.when(f == nf - 1)
    def _():
        @pl.when(i < nvalid)
        def _():
            y_ref[...] = acc_ref[...]

        @pl.when(i >= nvalid)
        def _():
            y_ref[...] = jnp.zeros_like(y_ref)


def _experts(hf, blk_expert, row_tok, nvalid, wg, wu, wd, li):
    t, d = hf.shape
    bm, tf = MOE_BM, MOE_TF
    n_rows = row_tok.shape[0]
    n_blocks = n_rows // bm
    dff = wg.shape[-1]
    nf = dff // tf

    def fidx(i, f, nv):
        return jnp.where(i < nv[0], f, nf - 1)

    grid_spec = pltpu.PrefetchScalarGridSpec(
        num_scalar_prefetch=3,
        grid=(n_blocks, nf),
        in_specs=[pl.BlockSpec(memory_space=pl.ANY),
                  pl.BlockSpec((None, None, d, tf), lambda i, f, be, tok, nv: (li, be[i], 0, fidx(i, f, nv))),
                  pl.BlockSpec((None, None, d, tf), lambda i, f, be, tok, nv: (li, be[i], 0, fidx(i, f, nv))),
                  pl.BlockSpec((None, None, tf, d), lambda i, f, be, tok, nv: (li, be[i], fidx(i, f, nv), 0))],
        out_specs=pl.BlockSpec((bm, d), lambda i, f, be, tok, nv: (i, 0)),
        scratch_shapes=[pltpu.VMEM((2, bm, d), F32),
                        pltpu.VMEM((bm, d), BF16),
                        pltpu.VMEM((bm, d), F32),
                        pltpu.SemaphoreType.DMA((2,))],
    )
    return pl.pallas_call(
        functools.partial(_experts_kernel, bm=bm, nf=nf),
        out_shape=jax.ShapeDtypeStruct((n_rows, d), F32),
        grid_spec=grid_spec,
        compiler_params=_cp(("arbitrary", "arbitrary")),
        name="moe_experts",
    )(blk_expert, row_tok, nvalid, hf, wg, wu, wd)


def _combine_kernel(pos_ref, y_hbm, x_ref, mod_ref, rt_ref, o_ref, ybuf, sem, *, tm, s_len):
    b = pl.program_id(0)
    i = pl.program_id(1)
    base = (b * s_len + i * tm) * 2

    def issue(r, c):
        _gather_copy(y_hbm, pos_ref[base + 2 * r], ybuf, 0, r, sem).start()
        _gather_copy(y_hbm, pos_ref[base + 2 * r + 1], ybuf, 1, r, sem).start()
        return c

    def wait(r, c):
        _gather_copy(y_hbm, 0, ybuf, 0, r, sem).wait()
        _gather_copy(y_hbm, 0, ybuf, 1, r, sem).wait()
        return c

    lax.fori_loop(0, tm, issue, 0, unroll=8)
    lax.fori_loop(0, tm, wait, 0, unroll=8)
    rt = rt_ref[...]
    o_ref[...] = x_ref[...] + mod_ref[5:6, :] * (rt[:, 2:3] * ybuf[0] + rt[:, 3:4] * ybuf[1])


def _combine(y, pos, x, mod, route, l):
    nb, s, d = x.shape
    tm = 256
    grid_spec = pltpu.PrefetchScalarGridSpec(
        num_scalar_prefetch=1,
        grid=(nb, s // tm),
        in_specs=[pl.BlockSpec(memory_space=pl.ANY),
                  pl.BlockSpec((None, tm, d), lambda b, i, p: (b, i, 0)),
                  pl.BlockSpec((None, None, 6, d), lambda b, i, p: (l, b, 0, 0)),
                  pl.BlockSpec((None, tm, 128), lambda b, i, p: (b, i, 0))],
        out_specs=pl.BlockSpec((None, tm, d), lambda b, i, p: (b, i, 0)),
        scratch_shapes=[pltpu.VMEM((2, tm, d), F32), pltpu.SemaphoreType.DMA((2,))],
    )
    return pl.pallas_call(
        functools.partial(_combine_kernel, tm=tm, s_len=s),
        out_shape=jax.ShapeDtypeStruct((nb, s, d), F32),
        grid_spec=grid_spec,
        compiler_params=_cp(("arbitrary", "arbitrary")),
        name="moe_combine",
    )(pos, y, x, mod, route.reshape(nb, s, 128))


def _moe(hf32, x, mod, l, li, router_w, router_b, wg, wu, wd):
    nb, s, d = x.shape
    t = nb * s
    bm = MOE_BM
    hf = hf32.reshape(t, d)
    rw = jnp.zeros((d, 128), F32).at[:, :N_EXPERTS].set(router_w[li])
    rb = jnp.zeros((1, 128), F32).at[0, :N_EXPERTS].set(router_b[li])
    route = _router(hf, rw, rb)
    e_flat = route[:, 0:2].astype(jnp.int32).reshape(-1)
    n_assign = 2 * t
    onehot = (e_flat[:, None] == jnp.arange(N_EXPERTS, dtype=jnp.int32)[None, :]).astype(jnp.int32)
    csum = jnp.cumsum(onehot, axis=0)
    rank = jnp.sum(onehot * csum, axis=1) - 1
    counts = csum[-1]
    padded = (counts + bm - 1) // bm * bm
    cum_padded = jnp.cumsum(padded)
    pstart = cum_padded - padded
    dest = pstart[e_flat] + rank
    n_blocks = n_assign // bm + N_EXPERTS
    n_rows = n_blocks * bm
    tok_flat = jnp.arange(n_assign, dtype=jnp.int32) // 2
    row_tok = jnp.zeros((n_rows,), jnp.int32).at[dest].set(tok_flat)
    blk_start = jnp.arange(n_blocks, dtype=jnp.int32) * bm
    blk_expert = jnp.minimum(jnp.searchsorted(cum_padded, blk_start, side='right'), N_EXPERTS - 1).astype(jnp.int32)
    nvalid = (cum_padded[-1] // bm).astype(jnp.int32).reshape(1)
    y = _experts(hf, blk_expert, row_tok, nvalid, wg, wu, wd, li)
    return _combine(y, dest.astype(jnp.int32), x, mod, route, l)


def _t5_bucket(dist):
    n = np.maximum(dist, 0)
    max_exact = NUM_BUCKETS // 2
    large = max_exact + (np.log(np.maximum(n, 1).astype(np.float32) / max_exact)
                         / math.log(MAX_DISTANCE / max_exact) * (NUM_BUCKETS - max_exact)).astype(np.int32)
    return np.where(n < max_exact, n, np.minimum(large, NUM_BUCKETS - 1)).astype(np.int32)


def _bias_by_bucket(rb, bucket):
    ids = jnp.asarray(bucket.astype(np.int8))[None]
    out = jnp.zeros((rb.shape[1],) + bucket.shape, F32)
    for k in range(NUM_BUCKETS):
        out = jnp.where(ids == k, rb[k][:, None, None], out)
    return out


def _nsa_tables(rel_bias, s):
    tq = NSA_TQ
    nc = s // CMP_STRIDE
    n_cmp = (s - CMP_BLOCK) // CMP_STRIDE + 1
    rb = rel_bias.astype(F32) * LOG2E
    t = np.arange(s)[:, None]
    cmp_end = np.arange(nc)[None, :] * CMP_STRIDE + CMP_BLOCK - 1
    ok = (cmp_end <= t) & (np.arange(nc)[None, :] < n_cmp)
    tc = jnp.where(jnp.asarray(ok)[None], _bias_by_bucket(rb, _t5_bucket(t - cmp_end)), NEG)
    dist = np.arange(tq)[:, None] - (np.arange(NSA_NEAR)[None, :] - WINDOW)
    tb = _bias_by_bucket(rb, _t5_bucket(dist))
    tsb = jnp.where(jnp.asarray(dist >= 0)[None], tb, NEG)
    twb = jnp.where(jnp.asarray((dist >= 0) & (dist < WINDOW))[None], tb, NEG)
    far_bucket = int(_t5_bucket(np.array([WINDOW]))[0])
    cb = rb[far_bucket]
    n_sel = s // SEL_BLOCK
    cs = np.arange(nc)[None, :] * CMP_STRIDE
    ss = np.arange(128)[:, None] * SEL_BLOCK
    ov = np.clip(np.minimum(cs + CMP_BLOCK, ss + SEL_BLOCK) - np.maximum(cs, ss), 0, None).astype(np.float32) / CMP_BLOCK
    ov = ov * (np.arange(128)[:, None] < n_sel) * (np.arange(nc)[None, :] < n_cmp)
    return tc, jnp.asarray(ov, F32), tsb, twb, cb


def _prep_layer_params(w_in, mlstm_gate_b, fox_f_b, nsa_gate_b):
    depth, d, _ = w_in.shape
    small = [w_in[:, :, 1024:1032], w_in[:, :, 1800:1804], w_in[:, :, 3212:3224]]
    w_p = jnp.concatenate([w_in[:, :, 0:1024], w_in[:, :, 1032:1800], w_in[:, :, 1804:2572], w_in[:, :, 2572:3212]]
                          + small + [jnp.zeros((depth, d, 128 - 24), w_in.dtype)], axis=-1).astype(BF16)
    wgt = jnp.concatenate(small + [jnp.zeros((depth, d, GT_ROWS - 24), w_in.dtype)], axis=-1)
    wgt = jnp.transpose(wgt, (0, 2, 1)).astype(BF16)
    gate_b = jnp.concatenate([mlstm_gate_b, fox_f_b, nsa_gate_b], axis=-1).astype(F32)
    bias_c = jnp.zeros((depth, 1, 128), F32).at[:, 0, :24].set(gate_b)
    bias_r = jnp.zeros((depth, GT_ROWS, 1), F32).at[:, :24, 0].set(gate_b)
    return w_p, wgt, bias_c, bias_r


def kernel(x, c, rel_bias, ada_w, ada_b, norm1_g, norm2_g, w_in, w_out, mlstm_conv_w, mlstm_gate_b, mlstm_out_g,
           fox_f_b, fox_q_g, fox_k_g, nsa_q_g, nsa_k_g, nsa_cmp_pos, nsa_cmp_wk, nsa_cmp_wv, nsa_gate_b,
           ffn_wg, ffn_wu, ffn_wd, moe_router_w, moe_router_b, moe_wg, moe_wu, moe_wd):
    nb, s, d = x.shape
    depth = w_in.shape[0]
    assert d == D_MODEL and s % 512 == 0 and s // SEL_BLOCK <= 128

    w_p, wgt, bias_c, bias_r = _prep_layer_params(w_in, mlstm_gate_b, fox_f_b, nsa_gate_b)
    w_out_b = w_out.astype(BF16)
    n1g = norm1_g.reshape(depth, 1, d)
    n2g = norm2_g.reshape(depth, 1, d)
    m_out_g = mlstm_out_g.reshape(depth, 1, GROUP)
    fq_g = jnp.tile(fox_q_g, (1, HEADS)).reshape(depth, 1, GROUP)
    fk_g = jnp.tile(fox_k_g, (1, HEADS)).reshape(depth, 1, GROUP)
    nq_g = jnp.tile(nsa_q_g, (1, HEADS)).reshape(depth, 1, GROUP)
    nk_g = jnp.zeros((depth, 8, HEAD_DIM), F32).at[:, :3].set(nsa_k_g)
    pos8 = jnp.zeros((depth, 8, CMP_BLOCK * HEAD_DIM), F32).at[:, 0].set(nsa_cmp_pos.reshape(depth, -1))
    ffn_wg_b, ffn_wu_b, ffn_wd_b = ffn_wg.astype(BF16), ffn_wu.astype(BF16), ffn_wd.astype(BF16)
    moe_wg_b, moe_wu_b, moe_wd_b = moe_wg.astype(BF16), moe_wu.astype(BF16), moe_wd.astype(BF16)
    hid = np.arange(GROUP) // HEAD_DIM
    bd = jnp.asarray((hid[:, None] == hid[None, :]).astype(np.float32) / HEAD_DIM)
    tri_l = jnp.asarray(np.tril(np.ones((MLSTM_L, MLSTM_L), np.float32)))
    tri_t = jnp.asarray(np.tril(np.ones((ATT_T, ATT_T), np.float32)))
    from_here = jnp.asarray(np.tril(np.ones((ATT_T, ATT_T), np.float32)), BF16)
    tc, ovt, tsb, twb, cb = _nsa_tables(rel_bias, s)

    mod = _adaln(c, ada_w, ada_b)
    for l in range(depth):
        proj, gt = _inproj(x, mod, l, n1g, w_p, wgt)
        y_a = _mlstm(proj, gt, l, mlstm_conv_w, bias_c, bias_r, m_out_g, tri_l)
        y_b = _fox(proj, gt, l, bias_r, fq_g, fk_g, bd, tri_t)
        y_c = _sb(proj, from_here)
        a_k = proj[:, :, C_NKV:C_NKV + HEAD_DIM].reshape(nb, s // CMP_STRIDE, CMP_STRIDE * HEAD_DIM)
        a_v = proj[:, :, C_NKV + HEAD_DIM:C_NKV + 2 * HEAD_DIM].reshape(nb, s // CMP_STRIDE, CMP_STRIDE * HEAD_DIM)
        y_d = _nsa(proj, a_k, a_v, l, bias_c, nq_g, nk_g, pos8, nsa_cmp_wk, nsa_cmp_wv, bd, tc, ovt, tsb, twb, cb)
        if l % 2 == 0:
            x, h2 = _outproj(x, (y_a, y_b, y_c, y_d), mod, l, w_out_b, n2g, BF16)
            x = _ffn(h2, x, mod, l, l // 2, ffn_wg_b, ffn_wu_b, ffn_wd_b)
        else:
            x, h2 = _outproj(x, (y_a, y_b, y_c, y_d), mod, l, w_out_b, n2g, F32)
            x = _moe(h2, x, mod, l, l // 2, moe_router_w, moe_router_b, moe_wg_b, moe_wu_b, moe_wd_b)
    return x
```

```python
import functools
import math

import numpy as np
import jax
import jax.numpy as jnp
from jax import lax
from jax.experimental import pallas as pl
from jax.experimental.pallas import tpu as pltpu

F32 = jnp.float32
BF16 = jnp.bfloat16
HI = lax.Precision.HIGHEST

D_MODEL = 1024
HEADS = 4
HEAD_DIM = 64
GROUP = HEADS * HEAD_DIM
NORM_EPS = 1e-6
NEG = -1e30
QK_SCALE = HEAD_DIM ** -0.5
LOG2E = 1.4426950408889634
CMP_BLOCK = 32
CMP_STRIDE = 16
SEL_BLOCK = 64
N_SEL_TOP = 16
WINDOW = 512
NUM_BUCKETS = 32
MAX_DISTANCE = 128
N_EXPERTS = 8
VMEM_LIMIT = 56 * 1024 * 1024

PW = 3328
C_MQK, C_MV, C_MO = 0, 512, 768
C_FQ, C_FK, C_FV = 1024, 1280, 1536
C_SQ, C_SK, C_SV = 1792, 2048, 2304
C_NQ, C_NKV, C_GATES = 2560, 2816, 3200
G_MI, G_MF, G_FF, G_NG = 0, 4, 8, 12
GT_ROWS = 32

MLSTM_TS = 512
MLSTM_L = 128
ATT_T = 256
NSA_TQ = 256
NSA_NEAR = WINDOW + NSA_TQ
MOE_BM = 512
MOE_TF = 896
FFN_TM = 512
FFN_TF = 1408


def _cp(sem, vmem=VMEM_LIMIT):
    return pltpu.CompilerParams(dimension_semantics=sem, vmem_limit_bytes=vmem)


def _dot(a, b):
    return jnp.dot(a.astype(BF16), b.astype(BF16), preferred_element_type=F32)


def _dot_t(a, b):
    return lax.dot_general(a.astype(BF16), b.astype(BF16), (((1,), (1,)), ((), ())), preferred_element_type=F32)


def _dot_hi(a, b):
    return jnp.dot(a, b, precision=HI, preferred_element_type=F32)


def _dot_t_hi(a, b):
    return lax.dot_general(a, b, (((1,), (1,)), ((), ())), precision=HI, preferred_element_type=F32)


def _sigmoid(x):
    return 1.0 / (1.0 + jnp.exp(-x))


def _log_sigmoid(x):
    return jnp.minimum(x, 0.0) - jnp.log1p(jnp.exp(-jnp.abs(x)))


def _rms_lanes(x, g):
    return x * lax.rsqrt(jnp.mean(x * x, axis=-1, keepdims=True) + NORM_EPS) * g


def _split2(a):
    hi = a.astype(BF16)
    return hi, (a - hi.astype(F32)).astype(BF16)


def _dot_2x(a, b):
    hi, lo = _split2(a)
    bb = b.astype(BF16)
    return jnp.dot(hi, bb, preferred_element_type=F32) + jnp.dot(lo, bb, preferred_element_type=F32)


def _dot_t_2x(a, b):
    hi, lo = _split2(a)
    bb = b.astype(BF16)
    dn = (((1,), (1,)), ((), ()))
    return (lax.dot_general(hi, bb, dn, preferred_element_type=F32)
            + lax.dot_general(lo, bb, dn, preferred_element_type=F32))


def _rms_heads(x, bd, g):
    return x * lax.rsqrt(_dot_2x(x * x, bd) + NORM_EPS) * g


def _adaln_kernel(c_ref, w_ref, b_ref, o_ref):
    c = c_ref[...]
    o_ref[...] = _dot_hi(c * _sigmoid(c), w_ref[...]) + b_ref[...]


def _adaln(c, ada_w, ada_b):
    depth, d, six_d = ada_w.shape
    nb = c.shape[0]
    out = pl.pallas_call(
        _adaln_kernel,
        out_shape=jax.ShapeDtypeStruct((depth, nb, six_d), F32),
        grid=(depth, six_d // d),
        in_specs=[pl.BlockSpec((nb, d), lambda l, j: (0, 0)),
                  pl.BlockSpec((None, d, d), lambda l, j: (l, 0, j)),
                  pl.BlockSpec((None, 1, d), lambda l, j: (l, 0, j))],
        out_specs=pl.BlockSpec((None, nb, d), lambda l, j: (l, 0, j)),
        compiler_params=_cp(("arbitrary", "arbitrary")),
        name="adaln",
    )(c, ada_w, ada_b.reshape(depth, 1, six_d))
    return out.reshape(depth, nb, 6, d)


def _inproj_kernel(x_ref, mod_ref, g_ref, w_ref, wgt_ref, o_ref, gt_ref):
    h = _rms_lanes(x_ref[...], g_ref[...]) * (1.0 + mod_ref[1:2, :]) + mod_ref[0:1, :]
    hb = h.astype(BF16)
    o_ref[...] = jnp.dot(hb, w_ref[...], preferred_element_type=F32)
    gt_ref[...] = lax.dot_general(wgt_ref[...], hb, (((1,), (1,)), ((), ())), preferred_element_type=F32)


def _inproj(x, mod, l, norm_g, w_p, wgt):
    nb, s, d = x.shape
    tm = 256
    return pl.pallas_call(
        _inproj_kernel,
        out_shape=(jax.ShapeDtypeStruct((nb, s, PW), F32), jax.ShapeDtypeStruct((nb, GT_ROWS, s), F32)),
        grid=(nb, s // tm),
        in_specs=[pl.BlockSpec((None, tm, d), lambda b, i: (b, i, 0)),
                  pl.BlockSpec((None, None, 6, d), lambda b, i: (l, b, 0, 0)),
                  pl.BlockSpec((None, 1, d), lambda b, i: (l, 0, 0)),
                  pl.BlockSpec((None, d, PW), lambda b, i: (l, 0, 0)),
                  pl.BlockSpec((None, GT_ROWS, d), lambda b, i: (l, 0, 0))],
        out_specs=(pl.BlockSpec((None, tm, PW), lambda b, i: (b, i, 0)),
                   pl.BlockSpec((None, GT_ROWS, tm), lambda b, i: (b, 0, i))),
        compiler_params=_cp(("arbitrary", "arbitrary")),
        name="inproj",
    )(x, mod, norm_g, w_p, wgt)


def _mlstm_kernel(qk_ref, v_ref, op_ref, gc_ref, gr_ref, cw_ref, bc_ref, br_ref, og_ref, tri_ref,
                  hm_ref, segs_ref, segq_ref, exp_ref, bd_ref, y_ref,
                  prev_ref, qs_ref, ks_ref, c_ref, n_ref, m_ref, *, ts, cl):
    @pl.when(pl.program_id(1) == 0)
    def _():
        prev_ref[...] = jnp.zeros_like(prev_ref)
        c_ref[...] = jnp.zeros_like(c_ref)
        n_ref[...] = jnp.zeros_like(n_ref)
        m_ref[...] = jnp.zeros_like(m_ref)

    x = qk_ref[...]
    xc = jnp.concatenate([prev_ref[...], x], axis=0)
    cw = cw_ref[...]
    y = (cw[0:1] * xc[8:8 + ts] + cw[1:2] * xc[7:7 + ts] + cw[2:3] * xc[6:6 + ts] + cw[3:4] * xc[5:5 + ts])
    prev_ref[...] = x[ts - 8:ts]
    y = y * _sigmoid(y)
    qs_ref[...] = y[:, :GROUP]
    ks_ref[...] = y[:, GROUP:] * QK_SCALE

    tri = tri_ref[...]
    lower = lax.broadcasted_iota(jnp.int32, (cl, cl), 0) >= lax.broadcasted_iota(jnp.int32, (cl, cl), 1)

    def chunk(ci, carry):
        r0 = pl.multiple_of(ci * cl, cl)
        gc = gc_ref[pl.ds(r0, cl), :] + bc_ref[...]
        gr = gr_ref[0:8, pl.ds(r0, cl)] + br_ref[0:8, :]
        lf_hi, lf_lo = _split2(_log_sigmoid(gc))
        trib = tri.astype(BF16)
        b_c = (jnp.dot(trib, lf_hi, preferred_element_type=F32)
               + jnp.dot(trib, lf_lo, preferred_element_type=F32))
        b_r = _dot_t_2x(_log_sigmoid(gr), tri)
        q = qs_ref[pl.ds(r0, cl), :]
        k = ks_ref[pl.ds(r0, cl), :]
        v = v_ref[pl.ds(r0, cl), :]
        kb = k.astype(BF16)
        hs = range(HEADS)
        ig_c = pltpu.roll(gc, G_MF - G_MI, 1)
        ig_r = pltpu.roll(gr, G_MF - G_MI, 0)
        u_c = ig_c - b_c
        u_r = ig_r - b_r
        g_row = b_c[cl - 1:cl, :]
        w_c = g_row + u_c
        m_loc = jnp.max(w_c, axis=0, keepdims=True)
        ew_c = jnp.exp(w_c - m_loc)
        m_prev = m_ref[...]
        pm = u_c
        row = lax.broadcasted_iota(jnp.int32, (cl, 128), 0)
        sh = 1
        while sh < cl:
            pm = jnp.where(row >= sh, jnp.maximum(pm, pltpu.roll(pm, sh, 0)), pm)
            sh *= 2
        mm_c = jnp.maximum(pm, m_prev)
        iw_c = jnp.exp(m_prev - mm_c)
        em_c = jnp.exp(-(b_c + mm_c))
        sb = []
        for h in hs:
            qm = jnp.where(hm_ref[h:h + 1, :] > 0.5, q, 0.0).astype(BF16)
            arg = jnp.where(lower, u_r[G_MF + h:G_MF + h + 1, :] - mm_c[:, G_MF + h:G_MF + h + 1], NEG)
            sb.append((_dot_t(qm, kb) * jnp.exp(arg)).astype(BF16))
        den_s = jnp.dot(jnp.concatenate(sb, axis=1), segs_ref[...], preferred_element_type=F32)
        pv = None
        for h in hs:
            vm = jnp.where(hm_ref[h:h + 1, :] > 0.5, v, 0.0).astype(BF16)
            t_h = jnp.dot(sb[h], vm, preferred_element_type=F32)
            pv = t_h if pv is None else pv + t_h
        c_prev = c_ref[...]
        n_prev = n_ref[...]
        inter = _dot_t(q, c_prev)
        qn_c = _dot_2x(q * n_prev, segq_ref[...])
        den_c = den_s + iw_c * qn_c
        lane = lax.broadcasted_iota(jnp.int32, (cl, 128), 1)
        r_c = jnp.where((lane >= G_MF) & (lane < G_MF + HEADS), 1.0 / jnp.maximum(jnp.abs(den_c), em_c), 0.0)
        expand = exp_ref[...]
        wide = _dot_2x(jnp.concatenate([ew_c, iw_c, r_c], axis=0), expand)
        ew_w, iw_w, r_w = wide[0:cl], wide[cl:2 * cl], wide[2 * cl:3 * cl]
        hh = (pv + iw_w * inter) * r_w
        hn = _rms_heads(hh, bd_ref[...], og_ref[...])
        y_ref[pl.ds(r0, cl), :] = (hn * _sigmoid(op_ref[pl.ds(r0, cl), :])).astype(y_ref.dtype)
        m_new = jnp.maximum(g_row + m_prev, m_loc)
        ab = jnp.concatenate([jnp.exp(g_row + m_prev - m_new), jnp.exp(m_loc - m_new),
                              jnp.zeros((6, 128), F32)], axis=0)
        ab_w = _dot_2x(ab, expand)
        c_full = lax.dot_general((v * ew_w).astype(BF16), kb, (((0,), (0,)), ((), ())), preferred_element_type=F32)
        c_ref[...] = ab_w[0:1] * c_prev + ab_w[1:2] * jnp.where(bd_ref[...] > 0.0, c_full, 0.0)
        n_ref[...] = ab_w[0:1] * n_prev + ab_w[1:2] * jnp.sum(ew_w * k, axis=0, keepdims=True)
        m_ref[...] = m_new
        return carry

    lax.fori_loop(0, ts // cl, chunk, 0, unroll=2)


def _mlstm_consts(cl):
    hid = np.arange(GROUP) // HEAD_DIM
    hm = np.zeros((8, GROUP), np.float32)
    hm[:HEADS] = (hid[None, :] == np.arange(HEADS)[:, None])
    segs = np.zeros((HEADS * cl, 128), np.float32)
    segs[np.arange(HEADS * cl), G_MF + np.arange(HEADS * cl) // cl] = 1.0
    segq = np.zeros((GROUP, 128), np.float32)
    segq[np.arange(GROUP), G_MF + hid] = 1.0
    return jnp.asarray(hm), jnp.asarray(segs, BF16), jnp.asarray(segq, BF16), jnp.asarray(segq.T, BF16)


def _mlstm(proj, gt, l, conv_w, bias_c, bias_r, out_g, tri, bd):
    nb, s, _ = proj.shape
    ts, cl = MLSTM_TS, MLSTM_L
    kern = functools.partial(_mlstm_kernel, ts=ts, cl=cl)
    hm, segs, segq, expand = _mlstm_consts(cl)
    const2 = lambda b, i: (0, 0)
    return pl.pallas_call(
        kern,
        out_shape=jax.ShapeDtypeStruct((nb, s, GROUP), BF16),
        grid=(nb, s // ts),
        in_specs=[pl.BlockSpec((None, ts, 2 * GROUP), lambda b, i: (b, i, C_MQK // (2 * GROUP))),
                  pl.BlockSpec((None, ts, GROUP), lambda b, i: (b, i, C_MV // GROUP)),
                  pl.BlockSpec((None, ts, GROUP), lambda b, i: (b, i, C_MO // GROUP)),
                  pl.BlockSpec((None, ts, 128), lambda b, i: (b, i, C_GATES // 128)),
                  pl.BlockSpec((None, GT_ROWS, ts), lambda b, i: (b, 0, i)),
                  pl.BlockSpec((None, 4, 2 * GROUP), lambda b, i: (l, 0, 0)),
                  pl.BlockSpec((None, 1, 128), lambda b, i: (l, 0, 0)),
                  pl.BlockSpec((None, GT_ROWS, 1), lambda b, i: (l, 0, 0)),
                  pl.BlockSpec((None, 1, GROUP), lambda b, i: (l, 0, 0)),
                  pl.BlockSpec((cl, cl), const2),
                  pl.BlockSpec((8, GROUP), const2),
                  pl.BlockSpec((HEADS * cl, 128), const2),
                  pl.BlockSpec((GROUP, 128), const2),
                  pl.BlockSpec((128, GROUP), const2),
                  pl.BlockSpec((GROUP, GROUP), const2)],
        out_specs=pl.BlockSpec((None, ts, GROUP), lambda b, i: (b, i, 0)),
        scratch_shapes=[pltpu.VMEM((8, 2 * GROUP), F32),
                        pltpu.VMEM((ts, GROUP), F32),
                        pltpu.VMEM((ts, GROUP), F32),
                        pltpu.VMEM((GROUP, GROUP), F32),
                        pltpu.VMEM((1, GROUP), F32),
                        pltpu.VMEM((1, 128), F32)],
        compiler_params=_cp(("arbitrary", "arbitrary")),
        name="mlstm",
    )(proj, proj, proj, proj, gt, conv_w, bias_c, bias_r, out_g, tri, hm, segs, segq, expand, bd)


def _fox_kernel(q_ref, k_ref, v_ref, gr_ref, br_ref, qg_ref, kg_ref, bd_ref, tri_ref, y_ref,
                kn_ref, vx_ref, fn_ref, m_ref, acc_ref, *, t, s_len):
    qi = pl.program_id(1)
    bd = bd_ref[...]

    @pl.when(qi == 0)
    def _():
        kg = kg_ref[...]
        ones = jnp.ones((t, HEAD_DIM), BF16)
        for r in range(s_len // t):
            rows = slice(r * t, (r + 1) * t)
            kn = _rms_heads(k_ref[rows, :], bd, kg)
            vv = v_ref[rows, :]
            for h in range(HEADS):
                sl = slice(h * HEAD_DIM, (h + 1) * HEAD_DIM)
                kn_ref[h, rows, :] = kn[:, sl].astype(BF16)
                vx_ref[h, rows, :] = jnp.concatenate([vv[:, sl].astype(BF16), ones], axis=1)
        carry = jnp.zeros((8, 1), F32)
        for r in range(s_len // t):
            cols = slice(r * t, (r + 1) * t)
            lf = _log_sigmoid(gr_ref[G_FF:G_FF + 8, cols] + br_ref[G_FF:G_FF + 8, :])
            cs = _dot_t_hi(lf, tri_ref[...]) + carry
            fn_ref[:, cols] = -LOG2E * cs
            carry = cs[:, t - 1:t]

    qn = _rms_heads(q_ref[...], bd, qg_ref[...]) * (QK_SCALE * LOG2E)
    qh = [qn[:, h * HEAD_DIM:(h + 1) * HEAD_DIM].astype(BF16) for h in range(HEADS)]
    causal = lax.broadcasted_iota(jnp.int32, (t, t), 1) <= lax.broadcasted_iota(jnp.int32, (t, t), 0)
    m_ref[...] = jnp.full(m_ref.shape, NEG, F32)
    acc_ref[...] = jnp.zeros(acc_ref.shape, F32)

    hs = range(HEADS)

    def scores(kb, masked):
        k0 = pl.multiple_of(kb * t, t)
        sc = [_dot_t(qh[h], kn_ref[h, pl.ds(k0, t), :]) + fn_ref[h:h + 1, pl.ds(k0, t)] for h in hs]
        return [jnp.where(causal, s, NEG) for s in sc] if masked else sc

    def consume(sc, kb):
        k0 = pl.multiple_of(kb * t, t)
        m_old = [m_ref[h] for h in hs]
        m_new = [jnp.maximum(m_old[h], jnp.max(sc[h], axis=1, keepdims=True)) for h in hs]
        p = [jnp.exp2(sc[h] - jnp.concatenate([m_new[h]] * (t // 128), axis=1)).astype(BF16) for h in hs]
        pv = [jnp.dot(p[h], vx_ref[h, pl.ds(k0, t), :], preferred_element_type=F32) for h in hs]
        for h in hs:
            acc_ref[h] = jnp.exp2(m_old[h] - m_new[h]) * acc_ref[h] + pv[h]
            m_ref[h] = m_new[h]

    def body(kp, c):
        consume(scores(2 * kp, False), 2 * kp)
        consume(scores(2 * kp + 1, False), 2 * kp + 1)
        return c

    lax.fori_loop(0, qi // 2, body, 0)

    @pl.when(qi % 2 == 1)
    def _():
        consume(scores(qi - 1, False), qi - 1)

    consume(scores(qi, True), qi)
    outs = []
    for h in range(HEADS):
        a = acc_ref[h]
        outs.append(a[:, :HEAD_DIM] / a[:, HEAD_DIM:])
    y_ref[...] = jnp.concatenate(outs, axis=1).astype(y_ref.dtype)


def _fox(proj, gt, l, bias_r, q_g, k_g, bd, tri):
    nb, s, _ = proj.shape
    t = ATT_T
    kern = functools.partial(_fox_kernel, t=t, s_len=s)
    return pl.pallas_call(
        kern,
        out_shape=jax.ShapeDtypeStruct((nb, s, GROUP), BF16),
        grid=(nb, s // t),
        in_specs=[pl.BlockSpec((None, t, GROUP), lambda b, i: (b, i, C_FQ // GROUP)),
                  pl.BlockSpec((None, s, GROUP), lambda b, i: (b, 0, C_FK // GROUP)),
                  pl.BlockSpec((None, s, GROUP), lambda b, i: (b, 0, C_FV // GROUP)),
                  pl.BlockSpec((None, GT_ROWS, s), lambda b, i: (b, 0, 0)),
                  pl.BlockSpec((None, GT_ROWS, 1), lambda b, i: (l, 0, 0)),
                  pl.BlockSpec((None, 1, GROUP), lambda b, i: (l, 0, 0)),
                  pl.BlockSpec((None, 1, GROUP), lambda b, i: (l, 0, 0)),
                  pl.BlockSpec((GROUP, GROUP), lambda b, i: (0, 0)),
                  pl.BlockSpec((t, t), lambda b, i: (0, 0))],
        out_specs=pl.BlockSpec((None, t, GROUP), lambda b, i: (b, i, 0)),
        scratch_shapes=[pltpu.VMEM((HEADS, s, HEAD_DIM), BF16),
                        pltpu.VMEM((HEADS, s, 2 * HEAD_DIM), BF16),
                        pltpu.VMEM((8, s), F32),
                        pltpu.VMEM((HEADS, t, 128), F32),
                        pltpu.VMEM((HEADS, t, 2 * HEAD_DIM), F32)],
        compiler_params=_cp(("arbitrary", "arbitrary")),
        name="fox",
    )(proj, proj, proj, gt, bias_r, q_g, k_g, bd, tri)


def _sb_kernel(q_ref, k_ref, v_ref, sm_ref, y_ref, kb_ref, vb_ref, rest_ref, acc_ref, *, t, s_len):
    qi = pl.program_id(1)

    @pl.when(qi == 0)
    def _():
        for r in range(s_len // t):
            rows = slice(r * t, (r + 1) * t)
            kk = k_ref[rows, :]
            vv = v_ref[rows, :]
            for h in range(HEADS):
                sl = slice(h * HEAD_DIM, (h + 1) * HEAD_DIM)
                kb_ref[h, rows, :] = kk[:, sl].astype(BF16)
                vb_ref[h, rows, :] = vv[:, sl].astype(BF16)

    qs = q_ref[...] * (QK_SCALE * LOG2E)
    qh = [qs[:, h * HEAD_DIM:(h + 1) * HEAD_DIM].astype(BF16) for h in range(HEADS)]
    strict = lax.broadcasted_iota(jnp.int32, (t, t), 1) < lax.broadcasted_iota(jnp.int32, (t, t), 0)
    from_here = sm_ref[...]
    rest_ref[...] = jnp.zeros(rest_ref.shape, F32)
    acc_ref[...] = jnp.zeros(acc_ref.shape, F32)

    def step(kb, masked):
        k0 = pl.multiple_of(kb * t, t)
        hs = range(HEADS)
        z = [_dot_t(qh[h], kb_ref[h, pl.ds(k0, t), :]) for h in hs]
        u = [jnp.maximum(z[h], 0.0) + jnp.log2(1.0 + jnp.exp2(-jnp.abs(z[h]))) for h in hs]
        if masked:
            u = [jnp.where(strict, v, 0.0) for v in u]
        incl = [jnp.dot(u[h].astype(BF16), from_here, preferred_element_type=F32) for h in hs]
        rest = [rest_ref[h] for h in hs]
        a = [jnp.exp2(z[h] - incl[h] - jnp.concatenate([rest[h]] * (t // 128), axis=1)) for h in hs]
        if masked:
            a = [jnp.where(strict, v, 0.0) for v in a]
        av = [jnp.dot(a[h].astype(BF16), vb_ref[h, pl.ds(k0, t), :], preferred_element_type=F32) for h in hs]
        for h in hs:
            acc_ref[h] += av[h]
            rest_ref[h] = rest[h] + incl[h][:, 0:1]

    step(qi, True)

    @pl.when(qi % 2 == 1)
    def _():
        step(qi - 1, False)

    def body(j, c):
        kb = 2 * (qi // 2 - 1 - j)
        step(kb + 1, False)
        step(kb, False)
        return c

    lax.fori_loop(0, qi // 2, body, 0)
    y_ref[...] = jnp.concatenate([acc_ref[h] for h in range(HEADS)], axis=1).astype(y_ref.dtype)


def _sb(proj, after):
    nb, s, _ = proj.shape
    t = ATT_T
    kern = functools.partial(_sb_kernel, t=t, s_len=s)
    return pl.pallas_call(
        kern,
        out_shape=jax.ShapeDtypeStruct((nb, s, GROUP), BF16),
        grid=(nb, s // t),
        in_specs=[pl.BlockSpec((None, t, GROUP), lambda b, i: (b, i, C_SQ // GROUP)),
                  pl.BlockSpec((None, s, GROUP), lambda b, i: (b, 0, C_SK // GROUP)),
                  pl.BlockSpec((None, s, GROUP), lambda b, i: (b, 0, C_SV // GROUP)),
                  pl.BlockSpec((t, t), lambda b, i: (0, 0))],
        out_specs=pl.BlockSpec((None, t, GROUP), lambda b, i: (b, i, 0)),
        scratch_shapes=[pltpu.VMEM((HEADS, s, HEAD_DIM), BF16),
                        pltpu.VMEM((HEADS, s, HEAD_DIM), BF16),
                        pltpu.VMEM((HEADS, t, 128), F32),
                        pltpu.VMEM((HEADS, t, HEAD_DIM), F32)],
        compiler_params=_cp(("arbitrary", "arbitrary")),
        name="stick_breaking",
    )(proj, proj, proj, after)


def _nsa_kernel(q_ref, kvs_ref, kvw_ref, ak_ref, av_ref, gc_ref, gb_ref, qg_ref, kg_ref, pos_ref,
                wk_ref, wv_ref, bd_ref, tc_ref, ovt_ref, tsb_ref, twb_ref, cb_ref, y_ref,
                ksel_ref, vsel_ref, kwin_ref, vwin_ref, kcmp_ref, vcmp_ref, m_ref, acc_ref,
                *, tq, s_len, n_sel):
    qi = pl.program_id(1)
    nc = s_len // CMP_STRIDE

    @pl.when(qi == 0)
    def _():
        zk = jnp.zeros((WINDOW, HEAD_DIM), BF16)
        zv = jnp.zeros((WINDOW, 2 * HEAD_DIM), BF16)
        ksel_ref[0:WINDOW, :] = zk
        vsel_ref[0:WINDOW, :] = zv
        kwin_ref[0:WINDOW, :] = zk
        vwin_ref[0:WINDOW, :] = zv
        ones = jnp.ones((512, HEAD_DIM), BF16)
        for r in range(s_len // 512):
            rows = slice(r * 512, (r + 1) * 512)
            dst = slice(WINDOW + r * 512, WINDOW + (r + 1) * 512)
            kvs = kvs_ref[rows, :]
            kvw = kvw_ref[rows, :]
            ksel_ref[dst, :] = _rms_lanes(kvs[:, :HEAD_DIM], kg_ref[1:2, :]).astype(BF16)
            kwin_ref[dst, :] = _rms_lanes(kvw[:, :HEAD_DIM], kg_ref[2:3, :]).astype(BF16)
            vsel_ref[dst, :] = jnp.concatenate([kvs[:, HEAD_DIM:].astype(BF16), ones], axis=1)
            vwin_ref[dst, :] = jnp.concatenate([kvw[:, HEAD_DIM:].astype(BF16), ones], axis=1)
        half = CMP_STRIDE * HEAD_DIM
        for a_ref, w_ref, dst_ref, norm in ((ak_ref, wk_ref, kcmp_ref, True), (av_ref, wv_ref, vcmp_ref, False)):
            a = a_ref[...]
            pos_b = _dot_hi(pos_ref[...], w_ref[...])[0:1, :]
            p1 = _dot_hi(a, w_ref[0:half, :])
            p2 = _dot_hi(a, w_ref[half:2 * half, :])
            c = p1 + pltpu.roll(p2, nc - 1, 0) + pos_b
            if norm:
                hi, lo = _split2(_rms_lanes(c, kg_ref[0:1, :]))
                kcmp_ref[0] = hi
                kcmp_ref[1] = lo
            else:
                dst_ref[...] = c.astype(BF16)

    t0 = qi * tq
    qn = _rms_heads(q_ref[...], bd_ref[...], qg_ref[...]) * (QK_SCALE * LOG2E)
    qf = [qn[:, h * HEAD_DIM:(h + 1) * HEAD_DIM] for h in range(HEADS)]
    qh = [q.astype(BF16) for q in qf]
    ql = [(qf[h] - qh[h].astype(F32)).astype(BF16) for h in range(HEADS)]

    hs = range(HEADS)
    k_hi, k_lo = kcmp_ref[0], kcmp_ref[1]
    dn = (((1,), (1,)), ((), ()))
    sc = [lax.dot_general(qh[h], k_hi, dn, preferred_element_type=F32)
          + lax.dot_general(ql[h], k_hi, dn, preferred_element_type=F32)
          + lax.dot_general(qh[h], k_lo, dn, preferred_element_type=F32) + tc_ref[h] for h in hs]
    e = [jnp.exp2(sc[h] - jnp.max(sc[h], axis=1, keepdims=True)) for h in hs]
    p = [jnp.where(tc_ref[h] > 0.5 * NEG, e[h] / jnp.sum(e[h], axis=1, keepdims=True), 0.0) for h in hs]
    o_cmp = [jnp.dot(p[h].astype(BF16), vcmp_ref[...], preferred_element_type=F32) for h in hs]
    ps_hi, ps_lo = _split2(p[0] + p[1] + p[2] + p[3])
    ovt = ovt_ref[...].astype(BF16)
    imp_t = (lax.dot_general(ovt, ps_hi, dn, preferred_element_type=F32)
             + lax.dot_general(ovt, ps_lo, dn, preferred_element_type=F32))

    nr = -(-n_sel // 8) * 8
    jj = lax.broadcasted_iota(jnp.int32, (nr, tq), 0)
    tt = t0 + lax.broadcasted_iota(jnp.int32, (nr, tq), 1)
    cur = tt // SEL_BLOCK
    forced = (jj == 0) | (jj == cur) | (jj == cur - 1)
    score = jnp.where(forced, -NEG, jnp.where(jj * SEL_BLOCK <= tt, imp_t[:nr], NEG))
    cnt = jnp.zeros((nr, tq), F32)
    for i in range(n_sel):
        ri = score[i:i + 1, :]
        cnt = cnt + jnp.where(jj > i, (ri >= score).astype(F32), (ri > score).astype(F32))
    sel_t = (cnt < float(min(N_SEL_TOP, n_sel))).astype(F32)
    if nr < 128:
        sel_t = jnp.concatenate([sel_t, jnp.zeros((128 - nr, tq), F32)], axis=0)
    selb = sel_t.T.astype(BF16)

    m_ref[...] = jnp.full(m_ref.shape, NEG, F32)
    acc_ref[...] = jnp.zeros(acc_ref.shape, F32)
    jrow = lax.broadcasted_iota(jnp.int32, (128, tq), 0)
    jcol = lax.broadcasted_iota(jnp.int32, (128, tq), 1) // SEL_BLOCK

    def sel_chunk(kp0, near):
        jb = kp0 // SEL_BLOCK - WINDOW // SEL_BLOCK
        expand = (jrow == jb + jcol).astype(BF16)
        picked = jnp.dot(selb, expand, preferred_element_type=F32) > 0.5
        kblk = ksel_ref[pl.ds(kp0, tq), :]
        vblk = vsel_ref[pl.ds(kp0, tq), :]
        hs = range(HEADS)
        bias = [cb_ref[h] if near is None else tsb_ref[h, :, near * tq:(near + 1) * tq] for h in hs]
        sck = [_dot_t(qh[h], kblk) + jnp.where(picked, bias[h], NEG) for h in hs]
        m_old = [m_ref[h] for h in hs]
        m_new = [jnp.maximum(m_old[h], jnp.max(sck[h], axis=1, keepdims=True)) for h in hs]
        pk = [jnp.exp2(sck[h] - jnp.concatenate([m_new[h]] * (tq // 128), axis=1)).astype(BF16) for h in hs]
        pv = [jnp.dot(pk[h], vblk, preferred_element_type=F32) for h in hs]
        for h in hs:
            acc_ref[h] = jnp.exp2(m_old[h] - m_new[h]) * acc_ref[h] + pv[h]
            m_ref[h] = m_new[h]

    def far(cp, carry):
        sel_chunk(pl.multiple_of(WINDOW + 2 * cp * tq, tq), None)
        sel_chunk(pl.multiple_of(WINDOW + (2 * cp + 1) * tq, tq), None)
        return carry

    n_far = jnp.maximum(qi - WINDOW // tq, 0)
    lax.fori_loop(0, n_far // 2, far, 0)

    @pl.when(n_far % 2 == 1)
    def _():
        sel_chunk(pl.multiple_of(WINDOW + (n_far - 1) * tq, tq), None)
    for r in range(NSA_NEAR // tq):
        @pl.when(qi + r >= WINDOW // tq)
        def _(r=r):
            sel_chunk(pl.multiple_of(t0 + r * tq, tq), r)

    kw = kwin_ref[pl.ds(pl.multiple_of(t0, tq), NSA_NEAR), :]
    vw = vwin_ref[pl.ds(pl.multiple_of(t0, tq), NSA_NEAR), :]
    in_seq = lax.broadcasted_iota(jnp.int32, (tq, NSA_NEAR), 1) >= WINDOW - t0
    g = _sigmoid(gc_ref[...] + gb_ref[...])
    sw = [jnp.where(in_seq, _dot_t(qh[h], kw) + twb_ref[h], NEG) for h in hs]
    pw = [jnp.exp2(sw[h] - jnp.max(sw[h], axis=1, keepdims=True)).astype(BF16) for h in hs]
    ows = [jnp.dot(pw[h], vw, preferred_element_type=F32) for h in hs]
    outs = []
    for h in hs:
        ow = ows[h]
        o_win = ow[:, :HEAD_DIM] / ow[:, HEAD_DIM:]
        a = acc_ref[h]
        o_sel = a[:, :HEAD_DIM] / a[:, HEAD_DIM:]
        c0 = G_NG + 3 * h
        outs.append(g[:, c0:c0 + 1] * o_cmp[h] + g[:, c0 + 1:c0 + 2] * o_sel + g[:, c0 + 2:c0 + 3] * o_win)
    y_ref[...] = jnp.concatenate(outs, axis=1).astype(y_ref.dtype)


def _nsa(proj, a_k, a_v, l, bias_c, q_g, k_g, pos, wk, wv, bd, tc, ovt, tsb, twb, cb):
    nb, s, _ = proj.shape
    tq = NSA_TQ
    nc = s // CMP_STRIDE
    kern = functools.partial(_nsa_kernel, tq=tq, s_len=s, n_sel=s // SEL_BLOCK)
    const2 = lambda b, i: (0, 0)
    const3 = lambda b, i: (0, 0, 0)
    lay3 = lambda b, i: (l, 0, 0)
    return pl.pallas_call(
        kern,
        out_shape=jax.ShapeDtypeStruct((nb, s, GROUP), BF16),
        grid=(nb, s // tq),
        in_specs=[pl.BlockSpec((None, tq, GROUP), lambda b, i: (b, i, C_NQ // GROUP)),
                  pl.BlockSpec((None, s, 128), lambda b, i: (b, 0, C_NKV // 128 + 1)),
                  pl.BlockSpec((None, s, 128), lambda b, i: (b, 0, C_NKV // 128 + 2)),
                  pl.BlockSpec((None, nc, CMP_STRIDE * HEAD_DIM), lambda b, i: (b, 0, 0)),
                  pl.BlockSpec((None, nc, CMP_STRIDE * HEAD_DIM), lambda b, i: (b, 0, 0)),
                  pl.BlockSpec((None, tq, 128), lambda b, i: (b, i, C_GATES // 128)),
                  pl.BlockSpec((None, 1, 128), lay3),
                  pl.BlockSpec((None, 1, GROUP), lay3),
                  pl.BlockSpec((None, 8, HEAD_DIM), lay3),
                  pl.BlockSpec((None, 8, CMP_BLOCK * HEAD_DIM), lay3),
                  pl.BlockSpec((None, CMP_BLOCK * HEAD_DIM, HEAD_DIM), lay3),
                  pl.BlockSpec((None, CMP_BLOCK * HEAD_DIM, HEAD_DIM), lay3),
                  pl.BlockSpec((GROUP, GROUP), const2),
                  pl.BlockSpec((HEADS, tq, nc), lambda b, i: (0, i, 0)),
                  pl.BlockSpec((128, nc), const2),
                  pl.BlockSpec((HEADS, tq, NSA_NEAR), const3),
                  pl.BlockSpec((HEADS, tq, NSA_NEAR), const3),
                  pl.BlockSpec(memory_space=pltpu.SMEM)],
        out_specs=pl.BlockSpec((None, tq, GROUP), lambda b, i: (b, i, 0)),
        scratch_shapes=[pltpu.VMEM((s + WINDOW, HEAD_DIM), BF16),
                        pltpu.VMEM((s + WINDOW, 2 * HEAD_DIM), BF16),
                        pltpu.VMEM((s + WINDOW, HEAD_DIM), BF16),
                        pltpu.VMEM((s + WINDOW, 2 * HEAD_DIM), BF16),
                        pltpu.VMEM((2, nc, HEAD_DIM), BF16),
                        pltpu.VMEM((nc, HEAD_DIM), BF16),
                        pltpu.VMEM((HEADS, tq, 128), F32),
                        pltpu.VMEM((HEADS, tq, 2 * HEAD_DIM), F32)],
        compiler_params=_cp(("arbitrary", "arbitrary")),
        name="nsa",
    )(proj, proj, proj, a_k, a_v, proj, bias_c, q_g, k_g, pos, wk, wv, bd, tc, ovt, tsb, twb, cb)


def _outproj_kernel(x_ref, ya_ref, yb_ref, yc_ref, yd_ref, w_ref, mod_ref, g_ref, xo_ref, h_ref):
    acc = jnp.dot(ya_ref[...], w_ref[0:GROUP, :], preferred_element_type=F32)
    acc += jnp.dot(yb_ref[...], w_ref[GROUP:2 * GROUP, :], preferred_element_type=F32)
    acc += jnp.dot(yc_ref[...], w_ref[2 * GROUP:3 * GROUP, :], preferred_element_type=F32)
    acc += jnp.dot(yd_ref[...], w_ref[3 * GROUP:4 * GROUP, :], preferred_element_type=F32)
    xn = x_ref[...] + mod_ref[2:3, :] * acc
    xo_ref[...] = xn
    h = _rms_lanes(xn, g_ref[...]) * (1.0 + mod_ref[4:5, :]) + mod_ref[3:4, :]
    h_ref[...] = h.astype(h_ref.dtype)


def _outproj(x, ys, mod, l, w_out, norm_g, h_dtype):
    nb, s, d = x.shape
    tm = 512
    yspec = pl.BlockSpec((None, tm, GROUP), lambda b, i: (b, i, 0))
    xspec = pl.BlockSpec((None, tm, d), lambda b, i: (b, i, 0))
    return pl.pallas_call(
        _outproj_kernel,
        out_shape=(jax.ShapeDtypeStruct((nb, s, d), F32), jax.ShapeDtypeStruct((nb, s, d), h_dtype)),
        grid=(nb, s // tm),
        in_specs=[xspec, yspec, yspec, yspec, yspec,
                  pl.BlockSpec((None, d, d), lambda b, i: (l, 0, 0)),
                  pl.BlockSpec((None, None, 6, d), lambda b, i: (l, b, 0, 0)),
                  pl.BlockSpec((None, 1, d), lambda b, i: (l, 0, 0))],
        out_specs=(xspec, xspec),
        compiler_params=_cp(("arbitrary", "arbitrary")),
        name="outproj",
    )(x, *ys, w_out, mod, norm_g)


def _ffn_kernel(h_ref, x_ref, mod_ref, wg_ref, wu_ref, wd_ref, o_ref, acc_ref, *, nf):
    f = pl.program_id(2)

    @pl.when(f == 0)
    def _():
        acc_ref[...] = jnp.zeros_like(acc_ref)

    h = h_ref[...]
    a = jnp.dot(h, wg_ref[...], preferred_element_type=F32)
    u = jnp.dot(h, wu_ref[...], preferred_element_type=F32)
    act = (a * _sigmoid(a) * u).astype(BF16)
    acc_ref[...] += jnp.dot(act, wd_ref[...], preferred_element_type=F32)

    @pl.when(f == nf - 1)
    def _():
        o_ref[...] = x_ref[...] + mod_ref[5:6, :] * acc_ref[...]


def _ffn(h, x, mod, l, li, wg, wu, wd):
    nb, s, d = x.shape
    dff = wg.shape[-1]
    tm, tf = FFN_TM, FFN_TF
    nf = dff // tf
    xspec = pl.BlockSpec((None, tm, d), lambda b, i, f: (b, i, 0))
    return pl.pallas_call(
        functools.partial(_ffn_kernel, nf=nf),
        out_shape=jax.ShapeDtypeStruct((nb, s, d), F32),
        grid=(nb, s // tm, nf),
        in_specs=[xspec, xspec,
                  pl.BlockSpec((None, None, 6, d), lambda b, i, f: (l, b, 0, 0)),
                  pl.BlockSpec((None, d, tf), lambda b, i, f: (li, 0, f)),
                  pl.BlockSpec((None, d, tf), lambda b, i, f: (li, 0, f)),
                  pl.BlockSpec((None, tf, d), lambda b, i, f: (li, f, 0))],
        out_specs=xspec,
        scratch_shapes=[pltpu.VMEM((tm, d), F32)],
        compiler_params=_cp(("arbitrary", "arbitrary", "arbitrary")),
        name="ffn_dense",
    )(h, x, mod, wg, wu, wd)


def _router_kernel(h_ref, w_ref, b_ref, o_ref):
    logits = _dot_hi(h_ref[...], w_ref[...]) + b_ref[...]
    lane = lax.broadcasted_iota(jnp.int32, logits.shape, 1).astype(F32)
    lg = jnp.where(lane < N_EXPERTS, logits, -3e38)
    m1 = jnp.max(lg, axis=1, keepdims=True)
    i1 = jnp.min(jnp.where(lg == m1, lane, 128.0), axis=1, keepdims=True)
    lg2 = jnp.where(lane == i1, -3e38, lg)
    m2 = jnp.max(lg2, axis=1, keepdims=True)
    i2 = jnp.min(jnp.where(lg2 == m2, lane, 128.0), axis=1, keepdims=True)
    e2 = jnp.exp(m2 - m1)
    w1 = 1.0 / (1.0 + e2)
    w2 = e2 / (1.0 + e2)
    o_ref[...] = jnp.where(lane == 0, i1, jnp.where(lane == 1, i2, jnp.where(lane == 2, w1, jnp.where(lane == 3, w2, 0.0))))


def _router(hf, rw, rb):
    t, d = hf.shape
    tm = 1024
    return pl.pallas_call(
        _router_kernel,
        out_shape=jax.ShapeDtypeStruct((t, 128), F32),
        grid=(t // tm,),
        in_specs=[pl.BlockSpec((tm, d), lambda i: (i, 0)),
                  pl.BlockSpec((d, 128), lambda i: (0, 0)),
                  pl.BlockSpec((1, 128), lambda i: (0, 0))],
        out_specs=pl.BlockSpec((tm, 128), lambda i: (i, 0)),
        compiler_params=_cp(("arbitrary",)),
        name="moe_router",
    )(hf, rw, rb)


def _gather_copy(src_hbm, row, dst_buf, slot, r, sem):
    return pltpu.make_async_copy(src_hbm.at[pl.ds(row, 1)], dst_buf.at[slot, pl.ds(r, 1)], sem.at[slot])


def _experts_kernel(be_ref, tok_ref, nv_ref, h_hbm, wg_ref, wu_ref, wd_ref, y_ref,
                    xbuf, xb_ref, acc_ref, sem, *, bm, nf):
    del be_ref
    i = pl.program_id(0)
    f = pl.program_id(1)
    nvalid = nv_ref[0]
    slot = i % 2
    chunk = bm // nf

    def for_rows(fn):
        def body(r, c):
            fn(r)
            return c
        lax.fori_loop(0, bm, body, 0, unroll=8)

    @pl.when((f == 0) & (i == 0))
    def _():
        for_rows(lambda r: _gather_copy(h_hbm, tok_ref[r], xbuf, 0, r, sem).start())

    @pl.when((f == 0) & (i < nvalid))
    def _():
        for_rows(lambda r: _gather_copy(h_hbm, 0, xbuf, slot, r, sem).wait())
        xb_ref[...] = xbuf[slot].astype(BF16)
        acc_ref[...] = jnp.zeros_like(acc_ref)

    def compute(prefetch):
        third = -(-chunk // 3)

        def issue_rows(lo, hi):
            if prefetch:
                for j in range(lo, min(hi, chunk)):
                    r = f * chunk + j
                    _gather_copy(h_hbm, tok_ref[(i + 1) * bm + r], xbuf, 1 - slot, r, sem).start()

        x = xb_ref[...]
        issue_rows(0, third)
        a = jnp.dot(x, wg_ref[...], preferred_element_type=F32)
        issue_rows(third, 2 * third)
        u = jnp.dot(x, wu_ref[...], preferred_element_type=F32)
        issue_rows(2 * third, 3 * third)
        act = (a * _sigmoid(a) * u).astype(BF16)
        acc_ref[...] += jnp.dot(act, wd_ref[...], preferred_element_type=F32)

    @pl.when(i + 1 < nvalid)
    def _():
        compute(True)

    @pl.when(i + 1 == nvalid)
    def _():
        compute(False)

    @pl.when(f == nf - 1)
    def _():
        @pl.when(i < nvalid)
        def _():
            y_ref[...] = acc_ref[...]

        @pl.when(i >= nvalid)
        def _():
            y_ref[...] = jnp.zeros_like(y_ref)


def _experts(hf, blk_expert, row_tok, nvalid, wg, wu, wd, li):
    t, d = hf.shape
    bm, tf = MOE_BM, MOE_TF
    n_rows = row_tok.shape[0]
    n_blocks = n_rows // bm
    dff = wg.shape[-1]
    nf = dff // tf

    def fidx(i, f, nv):
        return jnp.where(i < nv[0], f, nf - 1)

    grid_spec = pltpu.PrefetchScalarGridSpec(
        num_scalar_prefetch=3,
        grid=(n_blocks, nf),
        in_specs=[pl.BlockSpec(memory_space=pl.ANY),
                  pl.BlockSpec((None, None, d, tf), lambda i, f, be, tok, nv: (li, be[i], 0, fidx(i, f, nv))),
                  pl.BlockSpec((None, None, d, tf), lambda i, f, be, tok, nv: (li, be[i], 0, fidx(i, f, nv))),
                  pl.BlockSpec((None, None, tf, d), lambda i, f, be, tok, nv: (li, be[i], fidx(i, f, nv), 0))],
        out_specs=pl.BlockSpec((bm, d), lambda i, f, be, tok, nv: (i, 0)),
        scratch_shapes=[pltpu.VMEM((2, bm, d), F32),
                        pltpu.VMEM((bm, d), BF16),
                        pltpu.VMEM((bm, d), F32),
                        pltpu.SemaphoreType.DMA((2,))],
    )
    return pl.pallas_call(
        functools.partial(_experts_kernel, bm=bm, nf=nf),
        out_shape=jax.ShapeDtypeStruct((n_rows, d), F32),
        grid_spec=grid_spec,
        compiler_params=_cp(("arbitrary", "arbitrary")),
        name="moe_experts",
    )(blk_expert, row_tok, nvalid, hf, wg, wu, wd)


def _combine_kernel(pos_ref, y_hbm, x_ref, mod_ref, rt_ref, o_ref, ybuf, sem, *, tm, s_len):
    b = pl.program_id(0)
    i = pl.program_id(1)
    base = (b * s_len + i * tm) * 2

    def issue(r, c):
        _gather_copy(y_hbm, pos_ref[base + 2 * r], ybuf, 0, r, sem).start()
        _gather_copy(y_hbm, pos_ref[base + 2 * r + 1], ybuf, 1, r, sem).start()
        return c

    def wait(r, c):
        _gather_copy(y_hbm, 0, ybuf, 0, r, sem).wait()
        _gather_copy(y_hbm, 0, ybuf, 1, r, sem).wait()
        return c

    lax.fori_loop(0, tm, issue, 0, unroll=8)
    lax.fori_loop(0, tm, wait, 0, unroll=8)
    rt = rt_ref[...]
    o_ref[...] = x_ref[...] + mod_ref[5:6, :] * (rt[:, 2:3] * ybuf[0] + rt[:, 3:4] * ybuf[1])


def _combine(y, pos, x, mod, route, l):
    nb, s, d = x.shape
    tm = 256
    grid_spec = pltpu.PrefetchScalarGridSpec(
        num_scalar_prefetch=1,
        grid=(nb, s // tm),
        in_specs=[pl.BlockSpec(memory_space=pl.ANY),
                  pl.BlockSpec((None, tm, d), lambda b, i, p: (b, i, 0)),
                  pl.BlockSpec((None, None, 6, d), lambda b, i, p: (l, b, 0, 0)),
                  pl.BlockSpec((None, tm, 128), lambda b, i, p: (b, i, 0))],
        out_specs=pl.BlockSpec((None, tm, d), lambda b, i, p: (b, i, 0)),
        scratch_shapes=[pltpu.VMEM((2, tm, d), F32), pltpu.SemaphoreType.DMA((2,))],
    )
    return pl.pallas_call(
        functools.partial(_combine_kernel, tm=tm, s_len=s),
        out_shape=jax.ShapeDtypeStruct((nb, s, d), F32),
        grid_spec=grid_spec,
        compiler_params=_cp(("arbitrary", "arbitrary")),
        name="moe_combine",
    )(pos, y, x, mod, route.reshape(nb, s, 128))


def _moe(hf32, x, mod, l, li, router_w, router_b, wg, wu, wd):
    nb, s, d = x.shape
    t = nb * s
    bm = MOE_BM
    hf = hf32.reshape(t, d)
    rw = jnp.zeros((d, 128), F32).at[:, :N_EXPERTS].set(router_w[li])
    rb = jnp.zeros((1, 128), F32).at[0, :N_EXPERTS].set(router_b[li])
    route = _router(hf, rw, rb)
    e_flat = route[:, 0:2].astype(jnp.int32).reshape(-1)
    n_assign = 2 * t
    onehot = (e_flat[:, None] == jnp.arange(N_EXPERTS, dtype=jnp.int32)[None, :]).astype(jnp.int32)
    csum = jnp.cumsum(onehot, axis=0)
    rank = jnp.sum(onehot * csum, axis=1) - 1
    counts = csum[-1]
    padded = (counts + bm - 1) // bm * bm
    cum_padded = jnp.cumsum(padded)
    pstart = cum_padded - padded
    dest = pstart[e_flat] + rank
    n_blocks = n_assign // bm + N_EXPERTS
    n_rows = n_blocks * bm
    tok_flat = jnp.arange(n_assign, dtype=jnp.int32) // 2
    row_tok = jnp.zeros((n_rows,), jnp.int32).at[dest].set(tok_flat)
    blk_start = jnp.arange(n_blocks, dtype=jnp.int32) * bm
    blk_expert = jnp.minimum(jnp.searchsorted(cum_padded, blk_start, side='right'), N_EXPERTS - 1).astype(jnp.int32)
    nvalid = (cum_padded[-1] // bm).astype(jnp.int32).reshape(1)
    y = _experts(hf, blk_expert, row_tok, nvalid, wg, wu, wd, li)
    return _combine(y, dest.astype(jnp.int32), x, mod, route, l)


def _t5_bucket(dist):
    n = np.maximum(dist, 0)
    max_exact = NUM_BUCKETS // 2
    large = max_exact + (np.log(np.maximum(n, 1).astype(np.float32) / max_exact)
                         / math.log(MAX_DISTANCE / max_exact) * (NUM_BUCKETS - max_exact)).astype(np.int32)
    return np.where(n < max_exact, n, np.minimum(large, NUM_BUCKETS - 1)).astype(np.int32)


def _bias_by_bucket(rb, bucket):
    ids = jnp.asarray(bucket.astype(np.int8))[None]
    out = jnp.zeros((rb.shape[1],) + bucket.shape, F32)
    for k in range(NUM_BUCKETS):
        out = jnp.where(ids == k, rb[k][:, None, None], out)
    return out


def _nsa_tables(rel_bias, s):
    tq = NSA_TQ
    nc = s // CMP_STRIDE
    n_cmp = (s - CMP_BLOCK) // CMP_STRIDE + 1
    rb = rel_bias.astype(F32) * LOG2E
    t = np.arange(s)[:, None]
    cmp_end = np.arange(nc)[None, :] * CMP_STRIDE + CMP_BLOCK - 1
    ok = (cmp_end <= t) & (np.arange(nc)[None, :] < n_cmp)
    tc = jnp.where(jnp.asarray(ok)[None], _bias_by_bucket(rb, _t5_bucket(t - cmp_end)), NEG)
    dist = np.arange(tq)[:, None] - (np.arange(NSA_NEAR)[None, :] - WINDOW)
    tb = _bias_by_bucket(rb, _t5_bucket(dist))
    tsb = jnp.where(jnp.asarray(dist >= 0)[None], tb, NEG)
    twb = jnp.where(jnp.asarray((dist >= 0) & (dist < WINDOW))[None], tb, NEG)
    far_bucket = int(_t5_bucket(np.array([WINDOW]))[0])
    cb = rb[far_bucket]
    n_sel = s // SEL_BLOCK
    cs = np.arange(nc)[None, :] * CMP_STRIDE
    ss = np.arange(128)[:, None] * SEL_BLOCK
    ov = np.clip(np.minimum(cs + CMP_BLOCK, ss + SEL_BLOCK) - np.maximum(cs, ss), 0, None).astype(np.float32) / CMP_BLOCK
    ov = ov * (np.arange(128)[:, None] < n_sel) * (np.arange(nc)[None, :] < n_cmp)
    return tc, jnp.asarray(ov, F32), tsb, twb, cb


def _prep_layer_params(w_in, mlstm_gate_b, fox_f_b, nsa_gate_b):
    depth, d, _ = w_in.shape
    small = [w_in[:, :, 1024:1032], w_in[:, :, 1800:1804], w_in[:, :, 3212:3224]]
    w_p = jnp.concatenate([w_in[:, :, 0:1024], w_in[:, :, 1032:1800], w_in[:, :, 1804:2572], w_in[:, :, 2572:3212]]
                          + small + [jnp.zeros((depth, d, 128 - 24), w_in.dtype)], axis=-1).astype(BF16)
    wgt = jnp.concatenate(small + [jnp.zeros((depth, d, GT_ROWS - 24), w_in.dtype)], axis=-1)
    wgt = jnp.transpose(wgt, (0, 2, 1)).astype(BF16)
    gate_b = jnp.concatenate([mlstm_gate_b, fox_f_b, nsa_gate_b], axis=-1).astype(F32)
    bias_c = jnp.zeros((depth, 1, 128), F32).at[:, 0, :24].set(gate_b)
    bias_r = jnp.zeros((depth, GT_ROWS, 1), F32).at[:, :24, 0].set(gate_b)
    return w_p, wgt, bias_c, bias_r


def kernel(x, c, rel_bias, ada_w, ada_b, norm1_g, norm2_g, w_in, w_out, mlstm_conv_w, mlstm_gate_b, mlstm_out_g,
           fox_f_b, fox_q_g, fox_k_g, nsa_q_g, nsa_k_g, nsa_cmp_pos, nsa_cmp_wk, nsa_cmp_wv, nsa_gate_b,
           ffn_wg, ffn_wu, ffn_wd, moe_router_w, moe_router_b, moe_wg, moe_wu, moe_wd):
    nb, s, d = x.shape
    depth = w_in.shape[0]
    assert d == D_MODEL and s % 512 == 0 and s // SEL_BLOCK <= 128

    w_p, wgt, bias_c, bias_r = _prep_layer_params(w_in, mlstm_gate_b, fox_f_b, nsa_gate_b)
    w_out_b = w_out.astype(BF16)
    n1g = norm1_g.reshape(depth, 1, d)
    n2g = norm2_g.reshape(depth, 1, d)
    m_out_g = mlstm_out_g.reshape(depth, 1, GROUP)
    fq_g = jnp.tile(fox_q_g, (1, HEADS)).reshape(depth, 1, GROUP)
    fk_g = jnp.tile(fox_k_g, (1, HEADS)).reshape(depth, 1, GROUP)
    nq_g = jnp.tile(nsa_q_g, (1, HEADS)).reshape(depth, 1, GROUP)
    nk_g = jnp.zeros((depth, 8, HEAD_DIM), F32).at[:, :3].set(nsa_k_g)
    pos8 = jnp.zeros((depth, 8, CMP_BLOCK * HEAD_DIM), F32).at[:, 0].set(nsa_cmp_pos.reshape(depth, -1))
    ffn_wg_b, ffn_wu_b, ffn_wd_b = ffn_wg.astype(BF16), ffn_wu.astype(BF16), ffn_wd.astype(BF16)
    moe_wg_b, moe_wu_b, moe_wd_b = moe_wg.astype(BF16), moe_wu.astype(BF16), moe_wd.astype(BF16)
    hid = np.arange(GROUP) // HEAD_DIM
    bd = jnp.asarray((hid[:, None] == hid[None, :]).astype(np.float32) / HEAD_DIM)
    tri_l = jnp.asarray(np.tril(np.ones((MLSTM_L, MLSTM_L), np.float32)))
    tri_t = jnp.asarray(np.tril(np.ones((ATT_T, ATT_T), np.float32)))
    from_here = jnp.asarray(np.tril(np.ones((ATT_T, ATT_T), np.float32)), BF16)
    tc, ovt, tsb, twb, cb = _nsa_tables(rel_bias, s)

    mod = _adaln(c, ada_w, ada_b)
    for l in range(depth):
        proj, gt = _inproj(x, mod, l, n1g, w_p, wgt)
        y_a = _mlstm(proj, gt, l, mlstm_conv_w, bias_c, bias_r, m_out_g, tri_l, bd)
        y_b = _fox(proj, gt, l, bias_r, fq_g, fk_g, bd, tri_t)
        y_c = _sb(proj, from_here)
        a_k = proj[:, :, C_NKV:C_NKV + HEAD_DIM].reshape(nb, s // CMP_STRIDE, CMP_STRIDE * HEAD_DIM)
        a_v = proj[:, :, C_NKV + HEAD_DIM:C_NKV + 2 * HEAD_DIM].reshape(nb, s // CMP_STRIDE, CMP_STRIDE * HEAD_DIM)
        y_d = _nsa(proj, a_k, a_v, l, bias_c, nq_g, nk_g, pos8, nsa_cmp_wk, nsa_cmp_wv, bd, tc, ovt, tsb, twb, cb)
        if l % 2 == 0:
            x, h2 = _outproj(x, (y_a, y_b, y_c, y_d), mod, l, w_out_b, n2g, BF16)
            x = _ffn(h2, x, mod, l, l // 2, ffn_wg_b, ffn_wu_b, ffn_wd_b)
        else:
            x, h2 = _outproj(x, (y_a, y_b, y_c, y_d), mod, l, w_out_b, n2g, F32)
            x = _moe(h2, x, mod, l, l // 2, moe_router_w, moe_router_b, moe_wg_b, moe_wu_b, moe_wd_b)
    return x
```

```python
import functools
import math

import numpy as np
import jax
import jax.numpy as jnp
from jax import lax
from jax.experimental import pallas as pl
from jax.experimental.pallas import tpu as pltpu

F32 = jnp.float32
BF16 = jnp.bfloat16
HI = lax.Precision.HIGHEST

D_MODEL = 1024
HEADS = 4
HEAD_DIM = 64
GROUP = HEADS * HEAD_DIM
NORM_EPS = 1e-6
NEG = -1e30
QK_SCALE = HEAD_DIM ** -0.5
LOG2E = 1.4426950408889634
CMP_BLOCK = 32
CMP_STRIDE = 16
SEL_BLOCK = 64
N_SEL_TOP = 16
WINDOW = 512
NUM_BUCKETS = 32
MAX_DISTANCE = 128
N_EXPERTS = 8
VMEM_LIMIT = 56 * 1024 * 1024

PW = 3328
C_MQK, C_MV, C_MO = 0, 512, 768
C_FQ, C_FK, C_FV = 1024, 1280, 1536
C_SQ, C_SK, C_SV = 1792, 2048, 2304
C_NQ, C_NKV, C_GATES = 2560, 2816, 3200
G_MI, G_MF, G_FF, G_NG = 0, 4, 8, 12
GT_ROWS = 32

MLSTM_TS = 512
MLSTM_L = 128
ATT_T = 256
FOX_TQ = 512
NSA_TQ = 256
NSA_NEAR = WINDOW + NSA_TQ
MOE_BM = 512
MOE_TF = 1792
FFN_TM = 512
FFN_TF = 1408


def _cp(sem, vmem=VMEM_LIMIT):
    return pltpu.CompilerParams(dimension_semantics=sem, vmem_limit_bytes=vmem)


def _dot(a, b):
    return jnp.dot(a.astype(BF16), b.astype(BF16), preferred_element_type=F32)


def _dot_t(a, b):
    return lax.dot_general(a.astype(BF16), b.astype(BF16), (((1,), (1,)), ((), ())), preferred_element_type=F32)


def _dot_hi(a, b):
    return jnp.dot(a, b, precision=HI, preferred_element_type=F32)


def _dot_t_hi(a, b):
    return lax.dot_general(a, b, (((1,), (1,)), ((), ())), precision=HI, preferred_element_type=F32)


def _sigmoid(x):
    return 1.0 / (1.0 + jnp.exp(-x))


def _log_sigmoid(x):
    return jnp.minimum(x, 0.0) - jnp.log1p(jnp.exp(-jnp.abs(x)))


def _rms_lanes(x, g):
    return x * lax.rsqrt(jnp.mean(x * x, axis=-1, keepdims=True) + NORM_EPS) * g


def _split2(a):
    hi = a.astype(BF16)
    return hi, (a - hi.astype(F32)).astype(BF16)


def _dot_2x(a, b):
    hi, lo = _split2(a)
    bb = b.astype(BF16)
    return jnp.dot(hi, bb, preferred_element_type=F32) + jnp.dot(lo, bb, preferred_element_type=F32)


def _dot_t_2x(a, b):
    hi, lo = _split2(a)
    bb = b.astype(BF16)
    dn = (((1,), (1,)), ((), ()))
    return (lax.dot_general(hi, bb, dn, preferred_element_type=F32)
            + lax.dot_general(lo, bb, dn, preferred_element_type=F32))


def _rms_heads(x, bd, g):
    return x * lax.rsqrt(_dot_2x(x * x, bd) + NORM_EPS) * g


def _adaln_kernel(c_ref, w_ref, b_ref, o_ref):
    c = c_ref[...]
    o_ref[...] = _dot_hi(c * _sigmoid(c), w_ref[...]) + b_ref[...]


def _adaln(c, ada_w, ada_b):
    depth, d, six_d = ada_w.shape
    nb = c.shape[0]
    out = pl.pallas_call(
        _adaln_kernel,
        out_shape=jax.ShapeDtypeStruct((depth, nb, six_d), F32),
        grid=(depth, six_d // d),
        in_specs=[pl.BlockSpec((nb, d), lambda l, j: (0, 0)),
                  pl.BlockSpec((None, d, d), lambda l, j: (l, 0, j)),
                  pl.BlockSpec((None, 1, d), lambda l, j: (l, 0, j))],
        out_specs=pl.BlockSpec((None, nb, d), lambda l, j: (l, 0, j)),
        compiler_params=_cp(("arbitrary", "arbitrary")),
        name="adaln",
    )(c, ada_w, ada_b.reshape(depth, 1, six_d))
    return out.reshape(depth, nb, 6, d)


def _inproj_kernel(x_ref, mod_ref, g_ref, w_ref, wgt_ref, o_ref, gt_ref):
    h = _rms_lanes(x_ref[...], g_ref[...]) * (1.0 + mod_ref[1:2, :]) + mod_ref[0:1, :]
    hb = h.astype(BF16)
    o_ref[...] = jnp.dot(hb, w_ref[...], preferred_element_type=F32)
    gt_ref[...] = lax.dot_general(wgt_ref[...], hb, (((1,), (1,)), ((), ())), preferred_element_type=F32)


def _inproj(x, mod, l, norm_g, w_p, wgt):
    nb, s, d = x.shape
    tm = 256
    return pl.pallas_call(
        _inproj_kernel,
        out_shape=(jax.ShapeDtypeStruct((nb, s, PW), F32), jax.ShapeDtypeStruct((nb, GT_ROWS, s), F32)),
        grid=(nb, s // tm),
        in_specs=[pl.BlockSpec((None, tm, d), lambda b, i: (b, i, 0)),
                  pl.BlockSpec((None, None, 6, d), lambda b, i: (l, b, 0, 0)),
                  pl.BlockSpec((None, 1, d), lambda b, i: (l, 0, 0)),
                  pl.BlockSpec((None, d, PW), lambda b, i: (l, 0, 0)),
                  pl.BlockSpec((None, GT_ROWS, d), lambda b, i: (l, 0, 0))],
        out_specs=(pl.BlockSpec((None, tm, PW), lambda b, i: (b, i, 0)),
                   pl.BlockSpec((None, GT_ROWS, tm), lambda b, i: (b, 0, i))),
        compiler_params=_cp(("arbitrary", "arbitrary")),
        name="inproj",
    )(x, mod, norm_g, w_p, wgt)


def _mlstm_kernel(qk_ref, v_ref, op_ref, gc_ref, gr_ref, cw_ref, bc_ref, br_ref, og_ref, tri_ref,
                  hm_ref, segs_ref, segq_ref, exp_ref, bd_ref, y_ref,
                  prev_ref, qs_ref, ks_ref, c_ref, n_ref, m_ref, *, ts, cl):
    @pl.when(pl.program_id(1) == 0)
    def _():
        prev_ref[...] = jnp.zeros_like(prev_ref)
        c_ref[...] = jnp.zeros_like(c_ref)
        n_ref[...] = jnp.zeros_like(n_ref)
        m_ref[...] = jnp.zeros_like(m_ref)

    x = qk_ref[...]
    xc = jnp.concatenate([prev_ref[...], x], axis=0)
    cw = cw_ref[...]
    y = (cw[0:1] * xc[8:8 + ts] + cw[1:2] * xc[7:7 + ts] + cw[2:3] * xc[6:6 + ts] + cw[3:4] * xc[5:5 + ts])
    prev_ref[...] = x[ts - 8:ts]
    y = y * _sigmoid(y)
    qs_ref[...] = y[:, :GROUP]
    ks_ref[...] = y[:, GROUP:] * QK_SCALE

    tri = tri_ref[...]
    lower = lax.broadcasted_iota(jnp.int32, (cl, cl), 0) >= lax.broadcasted_iota(jnp.int32, (cl, cl), 1)

    def chunk(ci, carry):
        r0 = pl.multiple_of(ci * cl, cl)
        gc = gc_ref[pl.ds(r0, cl), :] + bc_ref[...]
        gr = gr_ref[0:8, pl.ds(r0, cl)] + br_ref[0:8, :]
        lf_hi, lf_lo = _split2(_log_sigmoid(gc))
        trib = tri.astype(BF16)
        b_c = (jnp.dot(trib, lf_hi, preferred_element_type=F32)
               + jnp.dot(trib, lf_lo, preferred_element_type=F32))
        b_r = _dot_t_2x(_log_sigmoid(gr), tri)
        q = qs_ref[pl.ds(r0, cl), :]
        k = ks_ref[pl.ds(r0, cl), :]
        v = v_ref[pl.ds(r0, cl), :]
        kb = k.astype(BF16)
        hs = range(HEADS)
        ig_c = pltpu.roll(gc, G_MF - G_MI, 1)
        ig_r = pltpu.roll(gr, G_MF - G_MI, 0)
        u_c = ig_c - b_c
        u_r = ig_r - b_r
        g_row = b_c[cl - 1:cl, :]
        w_c = g_row + u_c
        m_loc = jnp.max(w_c, axis=0, keepdims=True)
        ew_c = jnp.exp(w_c - m_loc)
        m_prev = m_ref[...]
        pm = u_c
        row = lax.broadcasted_iota(jnp.int32, (cl, 128), 0)
        sh = 1
        while sh < cl:
            pm = jnp.where(row >= sh, jnp.maximum(pm, pltpu.roll(pm, sh, 0)), pm)
            sh *= 2
        mm_c = jnp.maximum(pm, m_prev)
        iw_c = jnp.exp(m_prev - mm_c)
        em_c = jnp.exp(-(b_c + mm_c))
        sb = []
        for h in hs:
            qm = jnp.where(hm_ref[h:h + 1, :] > 0.5, q, 0.0).astype(BF16)
            arg = jnp.where(lower, u_r[G_MF + h:G_MF + h + 1, :] - mm_c[:, G_MF + h:G_MF + h + 1], NEG)
            sb.append((_dot_t(qm, kb) * jnp.exp(arg)).astype(BF16))
        den_s = jnp.dot(jnp.concatenate(sb, axis=1), segs_ref[...], preferred_element_type=F32)
        pv = None
        for h in hs:
            vm = jnp.where(hm_ref[h:h + 1, :] > 0.5, v, 0.0).astype(BF16)
            t_h = jnp.dot(sb[h], vm, preferred_element_type=F32)
            pv = t_h if pv is None else pv + t_h
        c_prev = c_ref[...]
        n_prev = n_ref[...]
        inter = _dot_t(q, c_prev)
        qn_c = _dot_2x(q * n_prev, segq_ref[...])
        den_c = den_s + iw_c * qn_c
        lane = lax.broadcasted_iota(jnp.int32, (cl, 128), 1)
        r_c = jnp.where((lane >= G_MF) & (lane < G_MF + HEADS), 1.0 / jnp.maximum(jnp.abs(den_c), em_c), 0.0)
        expand = exp_ref[...]
        wide = _dot_2x(jnp.concatenate([ew_c, iw_c, r_c], axis=0), expand)
        ew_w, iw_w, r_w = wide[0:cl], wide[cl:2 * cl], wide[2 * cl:3 * cl]
        hh = (pv + iw_w * inter) * r_w
        hn = _rms_heads(hh, bd_ref[...], og_ref[...])
        y_ref[pl.ds(r0, cl), :] = (hn * _sigmoid(op_ref[pl.ds(r0, cl), :])).astype(y_ref.dtype)
        m_new = jnp.maximum(g_row + m_prev, m_loc)
        ab = jnp.concatenate([jnp.exp(g_row + m_prev - m_new), jnp.exp(m_loc - m_new),
                              jnp.zeros((6, 128), F32)], axis=0)
        ab_w = _dot_2x(ab, expand)
        c_full = lax.dot_general((v * ew_w).astype(BF16), kb, (((0,), (0,)), ((), ())), preferred_element_type=F32)
        c_ref[...] = ab_w[0:1] * c_prev + ab_w[1:2] * jnp.where(bd_ref[...] > 0.0, c_full, 0.0)
        n_ref[...] = ab_w[0:1] * n_prev + ab_w[1:2] * jnp.sum(ew_w * k, axis=0, keepdims=True)
        m_ref[...] = m_new
        return carry

    lax.fori_loop(0, ts // cl, chunk, 0, unroll=2)


def _mlstm_consts(cl):
    hid = np.arange(GROUP) // HEAD_DIM
    hm = np.zeros((8, GROUP), np.float32)
    hm[:HEADS] = (hid[None, :] == np.arange(HEADS)[:, None])
    segs = np.zeros((HEADS * cl, 128), np.float32)
    segs[np.arange(HEADS * cl), G_MF + np.arange(HEADS * cl) // cl] = 1.0
    segq = np.zeros((GROUP, 128), np.float32)
    segq[np.arange(GROUP), G_MF + hid] = 1.0
    return jnp.asarray(hm), jnp.asarray(segs, BF16), jnp.asarray(segq, BF16), jnp.asarray(segq.T, BF16)


def _mlstm(proj, gt, l, conv_w, bias_c, bias_r, out_g, tri, bd):
    nb, s, _ = proj.shape
    ts, cl = MLSTM_TS, MLSTM_L
    kern = functools.partial(_mlstm_kernel, ts=ts, cl=cl)
    hm, segs, segq, expand = _mlstm_consts(cl)
    const2 = lambda b, i: (0, 0)
    return pl.pallas_call(
        kern,
        out_shape=jax.ShapeDtypeStruct((nb, s, GROUP), BF16),
        grid=(nb, s // ts),
        in_specs=[pl.BlockSpec((None, ts, 2 * GROUP), lambda b, i: (b, i, C_MQK // (2 * GROUP))),
                  pl.BlockSpec((None, ts, GROUP), lambda b, i: (b, i, C_MV // GROUP)),
                  pl.BlockSpec((None, ts, GROUP), lambda b, i: (b, i, C_MO // GROUP)),
                  pl.BlockSpec((None, ts, 128), lambda b, i: (b, i, C_GATES // 128)),
                  pl.BlockSpec((None, GT_ROWS, ts), lambda b, i: (b, 0, i)),
                  pl.BlockSpec((None, 4, 2 * GROUP), lambda b, i: (l, 0, 0)),
                  pl.BlockSpec((None, 1, 128), lambda b, i: (l, 0, 0)),
                  pl.BlockSpec((None, GT_ROWS, 1), lambda b, i: (l, 0, 0)),
                  pl.BlockSpec((None, 1, GROUP), lambda b, i: (l, 0, 0)),
                  pl.BlockSpec((cl, cl), const2),
                  pl.BlockSpec((8, GROUP), const2),
                  pl.BlockSpec((HEADS * cl, 128), const2),
                  pl.BlockSpec((GROUP, 128), const2),
                  pl.BlockSpec((128, GROUP), const2),
                  pl.BlockSpec((GROUP, GROUP), const2)],
        out_specs=pl.BlockSpec((None, ts, GROUP), lambda b, i: (b, i, 0)),
        scratch_shapes=[pltpu.VMEM((8, 2 * GROUP), F32),
                        pltpu.VMEM((ts, GROUP), F32),
                        pltpu.VMEM((ts, GROUP), F32),
                        pltpu.VMEM((GROUP, GROUP), F32),
                        pltpu.VMEM((1, GROUP), F32),
                        pltpu.VMEM((1, 128), F32)],
        compiler_params=_cp(("arbitrary", "arbitrary")),
        name="mlstm",
    )(proj, proj, proj, proj, gt, conv_w, bias_c, bias_r, out_g, tri, hm, segs, segq, expand, bd)


def _fox_kernel(q_ref, k_ref, v_ref, gr_ref, br_ref, qg_ref, kg_ref, bd_ref, tri_ref, y_ref,
                kn_ref, vx_ref, fn_ref, m_ref, acc_ref, *, t, tq, s_len):
    qi = pl.program_id(1)
    bd = bd_ref[...]

    @pl.when(qi == 0)
    def _():
        kg = kg_ref[...]
        ones = jnp.ones((t, HEAD_DIM), BF16)
        for r in range(s_len // t):
            rows = slice(r * t, (r + 1) * t)
            kn = _rms_heads(k_ref[rows, :], bd, kg)
            vv = v_ref[rows, :]
            for h in range(HEADS):
                sl = slice(h * HEAD_DIM, (h + 1) * HEAD_DIM)
                kn_ref[h, rows, :] = kn[:, sl].astype(BF16)
                vx_ref[h, rows, :] = jnp.concatenate([vv[:, sl].astype(BF16), ones], axis=1)
        carry = jnp.zeros((8, 1), F32)
        for r in range(s_len // t):
            cols = slice(r * t, (r + 1) * t)
            lf = _log_sigmoid(gr_ref[G_FF:G_FF + 8, cols] + br_ref[G_FF:G_FF + 8, :])
            cs = _dot_t_hi(lf, tri_ref[...]) + carry
            fn_ref[:, cols] = -LOG2E * cs
            carry = cs[:, t - 1:t]

    qn = _rms_heads(q_ref[...], bd, qg_ref[...]) * (QK_SCALE * LOG2E)
    qh = [qn[:, h * HEAD_DIM:(h + 1) * HEAD_DIM].astype(BF16) for h in range(HEADS)]
    krel = lax.broadcasted_iota(jnp.int32, (tq, t), 1) - lax.broadcasted_iota(jnp.int32, (tq, t), 0)
    m_ref[...] = jnp.full(m_ref.shape, NEG, F32)
    acc_ref[...] = jnp.zeros(acc_ref.shape, F32)

    hs = range(HEADS)

    def scores(kb, diag=None):
        k0 = pl.multiple_of(kb * t, t)
        sc = [_dot_t(qh[h], kn_ref[h, pl.ds(k0, t), :]) + fn_ref[h:h + 1, pl.ds(k0, t)] for h in hs]
        if diag is None:
            return sc
        causal = krel <= -diag * t
        return [jnp.where(causal, s, NEG) for s in sc]

    def consume(sc, kb):
        k0 = pl.multiple_of(kb * t, t)
        m_old = [m_ref[h] for h in hs]
        m_new = [jnp.maximum(m_old[h], jnp.max(sc[h], axis=1, keepdims=True)) for h in hs]
        p = [jnp.exp2(sc[h] - jnp.concatenate([m_new[h]] * (t // 128), axis=1)).astype(BF16) for h in hs]
        pv = [jnp.dot(p[h], vx_ref[h, pl.ds(k0, t), :], preferred_element_type=F32) for h in hs]
        for h in hs:
            acc_ref[h] = jnp.exp2(m_old[h] - m_new[h]) * acc_ref[h] + pv[h]
            m_ref[h] = m_new[h]

    def body(kp, c):
        consume(scores(2 * kp), 2 * kp)
        consume(scores(2 * kp + 1), 2 * kp + 1)
        return c

    lax.fori_loop(0, qi, body, 0)
    for dg in range(tq // t):
        consume(scores(qi * (tq // t) + dg, dg), qi * (tq // t) + dg)
    outs = []
    for h in range(HEADS):
        a = acc_ref[h]
        outs.append(a[:, :HEAD_DIM] / a[:, HEAD_DIM:])
    y_ref[...] = jnp.concatenate(outs, axis=1).astype(y_ref.dtype)


def _fox(proj, gt, l, bias_r, q_g, k_g, bd, tri):
    nb, s, _ = proj.shape
    t = ATT_T
    tq = FOX_TQ
    kern = functools.partial(_fox_kernel, t=t, tq=tq, s_len=s)
    return pl.pallas_call(
        kern,
        out_shape=jax.ShapeDtypeStruct((nb, s, GROUP), BF16),
        grid=(nb, s // tq),
        in_specs=[pl.BlockSpec((None, tq, GROUP), lambda b, i: (b, i, C_FQ // GROUP)),
                  pl.BlockSpec((None, s, GROUP), lambda b, i: (b, 0, C_FK // GROUP)),
                  pl.BlockSpec((None, s, GROUP), lambda b, i: (b, 0, C_FV // GROUP)),
                  pl.BlockSpec((None, GT_ROWS, s), lambda b, i: (b, 0, 0)),
                  pl.BlockSpec((None, GT_ROWS, 1), lambda b, i: (l, 0, 0)),
                  pl.BlockSpec((None, 1, GROUP), lambda b, i: (l, 0, 0)),
                  pl.BlockSpec((None, 1, GROUP), lambda b, i: (l, 0, 0)),
                  pl.BlockSpec((GROUP, GROUP), lambda b, i: (0, 0)),
                  pl.BlockSpec((t, t), lambda b, i: (0, 0))],
        out_specs=pl.BlockSpec((None, tq, GROUP), lambda b, i: (b, i, 0)),
        scratch_shapes=[pltpu.VMEM((HEADS, s, HEAD_DIM), BF16),
                        pltpu.VMEM((HEADS, s, 2 * HEAD_DIM), BF16),
                        pltpu.VMEM((8, s), F32),
                        pltpu.VMEM((HEADS, tq, 128), F32),
                        pltpu.VMEM((HEADS, tq, 2 * HEAD_DIM), F32)],
        compiler_params=_cp(("arbitrary", "arbitrary")),
        name="fox",
    )(proj, proj, proj, gt, bias_r, q_g, k_g, bd, tri)


def _sb_kernel(q_ref, k_ref, v_ref, sm_ref, y_ref, kb_ref, vb_ref, rest_ref, acc_ref, *, t, tq, s_len):
    qi = pl.program_id(1)

    @pl.when(qi == 0)
    def _():
        for r in range(s_len // t):
            rows = slice(r * t, (r + 1) * t)
            kk = k_ref[rows, :]
            vv = v_ref[rows, :]
            for h in range(HEADS):
                sl = slice(h * HEAD_DIM, (h + 1) * HEAD_DIM)
                kb_ref[h, rows, :] = kk[:, sl].astype(BF16)
                vb_ref[h, rows, :] = vv[:, sl].astype(BF16)

    qs = q_ref[...] * (QK_SCALE * LOG2E)
    qh = [qs[:, h * HEAD_DIM:(h + 1) * HEAD_DIM].astype(BF16) for h in range(HEADS)]
    krel = lax.broadcasted_iota(jnp.int32, (tq, t), 1) - lax.broadcasted_iota(jnp.int32, (tq, t), 0)
    from_here = sm_ref[...]
    rest_ref[...] = jnp.zeros(rest_ref.shape, F32)
    acc_ref[...] = jnp.zeros(acc_ref.shape, F32)

    def step(kb, diag=None):
        k0 = pl.multiple_of(kb * t, t)
        hs = range(HEADS)
        masked = diag is not None
        if masked:
            strict = krel < -diag * t
        z = [_dot_t(qh[h], kb_ref[h, pl.ds(k0, t), :]) for h in hs]
        u = [jnp.maximum(z[h], 0.0) + jnp.log2(1.0 + jnp.exp2(-jnp.abs(z[h]))) for h in hs]
        if masked:
            u = [jnp.where(strict, v, 0.0) for v in u]
        incl = [jnp.dot(u[h].astype(BF16), from_here, preferred_element_type=F32) for h in hs]
        rest = [rest_ref[h] for h in hs]
        a = [jnp.exp2(z[h] - incl[h] - jnp.concatenate([rest[h]] * (t // 128), axis=1)) for h in hs]
        if masked:
            a = [jnp.where(strict, v, 0.0) for v in a]
        av = [jnp.dot(a[h].astype(BF16), vb_ref[h, pl.ds(k0, t), :], preferred_element_type=F32) for h in hs]
        for h in hs:
            acc_ref[h] += av[h]
            rest_ref[h] = rest[h] + incl[h][:, 0:1]

    nd = tq // t
    for dg in reversed(range(nd)):
        step(qi * nd + dg, dg)

    n_un = qi * nd

    if nd % 2 == 1:
        @pl.when(n_un % 2 == 1)
        def _():
            step(n_un - 1)

    def body(j, c):
        kb = 2 * (n_un // 2 - 1 - j)
        step(kb + 1)
        step(kb)
        return c

    lax.fori_loop(0, n_un // 2, body, 0)
    y_ref[...] = jnp.concatenate([acc_ref[h] for h in range(HEADS)], axis=1).astype(y_ref.dtype)


def _sb(proj, after):
    nb, s, _ = proj.shape
    t = ATT_T
    tq = ATT_T
    kern = functools.partial(_sb_kernel, t=t, tq=tq, s_len=s)
    return pl.pallas_call(
        kern,
        out_shape=jax.ShapeDtypeStruct((nb, s, GROUP), BF16),
        grid=(nb, s // tq),
        in_specs=[pl.BlockSpec((None, tq, GROUP), lambda b, i: (b, i, C_SQ // GROUP)),
                  pl.BlockSpec((None, s, GROUP), lambda b, i: (b, 0, C_SK // GROUP)),
                  pl.BlockSpec((None, s, GROUP), lambda b, i: (b, 0, C_SV // GROUP)),
                  pl.BlockSpec((t, t), lambda b, i: (0, 0))],
        out_specs=pl.BlockSpec((None, tq, GROUP), lambda b, i: (b, i, 0)),
        scratch_shapes=[pltpu.VMEM((HEADS, s, HEAD_DIM), BF16),
                        pltpu.VMEM((HEADS, s, HEAD_DIM), BF16),
                        pltpu.VMEM((HEADS, tq, 128), F32),
                        pltpu.VMEM((HEADS, tq, HEAD_DIM), F32)],
        compiler_params=_cp(("arbitrary", "arbitrary")),
        name="stick_breaking",
    )(proj, proj, proj, after)


def _nsa_kernel(q_ref, kvs_ref, kvw_ref, ak_ref, av_ref, gc_ref, gb_ref, qg_ref, kg_ref, pos_ref,
                wk_ref, wv_ref, bd_ref, tc_ref, ovt_ref, tsb_ref, twb_ref, cb_ref, y_ref,
                ksel_ref, vsel_ref, kwin_ref, vwin_ref, kcmp_ref, vcmp_ref, m_ref, acc_ref,
                *, tq, s_len, n_sel):
    qi = pl.program_id(1)
    nc = s_len // CMP_STRIDE

    @pl.when(qi == 0)
    def _():
        zk = jnp.zeros((WINDOW, HEAD_DIM), BF16)
        zv = jnp.zeros((WINDOW, 2 * HEAD_DIM), BF16)
        ksel_ref[0:WINDOW, :] = zk
        vsel_ref[0:WINDOW, :] = zv
        kwin_ref[0:WINDOW, :] = zk
        vwin_ref[0:WINDOW, :] = zv
        ones = jnp.ones((512, HEAD_DIM), BF16)
        for r in range(s_len // 512):
            rows = slice(r * 512, (r + 1) * 512)
            dst = slice(WINDOW + r * 512, WINDOW + (r + 1) * 512)
            kvs = kvs_ref[rows, :]
            kvw = kvw_ref[rows, :]
            ksel_ref[dst, :] = _rms_lanes(kvs[:, :HEAD_DIM], kg_ref[1:2, :]).astype(BF16)
            kwin_ref[dst, :] = _rms_lanes(kvw[:, :HEAD_DIM], kg_ref[2:3, :]).astype(BF16)
            vsel_ref[dst, :] = jnp.concatenate([kvs[:, HEAD_DIM:].astype(BF16), ones], axis=1)
            vwin_ref[dst, :] = jnp.concatenate([kvw[:, HEAD_DIM:].astype(BF16), ones], axis=1)
        half = CMP_STRIDE * HEAD_DIM
        for a_ref, w_ref, dst_ref, norm in ((ak_ref, wk_ref, kcmp_ref, True), (av_ref, wv_ref, vcmp_ref, False)):
            a = a_ref[...]
            pos_b = _dot_hi(pos_ref[...], w_ref[...])[0:1, :]
            p1 = _dot_hi(a, w_ref[0:half, :])
            p2 = _dot_hi(a, w_ref[half:2 * half, :])
            c = p1 + pltpu.roll(p2, nc - 1, 0) + pos_b
            if norm:
                hi, lo = _split2(_rms_lanes(c, kg_ref[0:1, :]))
                kcmp_ref[0] = hi
                kcmp_ref[1] = lo
            else:
                dst_ref[...] = c.astype(BF16)

    t0 = qi * tq
    qn = _rms_heads(q_ref[...], bd_ref[...], qg_ref[...]) * (QK_SCALE * LOG2E)
    qf = [qn[:, h * HEAD_DIM:(h + 1) * HEAD_DIM] for h in range(HEADS)]
    qh = [q.astype(BF16) for q in qf]
    ql = [(qf[h] - qh[h].astype(F32)).astype(BF16) for h in range(HEADS)]

    hs = range(HEADS)
    k_hi, k_lo = kcmp_ref[0], kcmp_ref[1]
    dn = (((1,), (1,)), ((), ()))
    sc = [lax.dot_general(qh[h], k_hi, dn, preferred_element_type=F32)
          + lax.dot_general(ql[h], k_hi, dn, preferred_element_type=F32)
          + lax.dot_general(qh[h], k_lo, dn, preferred_element_type=F32) + tc_ref[h] for h in hs]
    e = [jnp.exp2(sc[h] - jnp.max(sc[h], axis=1, keepdims=True)) for h in hs]
    p = [jnp.where(tc_ref[h] > 0.5 * NEG, e[h] / jnp.sum(e[h], axis=1, keepdims=True), 0.0) for h in hs]
    o_cmp = [jnp.dot(p[h].astype(BF16), vcmp_ref[...], preferred_element_type=F32) for h in hs]
    ps_hi, ps_lo = _split2(p[0] + p[1] + p[2] + p[3])
    ovt = ovt_ref[...].astype(BF16)
    imp_t = (lax.dot_general(ovt, ps_hi, dn, preferred_element_type=F32)
             + lax.dot_general(ovt, ps_lo, dn, preferred_element_type=F32))

    nr = -(-n_sel // 8) * 8
    jj = lax.broadcasted_iota(jnp.int32, (nr, tq), 0)
    tt = t0 + lax.broadcasted_iota(jnp.int32, (nr, tq), 1)
    cur = tt // SEL_BLOCK
    forced = (jj == 0) | (jj == cur) | (jj == cur - 1)
    score = jnp.where(forced, -NEG, jnp.where(jj * SEL_BLOCK <= tt, imp_t[:nr], NEG))
    cnt = jnp.zeros((nr, tq), F32)
    for i in range(n_sel):
        ri = score[i:i + 1, :]
        cnt = cnt + jnp.where(jj > i, (ri >= score).astype(F32), (ri > score).astype(F32))
    sel_t = (cnt < float(min(N_SEL_TOP, n_sel))).astype(F32)
    if nr < 128:
        sel_t = jnp.concatenate([sel_t, jnp.zeros((128 - nr, tq), F32)], axis=0)
    selb = sel_t.T.astype(BF16)

    m_ref[...] = jnp.full(m_ref.shape, NEG, F32)
    acc_ref[...] = jnp.zeros(acc_ref.shape, F32)
    jrow = lax.broadcasted_iota(jnp.int32, (128, tq), 0)
    jcol = lax.broadcasted_iota(jnp.int32, (128, tq), 1) // SEL_BLOCK

    def sel_chunk(kp0, near):
        jb = kp0 // SEL_BLOCK - WINDOW // SEL_BLOCK
        expand = (jrow == jb + jcol).astype(BF16)
        picked = jnp.dot(selb, expand, preferred_element_type=F32) > 0.5
        kblk = ksel_ref[pl.ds(kp0, tq), :]
        vblk = vsel_ref[pl.ds(kp0, tq), :]
        hs = range(HEADS)
        bias = [cb_ref[h] if near is None else tsb_ref[h, :, near * tq:(near + 1) * tq] for h in hs]
        sck = [_dot_t(qh[h], kblk) + jnp.where(picked, bias[h], NEG) for h in hs]
        m_old = [m_ref[h] for h in hs]
        m_new = [jnp.maximum(m_old[h], jnp.max(sck[h], axis=1, keepdims=True)) for h in hs]
        pk = [jnp.exp2(sck[h] - jnp.concatenate([m_new[h]] * (tq // 128), axis=1)).astype(BF16) for h in hs]
        pv = [jnp.dot(pk[h], vblk, preferred_element_type=F32) for h in hs]
        for h in hs:
            acc_ref[h] = jnp.exp2(m_old[h] - m_new[h]) * acc_ref[h] + pv[h]
            m_ref[h] = m_new[h]

    def far(cp, carry):
        sel_chunk(pl.multiple_of(WINDOW + 2 * cp * tq, tq), None)
        sel_chunk(pl.multiple_of(WINDOW + (2 * cp + 1) * tq, tq), None)
        return carry

    n_far = jnp.maximum(qi - WINDOW // tq, 0)
    lax.fori_loop(0, n_far // 2, far, 0)

    @pl.when(n_far % 2 == 1)
    def _():
        sel_chunk(pl.multiple_of(WINDOW + (n_far - 1) * tq, tq), None)
    for r in range(NSA_NEAR // tq):
        @pl.when(qi + r >= WINDOW // tq)
        def _(r=r):
            sel_chunk(pl.multiple_of(t0 + r * tq, tq), r)

    kw = kwin_ref[pl.ds(pl.multiple_of(t0, tq), NSA_NEAR), :]
    vw = vwin_ref[pl.ds(pl.multiple_of(t0, tq), NSA_NEAR), :]
    in_seq = lax.broadcasted_iota(jnp.int32, (tq, NSA_NEAR), 1) >= WINDOW - t0
    g = _sigmoid(gc_ref[...] + gb_ref[...])
    sw = [jnp.where(in_seq, _dot_t(qh[h], kw) + twb_ref[h], NEG) for h in hs]
    pw = [jnp.exp2(sw[h] - jnp.max(sw[h], axis=1, keepdims=True)).astype(BF16) for h in hs]
    ows = [jnp.dot(pw[h], vw, preferred_element_type=F32) for h in hs]
    outs = []
    for h in hs:
        ow = ows[h]
        o_win = ow[:, :HEAD_DIM] / ow[:, HEAD_DIM:]
        a = acc_ref[h]
        o_sel = a[:, :HEAD_DIM] / a[:, HEAD_DIM:]
        c0 = G_NG + 3 * h
        outs.append(g[:, c0:c0 + 1] * o_cmp[h] + g[:, c0 + 1:c0 + 2] * o_sel + g[:, c0 + 2:c0 + 3] * o_win)
    y_ref[...] = jnp.concatenate(outs, axis=1).astype(y_ref.dtype)


def _nsa(proj, a_k, a_v, l, bias_c, q_g, k_g, pos, wk, wv, bd, tc, ovt, tsb, twb, cb):
    nb, s, _ = proj.shape
    tq = NSA_TQ
    nc = s // CMP_STRIDE
    kern = functools.partial(_nsa_kernel, tq=tq, s_len=s, n_sel=s // SEL_BLOCK)
    const2 = lambda b, i: (0, 0)
    const3 = lambda b, i: (0, 0, 0)
    lay3 = lambda b, i: (l, 0, 0)
    return pl.pallas_call(
        kern,
        out_shape=jax.ShapeDtypeStruct((nb, s, GROUP), BF16),
        grid=(nb, s // tq),
        in_specs=[pl.BlockSpec((None, tq, GROUP), lambda b, i: (b, i, C_NQ // GROUP)),
                  pl.BlockSpec((None, s, 128), lambda b, i: (b, 0, C_NKV // 128 + 1)),
                  pl.BlockSpec((None, s, 128), lambda b, i: (b, 0, C_NKV // 128 + 2)),
                  pl.BlockSpec((None, nc, CMP_STRIDE * HEAD_DIM), lambda b, i: (b, 0, 0)),
                  pl.BlockSpec((None, nc, CMP_STRIDE * HEAD_DIM), lambda b, i: (b, 0, 0)),
                  pl.BlockSpec((None, tq, 128), lambda b, i: (b, i, C_GATES // 128)),
                  pl.BlockSpec((None, 1, 128), lay3),
                  pl.BlockSpec((None, 1, GROUP), lay3),
                  pl.BlockSpec((None, 8, HEAD_DIM), lay3),
                  pl.BlockSpec((None, 8, CMP_BLOCK * HEAD_DIM), lay3),
                  pl.BlockSpec((None, CMP_BLOCK * HEAD_DIM, HEAD_DIM), lay3),
                  pl.BlockSpec((None, CMP_BLOCK * HEAD_DIM, HEAD_DIM), lay3),
                  pl.BlockSpec((GROUP, GROUP), const2),
                  pl.BlockSpec((HEADS, tq, nc), lambda b, i: (0, i, 0)),
                  pl.BlockSpec((128, nc), const2),
                  pl.BlockSpec((HEADS, tq, NSA_NEAR), const3),
                  pl.BlockSpec((HEADS, tq, NSA_NEAR), const3),
                  pl.BlockSpec(memory_space=pltpu.SMEM)],
        out_specs=pl.BlockSpec((None, tq, GROUP), lambda b, i: (b, i, 0)),
        scratch_shapes=[pltpu.VMEM((s + WINDOW, HEAD_DIM), BF16),
                        pltpu.VMEM((s + WINDOW, 2 * HEAD_DIM), BF16),
                        pltpu.VMEM((s + WINDOW, HEAD_DIM), BF16),
                        pltpu.VMEM((s + WINDOW, 2 * HEAD_DIM), BF16),
                        pltpu.VMEM((2, nc, HEAD_DIM), BF16),
                        pltpu.VMEM((nc, HEAD_DIM), BF16),
                        pltpu.VMEM((HEADS, tq, 128), F32),
                        pltpu.VMEM((HEADS, tq, 2 * HEAD_DIM), F32)],
        compiler_params=_cp(("arbitrary", "arbitrary")),
        name="nsa",
    )(proj, proj, proj, a_k, a_v, proj, bias_c, q_g, k_g, pos, wk, wv, bd, tc, ovt, tsb, twb, cb)


def _outproj_kernel(x_ref, ya_ref, yb_ref, yc_ref, yd_ref, w_ref, mod_ref, g_ref, xo_ref, h_ref):
    acc = jnp.dot(ya_ref[...], w_ref[0:GROUP, :], preferred_element_type=F32)
    acc += jnp.dot(yb_ref[...], w_ref[GROUP:2 * GROUP, :], preferred_element_type=F32)
    acc += jnp.dot(yc_ref[...], w_ref[2 * GROUP:3 * GROUP, :], preferred_element_type=F32)
    acc += jnp.dot(yd_ref[...], w_ref[3 * GROUP:4 * GROUP, :], preferred_element_type=F32)
    xn = x_ref[...] + mod_ref[2:3, :] * acc
    xo_ref[...] = xn
    h = _rms_lanes(xn, g_ref[...]) * (1.0 + mod_ref[4:5, :]) + mod_ref[3:4, :]
    h_ref[...] = h.astype(h_ref.dtype)


def _outproj(x, ys, mod, l, w_out, norm_g, h_dtype):
    nb, s, d = x.shape
    tm = 512
    yspec = pl.BlockSpec((None, tm, GROUP), lambda b, i: (b, i, 0))
    xspec = pl.BlockSpec((None, tm, d), lambda b, i: (b, i, 0))
    return pl.pallas_call(
        _outproj_kernel,
        out_shape=(jax.ShapeDtypeStruct((nb, s, d), F32), jax.ShapeDtypeStruct((nb, s, d), h_dtype)),
        grid=(nb, s // tm),
        in_specs=[xspec, yspec, yspec, yspec, yspec,
                  pl.BlockSpec((None, d, d), lambda b, i: (l, 0, 0)),
                  pl.BlockSpec((None, None, 6, d), lambda b, i: (l, b, 0, 0)),
                  pl.BlockSpec((None, 1, d), lambda b, i: (l, 0, 0))],
        out_specs=(xspec, xspec),
        compiler_params=_cp(("arbitrary", "arbitrary")),
        name="outproj",
    )(x, *ys, w_out, mod, norm_g)


def _ffn_kernel(h_ref, x_ref, mod_ref, wg_ref, wu_ref, wd_ref, o_ref, acc_ref, *, nf):
    f = pl.program_id(2)

    @pl.when(f == 0)
    def _():
        acc_ref[...] = jnp.zeros_like(acc_ref)

    h = h_ref[...]
    a = jnp.dot(h, wg_ref[...], preferred_element_type=F32)
    u = jnp.dot(h, wu_ref[...], preferred_element_type=F32)
    act = (a * _sigmoid(a) * u).astype(BF16)
    acc_ref[...] += jnp.dot(act, wd_ref[...], preferred_element_type=F32)

    @pl.when(f == nf - 1)
    def _():
        o_ref[...] = x_ref[...] + mod_ref[5:6, :] * acc_ref[...]


def _ffn(h, x, mod, l, li, wg, wu, wd):
    nb, s, d = x.shape
    dff = wg.shape[-1]
    tm, tf = FFN_TM, FFN_TF
    nf = dff // tf
    xspec = pl.BlockSpec((None, tm, d), lambda b, i, f: (b, i, 0))
    return pl.pallas_call(
        functools.partial(_ffn_kernel, nf=nf),
        out_shape=jax.ShapeDtypeStruct((nb, s, d), F32),
        grid=(nb, s // tm, nf),
        in_specs=[xspec, xspec,
                  pl.BlockSpec((None, None, 6, d), lambda b, i, f: (l, b, 0, 0)),
                  pl.BlockSpec((None, d, tf), lambda b, i, f: (li, 0, f)),
                  pl.BlockSpec((None, d, tf), lambda b, i, f: (li, 0, f)),
                  pl.BlockSpec((None, tf, d), lambda b, i, f: (li, f, 0))],
        out_specs=xspec,
        scratch_shapes=[pltpu.VMEM((tm, d), F32)],
        compiler_params=_cp(("arbitrary", "arbitrary", "arbitrary")),
        name="ffn_dense",
    )(h, x, mod, wg, wu, wd)


def _router_kernel(h_ref, w_ref, b_ref, o_ref):
    logits = _dot_hi(h_ref[...], w_ref[...]) + b_ref[...]
    lane = lax.broadcasted_iota(jnp.int32, logits.shape, 1).astype(F32)
    lg = jnp.where(lane < N_EXPERTS, logits, -3e38)
    m1 = jnp.max(lg, axis=1, keepdims=True)
    i1 = jnp.min(jnp.where(lg == m1, lane, 128.0), axis=1, keepdims=True)
    lg2 = jnp.where(lane == i1, -3e38, lg)
    m2 = jnp.max(lg2, axis=1, keepdims=True)
    i2 = jnp.min(jnp.where(lg2 == m2, lane, 128.0), axis=1, keepdims=True)
    e2 = jnp.exp(m2 - m1)
    w1 = 1.0 / (1.0 + e2)
    w2 = e2 / (1.0 + e2)
    o_ref[...] = jnp.where(lane == 0, i1, jnp.where(lane == 1, i2, jnp.where(lane == 2, w1, jnp.where(lane == 3, w2, 0.0))))


def _router(hf, rw, rb):
    t, d = hf.shape
    tm = 1024
    return pl.pallas_call(
        _router_kernel,
        out_shape=jax.ShapeDtypeStruct((t, 128), F32),
        grid=(t // tm,),
        in_specs=[pl.BlockSpec((tm, d), lambda i: (i, 0)),
                  pl.BlockSpec((d, 128), lambda i: (0, 0)),
                  pl.BlockSpec((1, 128), lambda i: (0, 0))],
        out_specs=pl.BlockSpec((tm, 128), lambda i: (i, 0)),
        compiler_params=_cp(("arbitrary",)),
        name="moe_router",
    )(hf, rw, rb)


def _gather_copy(src_hbm, row, dst_buf, slot, r, sem):
    return pltpu.make_async_copy(src_hbm.at[pl.ds(row, 1)], dst_buf.at[slot, pl.ds(r, 1)], sem.at[slot])


def _experts_kernel(be_ref, tok_ref, nv_ref, h_hbm, wg_ref, wu_ref, wd_ref, y_ref,
                    xbuf, xb_ref, acc_ref, sem, *, bm, nf):
    del be_ref
    i = pl.program_id(0)
    f = pl.program_id(1)
    nvalid = nv_ref[0]
    slot = i % 2
    chunk = bm // nf

    def for_rows(fn):
        def body(r, c):
            fn(r)
            return c
        lax.fori_loop(0, bm, body, 0, unroll=8)

    @pl.when((f == 0) & (i == 0))
    def _():
        for_rows(lambda r: _gather_copy(h_hbm, tok_ref[r], xbuf, 0, r, sem).start())

    @pl.when((f == 0) & (i < nvalid))
    def _():
        for_rows(lambda r: _gather_copy(h_hbm, 0, xbuf, slot, r, sem).wait())
        xb_ref[...] = xbuf[slot].astype(BF16)
        acc_ref[...] = jnp.zeros_like(acc_ref)

    def compute(prefetch):
        third = -(-chunk // 3)

        def issue_rows(lo, hi):
            if prefetch:
                for j in range(lo, min(hi, chunk)):
                    r = f * chunk + j
                    _gather_copy(h_hbm, tok_ref[(i + 1) * bm + r], xbuf, 1 - slot, r, sem).start()

        x = xb_ref[...]
        issue_rows(0, third)
        a = jnp.dot(x, wg_ref[...], preferred_element_type=F32)
        issue_rows(third, 2 * third)
        u = jnp.dot(x, wu_ref[...], preferred_element_type=F32)
        issue_rows(2 * third, 3 * third)
        act = (a * _sigmoid(a) * u).astype(BF16)
        acc_ref[...] += jnp.dot(act, wd_ref[...], preferred_element_type=F32)

    @pl.when(i + 1 < nvalid)
    def _():
        compute(True)

    @pl.when(i + 1 == nvalid)
    def _():
        compute(False)

    @pl.when(f == nf - 1)
    def _():
        @pl.when(i < nvalid)
        def _():
            y_ref[...] = acc_ref[...]

        @pl.when(i >= nvalid)
        def _():
            y_ref[...] = jnp.zeros_like(y_ref)


def _experts(hf, blk_expert, row_tok, nvalid, wg, wu, wd, li):
    t, d = hf.shape
    bm, tf = MOE_BM, MOE_TF
    n_rows = row_tok.shape[0]
    n_blocks = n_rows // bm
    dff = wg.shape[-1]
    nf = dff // tf

    def fidx(i, f, nv):
        return jnp.where(i < nv[0], f, nf - 1)

    grid_spec = pltpu.PrefetchScalarGridSpec(
        num_scalar_prefetch=3,
        grid=(n_blocks, nf),
        in_specs=[pl.BlockSpec(memory_space=pl.ANY),
                  pl.BlockSpec((None, None, d, tf), lambda i, f, be, tok, nv: (li, be[i], 0, fidx(i, f, nv))),
                  pl.BlockSpec((None, None, d, tf), lambda i, f, be, tok, nv: (li, be[i], 0, fidx(i, f, nv))),
                  pl.BlockSpec((None, None, tf, d), lambda i, f, be, tok, nv: (li, be[i], fidx(i, f, nv), 0))],
        out_specs=pl.BlockSpec((bm, d), lambda i, f, be, tok, nv: (i, 0)),
        scratch_shapes=[pltpu.VMEM((2, bm, d), F32),
                        pltpu.VMEM((bm, d), BF16),
                        pltpu.VMEM((bm, d), F32),
                        pltpu.SemaphoreType.DMA((2,))],
    )
    return pl.pallas_call(
        functools.partial(_experts_kernel, bm=bm, nf=nf),
        out_shape=jax.ShapeDtypeStruct((n_rows, d), F32),
        grid_spec=grid_spec,
        compiler_params=_cp(("arbitrary", "arbitrary")),
        name="moe_experts",
    )(blk_expert, row_tok, nvalid, hf, wg, wu, wd)


def _combine_kernel(pos_ref, y_hbm, x_ref, mod_ref, rt_ref, o_ref, ybuf, sem, *, tm, s_len):
    b = pl.program_id(0)
    i = pl.program_id(1)
    base = (b * s_len + i * tm) * 2

    def issue(r, c):
        _gather_copy(y_hbm, pos_ref[base + 2 * r], ybuf, 0, r, sem).start()
        _gather_copy(y_hbm, pos_ref[base + 2 * r + 1], ybuf, 1, r, sem).start()
        return c

    def wait(r, c):
        _gather_copy(y_hbm, 0, ybuf, 0, r, sem).wait()
        _gather_copy(y_hbm, 0, ybuf, 1, r, sem).wait()
        return c

    lax.fori_loop(0, tm, issue, 0, unroll=8)
    lax.fori_loop(0, tm, wait, 0, unroll=8)
    rt = rt_ref[...]
    o_ref[...] = x_ref[...] + mod_ref[5:6, :] * (rt[:, 2:3] * ybuf[0] + rt[:, 3:4] * ybuf[1])


def _combine(y, pos, x, mod, route, l):
    nb, s, d = x.shape
    tm = 256
    grid_spec = pltpu.PrefetchScalarGridSpec(
        num_scalar_prefetch=1,
        grid=(nb, s // tm),
        in_specs=[pl.BlockSpec(memory_space=pl.ANY),
                  pl.BlockSpec((None, tm, d), lambda b, i, p: (b, i, 0)),
                  pl.BlockSpec((None, None, 6, d), lambda b, i, p: (l, b, 0, 0)),
                  pl.BlockSpec((None, tm, 128), lambda b, i, p: (b, i, 0))],
        out_specs=pl.BlockSpec((None, tm, d), lambda b, i, p: (b, i, 0)),
        scratch_shapes=[pltpu.VMEM((2, tm, d), F32), pltpu.SemaphoreType.DMA((2,))],
    )
    return pl.pallas_call(
        functools.partial(_combine_kernel, tm=tm, s_len=s),
        out_shape=jax.ShapeDtypeStruct((nb, s, d), F32),
        grid_spec=grid_spec,
        compiler_params=_cp(("arbitrary", "arbitrary")),
        name="moe_combine",
    )(pos, y, x, mod, route.reshape(nb, s, 128))


def _moe(hf32, x, mod, l, li, router_w, router_b, wg, wu, wd):
    nb, s, d = x.shape
    t = nb * s
    bm = MOE_BM
    hf = hf32.reshape(t, d)
    rw = jnp.zeros((d, 128), F32).at[:, :N_EXPERTS].set(router_w[li])
    rb = jnp.zeros((1, 128), F32).at[0, :N_EXPERTS].set(router_b[li])
    route = _router(hf, rw, rb)
    e_flat = route[:, 0:2].astype(jnp.int32).reshape(-1)
    n_assign = 2 * t
    onehot = (e_flat[:, None] == jnp.arange(N_EXPERTS, dtype=jnp.int32)[None, :]).astype(jnp.int32)
    csum = jnp.cumsum(onehot, axis=0)
    rank = jnp.sum(onehot * csum, axis=1) - 1
    counts = csum[-1]
    padded = (counts + bm - 1) // bm * bm
    cum_padded = jnp.cumsum(padded)
    pstart = cum_padded - padded
    dest = pstart[e_flat] + rank
    n_blocks = n_assign // bm + N_EXPERTS
    n_rows = n_blocks * bm
    tok_flat = jnp.arange(n_assign, dtype=jnp.int32) // 2
    row_tok = jnp.zeros((n_rows,), jnp.int32).at[dest].set(tok_flat)
    blk_start = jnp.arange(n_blocks, dtype=jnp.int32) * bm
    blk_expert = jnp.minimum(jnp.searchsorted(cum_padded, blk_start, side='right'), N_EXPERTS - 1).astype(jnp.int32)
    nvalid = (cum_padded[-1] // bm).astype(jnp.int32).reshape(1)
    y = _experts(hf, blk_expert, row_tok, nvalid, wg, wu, wd, li)
    return _combine(y, dest.astype(jnp.int32), x, mod, route, l)


def _t5_bucket(dist):
    n = np.maximum(dist, 0)
    max_exact = NUM_BUCKETS // 2
    large = max_exact + (np.log(np.maximum(n, 1).astype(np.float32) / max_exact)
                         / math.log(MAX_DISTANCE / max_exact) * (NUM_BUCKETS - max_exact)).astype(np.int32)
    return np.where(n < max_exact, n, np.minimum(large, NUM_BUCKETS - 1)).astype(np.int32)


def _bias_by_bucket(rb, bucket):
    ids = jnp.asarray(bucket.astype(np.int8))[None]
    out = jnp.zeros((rb.shape[1],) + bucket.shape, F32)
    for k in range(NUM_BUCKETS):
        out = jnp.where(ids == k, rb[k][:, None, None], out)
    return out


def _nsa_tables(rel_bias, s):
    tq = NSA_TQ
    nc = s // CMP_STRIDE
    n_cmp = (s - CMP_BLOCK) // CMP_STRIDE + 1
    rb = rel_bias.astype(F32) * LOG2E
    t = np.arange(s)[:, None]
    cmp_end = np.arange(nc)[None, :] * CMP_STRIDE + CMP_BLOCK - 1
    ok = (cmp_end <= t) & (np.arange(nc)[None, :] < n_cmp)
    tc = jnp.where(jnp.asarray(ok)[None], _bias_by_bucket(rb, _t5_bucket(t - cmp_end)), NEG)
    dist = np.arange(tq)[:, None] - (np.arange(NSA_NEAR)[None, :] - WINDOW)
    tb = _bias_by_bucket(rb, _t5_bucket(dist))
    tsb = jnp.where(jnp.asarray(dist >= 0)[None], tb, NEG)
    twb = jnp.where(jnp.asarray((dist >= 0) & (dist < WINDOW))[None], tb, NEG)
    far_bucket = int(_t5_bucket(np.array([WINDOW]))[0])
    cb = rb[far_bucket]
    n_sel = s // SEL_BLOCK
    cs = np.arange(nc)[None, :] * CMP_STRIDE
    ss = np.arange(128)[:, None] * SEL_BLOCK
    ov = np.clip(np.minimum(cs + CMP_BLOCK, ss + SEL_BLOCK) - np.maximum(cs, ss), 0, None).astype(np.float32) / CMP_BLOCK
    ov = ov * (np.arange(128)[:, None] < n_sel) * (np.arange(nc)[None, :] < n_cmp)
    return tc, jnp.asarray(ov, F32), tsb, twb, cb


def _prep_layer_params(w_in, mlstm_gate_b, fox_f_b, nsa_gate_b):
    depth, d, _ = w_in.shape
    small = [w_in[:, :, 1024:1032], w_in[:, :, 1800:1804], w_in[:, :, 3212:3224]]
    w_p = jnp.concatenate([w_in[:, :, 0:1024], w_in[:, :, 1032:1800], w_in[:, :, 1804:2572], w_in[:, :, 2572:3212]]
                          + small + [jnp.zeros((depth, d, 128 - 24), w_in.dtype)], axis=-1).astype(BF16)
    wgt = jnp.concatenate(small + [jnp.zeros((depth, d, GT_ROWS - 24), w_in.dtype)], axis=-1)
    wgt = jnp.transpose(wgt, (0, 2, 1)).astype(BF16)
    gate_b = jnp.concatenate([mlstm_gate_b, fox_f_b, nsa_gate_b], axis=-1).astype(F32)
    bias_c = jnp.zeros((depth, 1, 128), F32).at[:, 0, :24].set(gate_b)
    bias_r = jnp.zeros((depth, GT_ROWS, 1), F32).at[:, :24, 0].set(gate_b)
    return w_p, wgt, bias_c, bias_r


def kernel(x, c, rel_bias, ada_w, ada_b, norm1_g, norm2_g, w_in, w_out, mlstm_conv_w, mlstm_gate_b, mlstm_out_g,
           fox_f_b, fox_q_g, fox_k_g, nsa_q_g, nsa_k_g, nsa_cmp_pos, nsa_cmp_wk, nsa_cmp_wv, nsa_gate_b,
           ffn_wg, ffn_wu, ffn_wd, moe_router_w, moe_router_b, moe_wg, moe_wu, moe_wd):
    nb, s, d = x.shape
    depth = w_in.shape[0]
    assert d == D_MODEL and s % 512 == 0 and s // SEL_BLOCK <= 128

    w_p, wgt, bias_c, bias_r = _prep_layer_params(w_in, mlstm_gate_b, fox_f_b, nsa_gate_b)
    w_out_b = w_out.astype(BF16)
    n1g = norm1_g.reshape(depth, 1, d)
    n2g = norm2_g.reshape(depth, 1, d)
    m_out_g = mlstm_out_g.reshape(depth, 1, GROUP)
    fq_g = jnp.tile(fox_q_g, (1, HEADS)).reshape(depth, 1, GROUP)
    fk_g = jnp.tile(fox_k_g, (1, HEADS)).reshape(depth, 1, GROUP)
    nq_g = jnp.tile(nsa_q_g, (1, HEADS)).reshape(depth, 1, GROUP)
    nk_g = jnp.zeros((depth, 8, HEAD_DIM), F32).at[:, :3].set(nsa_k_g)
    pos8 = jnp.zeros((depth, 8, CMP_BLOCK * HEAD_DIM), F32).at[:, 0].set(nsa_cmp_pos.reshape(depth, -1))
    ffn_wg_b, ffn_wu_b, ffn_wd_b = ffn_wg.astype(BF16), ffn_wu.astype(BF16), ffn_wd.astype(BF16)
    moe_wg_b, moe_wu_b, moe_wd_b = moe_wg.astype(BF16), moe_wu.astype(BF16), moe_wd.astype(BF16)
    hid = np.arange(GROUP) // HEAD_DIM
    bd = jnp.asarray((hid[:, None] == hid[None, :]).astype(np.float32) / HEAD_DIM)
    tri_l = jnp.asarray(np.tril(np.ones((MLSTM_L, MLSTM_L), np.float32)))
    tri_t = jnp.asarray(np.tril(np.ones((ATT_T, ATT_T), np.float32)))
    from_here = jnp.asarray(np.tril(np.ones((ATT_T, ATT_T), np.float32)), BF16)
    tc, ovt, tsb, twb, cb = _nsa_tables(rel_bias, s)

    mod = _adaln(c, ada_w, ada_b)
    for l in range(depth):
        proj, gt = _inproj(x, mod, l, n1g, w_p, wgt)
        y_a = _mlstm(proj, gt, l, mlstm_conv_w, bias_c, bias_r, m_out_g, tri_l, bd)
        y_b = _fox(proj, gt, l, bias_r, fq_g, fk_g, bd, tri_t)
        y_c = _sb(proj, from_here)
        a_k = proj[:, :, C_NKV:C_NKV + HEAD_DIM].reshape(nb, s // CMP_STRIDE, CMP_STRIDE * HEAD_DIM)
        a_v = proj[:, :, C_NKV + HEAD_DIM:C_NKV + 2 * HEAD_DIM].reshape(nb, s // CMP_STRIDE, CMP_STRIDE * HEAD_DIM)
        y_d = _nsa(proj, a_k, a_v, l, bias_c, nq_g, nk_g, pos8, nsa_cmp_wk, nsa_cmp_wv, bd, tc, ovt, tsb, twb, cb)
        if l % 2 == 0:
            x, h2 = _outproj(x, (y_a, y_b, y_c, y_d), mod, l, w_out_b, n2g, BF16)
            x = _ffn(h2, x, mod, l, l // 2, ffn_wg_b, ffn_wu_b, ffn_wd_b)
        else:
            x, h2 = _outproj(x, (y_a, y_b, y_c, y_d), mod, l, w_out_b, n2g, F32)
            x = _moe(h2, x, mod, l, l // 2, moe_router_w, moe_router_b, moe_wg_b, moe_wu_b, moe_wd_b)
    return x
```

```python
import functools
import math

import numpy as np
import jax
import jax.numpy as jnp
from jax import lax
from jax.experimental import pallas as pl
from jax.experimental.pallas import tpu as pltpu

F32 = jnp.float32
BF16 = jnp.bfloat16
HI = lax.Precision.HIGHEST

D_MODEL = 1024
HEADS = 4
HEAD_DIM = 64
GROUP = HEADS * HEAD_DIM
NORM_EPS = 1e-6
NEG = -1e30
QK_SCALE = HEAD_DIM ** -0.5
LOG2E = 1.4426950408889634
CMP_BLOCK = 32
CMP_STRIDE = 16
SEL_BLOCK = 64
N_SEL_TOP = 16
WINDOW = 512
NUM_BUCKETS = 32
MAX_DISTANCE = 128
N_EXPERTS = 8
VMEM_LIMIT = 56 * 1024 * 1024

PW = 3328
C_MQK, C_MV, C_MO = 0, 512, 768
C_FQ, C_FK, C_FV = 1024, 1280, 1536
C_SQ, C_SK, C_SV = 1792, 2048, 2304
C_NQ, C_NKV, C_GATES = 2560, 2816, 3200
G_MI, G_MF, G_FF, G_NG = 0, 4, 8, 12
GT_ROWS = 32

MLSTM_TS = 512
MLSTM_L = 128
ATT_T = 256
FOX_TQ = 512
FOX_HEAD_GROUP = 4
NSA_TQ = 256
NSA_NEAR = WINDOW + NSA_TQ
MOE_BM = 512
MOE_TF = 3584
FFN_TM = 512
FFN_TF = 2816


def _cp(sem, vmem=VMEM_LIMIT):
    return pltpu.CompilerParams(dimension_semantics=sem, vmem_limit_bytes=vmem)


def _dot(a, b):
    return jnp.dot(a.astype(BF16), b.astype(BF16), preferred_element_type=F32)


def _dot_t(a, b):
    return lax.dot_general(a.astype(BF16), b.astype(BF16), (((1,), (1,)), ((), ())), preferred_element_type=F32)


def _dot_hi(a, b):
    return jnp.dot(a, b, precision=HI, preferred_element_type=F32)


def _dot_t_hi(a, b):
    return lax.dot_general(a, b, (((1,), (1,)), ((), ())), precision=HI, preferred_element_type=F32)


def _sigmoid(x):
    return 1.0 / (1.0 + jnp.exp(-x))


def _log_sigmoid(x):
    return jnp.minimum(x, 0.0) - jnp.log1p(jnp.exp(-jnp.abs(x)))


def _rms_lanes(x, g):
    return x * lax.rsqrt(jnp.mean(x * x, axis=-1, keepdims=True) + NORM_EPS) * g


def _split2(a):
    hi = a.astype(BF16)
    return hi, (a - hi.astype(F32)).astype(BF16)


def _dot_2x(a, b):
    hi, lo = _split2(a)
    bb = b.astype(BF16)
    return jnp.dot(hi, bb, preferred_element_type=F32) + jnp.dot(lo, bb, preferred_element_type=F32)


def _dot_t_2x(a, b):
    hi, lo = _split2(a)
    bb = b.astype(BF16)
    dn = (((1,), (1,)), ((), ()))
    return (lax.dot_general(hi, bb, dn, preferred_element_type=F32)
            + lax.dot_general(lo, bb, dn, preferred_element_type=F32))


def _rms_heads(x, bd, g):
    return x * lax.rsqrt(_dot_2x(x * x, bd) + NORM_EPS) * g


def _adaln_kernel(c_ref, w_ref, b_ref, o_ref):
    c = c_ref[...]
    o_ref[...] = _dot_hi(c * _sigmoid(c), w_ref[...]) + b_ref[...]


def _adaln(c, ada_w, ada_b):
    depth, d, six_d = ada_w.shape
    nb = c.shape[0]
    out = pl.pallas_call(
        _adaln_kernel,
        out_shape=jax.ShapeDtypeStruct((depth, nb, six_d), F32),
        grid=(depth, six_d // d),
        in_specs=[pl.BlockSpec((nb, d), lambda l, j: (0, 0)),
                  pl.BlockSpec((None, d, d), lambda l, j: (l, 0, j)),
                  pl.BlockSpec((None, 1, d), lambda l, j: (l, 0, j))],
        out_specs=pl.BlockSpec((None, nb, d), lambda l, j: (l, 0, j)),
        compiler_params=_cp(("arbitrary", "arbitrary")),
        name="adaln",
    )(c, ada_w, ada_b.reshape(depth, 1, six_d))
    return out.reshape(depth, nb, 6, d)


def _inproj_kernel(x_ref, mod_ref, g_ref, w_ref, wgt_ref, o_ref, gt_ref):
    h = _rms_lanes(x_ref[...], g_ref[...]) * (1.0 + mod_ref[1:2, :]) + mod_ref[0:1, :]
    hb = h.astype(BF16)
    o_ref[...] = jnp.dot(hb, w_ref[...], preferred_element_type=F32)
    gt_ref[...] = lax.dot_general(wgt_ref[...], hb, (((1,), (1,)), ((), ())), preferred_element_type=F32)


def _inproj(x, mod, l, norm_g, w_p, wgt):
    nb, s, d = x.shape
    tm = 256
    return pl.pallas_call(
        _inproj_kernel,
        out_shape=(jax.ShapeDtypeStruct((nb, s, PW), F32), jax.ShapeDtypeStruct((nb, GT_ROWS, s), F32)),
        grid=(nb, s // tm),
        in_specs=[pl.BlockSpec((None, tm, d), lambda b, i: (b, i, 0)),
                  pl.BlockSpec((None, None, 6, d), lambda b, i: (l, b, 0, 0)),
                  pl.BlockSpec((None, 1, d), lambda b, i: (l, 0, 0)),
                  pl.BlockSpec((None, d, PW), lambda b, i: (l, 0, 0)),
                  pl.BlockSpec((None, GT_ROWS, d), lambda b, i: (l, 0, 0))],
        out_specs=(pl.BlockSpec((None, tm, PW), lambda b, i: (b, i, 0)),
                   pl.BlockSpec((None, GT_ROWS, tm), lambda b, i: (b, 0, i))),
        compiler_params=_cp(("arbitrary", "arbitrary")),
        name="inproj",
    )(x, mod, norm_g, w_p, wgt)


def _mlstm_kernel(qk_ref, v_ref, op_ref, gc_ref, gr_ref, cw_ref, bc_ref, br_ref, og_ref, tri_ref,
                  hm_ref, segs_ref, segq_ref, exp_ref, bd_ref, y_ref,
                  prev_ref, qs_ref, ks_ref, c_ref, n_ref, m_ref, *, ts, cl):
    @pl.when(pl.program_id(1) == 0)
    def _():
        prev_ref[...] = jnp.zeros_like(prev_ref)
        c_ref[...] = jnp.zeros_like(c_ref)
        n_ref[...] = jnp.zeros_like(n_ref)
        m_ref[...] = jnp.zeros_like(m_ref)

    x = qk_ref[...]
    xc = jnp.concatenate([prev_ref[...], x], axis=0)
    cw = cw_ref[...]
    y = (cw[0:1] * xc[8:8 + ts] + cw[1:2] * xc[7:7 + ts] + cw[2:3] * xc[6:6 + ts] + cw[3:4] * xc[5:5 + ts])
    prev_ref[...] = x[ts - 8:ts]
    y = y * _sigmoid(y)
    qs_ref[...] = y[:, :GROUP]
    ks_ref[...] = y[:, GROUP:] * QK_SCALE

    tri = tri_ref[...]
    lower = lax.broadcasted_iota(jnp.int32, (cl, cl), 0) >= lax.broadcasted_iota(jnp.int32, (cl, cl), 1)

    def chunk(ci, carry):
        r0 = pl.multiple_of(ci * cl, cl)
        gc = gc_ref[pl.ds(r0, cl), :] + bc_ref[...]
        gr = gr_ref[0:8, pl.ds(r0, cl)] + br_ref[0:8, :]
        lf_hi, lf_lo = _split2(_log_sigmoid(gc))
        trib = tri.astype(BF16)
        b_c = (jnp.dot(trib, lf_hi, preferred_element_type=F32)
               + jnp.dot(trib, lf_lo, preferred_element_type=F32))
        b_r = _dot_t_2x(_log_sigmoid(gr), tri)
        q = qs_ref[pl.ds(r0, cl), :]
        k = ks_ref[pl.ds(r0, cl), :]
        v = v_ref[pl.ds(r0, cl), :]
        kb = k.astype(BF16)
        hs = range(HEADS)
        ig_c = pltpu.roll(gc, G_MF - G_MI, 1)
        ig_r = pltpu.roll(gr, G_MF - G_MI, 0)
        u_c = ig_c - b_c
        u_r = ig_r - b_r
        g_row = b_c[cl - 1:cl, :]
        w_c = g_row + u_c
        m_loc = jnp.max(w_c, axis=0, keepdims=True)
        ew_c = jnp.exp(w_c - m_loc)
        m_prev = m_ref[...]
        pm = u_c
        row = lax.broadcasted_iota(jnp.int32, (cl, 128), 0)
        sh = 1
        while sh < cl:
            pm = jnp.where(row >= sh, jnp.maximum(pm, pltpu.roll(pm, sh, 0)), pm)
            sh *= 2
        mm_c = jnp.maximum(pm, m_prev)
        iw_c = jnp.exp(m_prev - mm_c)
        em_c = jnp.exp(-(b_c + mm_c))
        sb = []
        for h in hs:
            qm = jnp.where(hm_ref[h:h + 1, :] > 0.5, q, 0.0).astype(BF16)
            arg = jnp.where(lower, u_r[G_MF + h:G_MF + h + 1, :] - mm_c[:, G_MF + h:G_MF + h + 1], NEG)
            sb.append((_dot_t(qm, kb) * jnp.exp(arg)).astype(BF16))
        den_s = jnp.dot(jnp.concatenate(sb, axis=1), segs_ref[...], preferred_element_type=F32)
        pv = None
        for h in hs:
            vm = jnp.where(hm_ref[h:h + 1, :] > 0.5, v, 0.0).astype(BF16)
            t_h = jnp.dot(sb[h], vm, preferred_element_type=F32)
            pv = t_h if pv is None else pv + t_h
        c_prev = c_ref[...]
        n_prev = n_ref[...]
        inter = _dot_t(q, c_prev)
        qn_c = _dot_2x(q * n_prev, segq_ref[...])
        den_c = den_s + iw_c * qn_c
        lane = lax.broadcasted_iota(jnp.int32, (cl, 128), 1)
        r_c = jnp.where((lane >= G_MF) & (lane < G_MF + HEADS), 1.0 / jnp.maximum(jnp.abs(den_c), em_c), 0.0)
        expand = exp_ref[...]
        wide = _dot_2x(jnp.concatenate([ew_c, iw_c, r_c], axis=0), expand)
        ew_w, iw_w, r_w = wide[0:cl], wide[cl:2 * cl], wide[2 * cl:3 * cl]
        hh = (pv + iw_w * inter) * r_w
        hn = _rms_heads(hh, bd_ref[...], og_ref[...])
        y_ref[pl.ds(r0, cl), :] = (hn * _sigmoid(op_ref[pl.ds(r0, cl), :])).astype(y_ref.dtype)
        m_new = jnp.maximum(g_row + m_prev, m_loc)
        ab = jnp.concatenate([jnp.exp(g_row + m_prev - m_new), jnp.exp(m_loc - m_new),
                              jnp.zeros((6, 128), F32)], axis=0)
        ab_w = _dot_2x(ab, expand)
        c_full = lax.dot_general((v * ew_w).astype(BF16), kb, (((0,), (0,)), ((), ())), preferred_element_type=F32)
        c_ref[...] = ab_w[0:1] * c_prev + ab_w[1:2] * jnp.where(bd_ref[...] > 0.0, c_full, 0.0)
        n_ref[...] = ab_w[0:1] * n_prev + ab_w[1:2] * jnp.sum(ew_w * k, axis=0, keepdims=True)
        m_ref[...] = m_new
        return carry

    lax.fori_loop(0, ts // cl, chunk, 0, unroll=2)


def _mlstm_consts(cl):
    hid = np.arange(GROUP) // HEAD_DIM
    hm = np.zeros((8, GROUP), np.float32)
    hm[:HEADS] = (hid[None, :] == np.arange(HEADS)[:, None])
    segs = np.zeros((HEADS * cl, 128), np.float32)
    segs[np.arange(HEADS * cl), G_MF + np.arange(HEADS * cl) // cl] = 1.0
    segq = np.zeros((GROUP, 128), np.float32)
    segq[np.arange(GROUP), G_MF + hid] = 1.0
    return jnp.asarray(hm), jnp.asarray(segs, BF16), jnp.asarray(segq, BF16), jnp.asarray(segq.T, BF16)


def _mlstm(proj, gt, l, conv_w, bias_c, bias_r, out_g, tri, bd):
    nb, s, _ = proj.shape
    ts, cl = MLSTM_TS, MLSTM_L
    kern = functools.partial(_mlstm_kernel, ts=ts, cl=cl)
    hm, segs, segq, expand = _mlstm_consts(cl)
    const2 = lambda b, i: (0, 0)
    return pl.pallas_call(
        kern,
        out_shape=jax.ShapeDtypeStruct((nb, s, GROUP), BF16),
        grid=(nb, s // ts),
        in_specs=[pl.BlockSpec((None, ts, 2 * GROUP), lambda b, i: (b, i, C_MQK // (2 * GROUP))),
                  pl.BlockSpec((None, ts, GROUP), lambda b, i: (b, i, C_MV // GROUP)),
                  pl.BlockSpec((None, ts, GROUP), lambda b, i: (b, i, C_MO // GROUP)),
                  pl.BlockSpec((None, ts, 128), lambda b, i: (b, i, C_GATES // 128)),
                  pl.BlockSpec((None, GT_ROWS, ts), lambda b, i: (b, 0, i)),
                  pl.BlockSpec((None, 4, 2 * GROUP), lambda b, i: (l, 0, 0)),
                  pl.BlockSpec((None, 1, 128), lambda b, i: (l, 0, 0)),
                  pl.BlockSpec((None, GT_ROWS, 1), lambda b, i: (l, 0, 0)),
                  pl.BlockSpec((None, 1, GROUP), lambda b, i: (l, 0, 0)),
                  pl.BlockSpec((cl, cl), const2),
                  pl.BlockSpec((8, GROUP), const2),
                  pl.BlockSpec((HEADS * cl, 128), const2),
                  pl.BlockSpec((GROUP, 128), const2),
                  pl.BlockSpec((128, GROUP), const2),
                  pl.BlockSpec((GROUP, GROUP), const2)],
        out_specs=pl.BlockSpec((None, ts, GROUP), lambda b, i: (b, i, 0)),
        scratch_shapes=[pltpu.VMEM((8, 2 * GROUP), F32),
                        pltpu.VMEM((ts, GROUP), F32),
                        pltpu.VMEM((ts, GROUP), F32),
                        pltpu.VMEM((GROUP, GROUP), F32),
                        pltpu.VMEM((1, GROUP), F32),
                        pltpu.VMEM((1, 128), F32)],
        compiler_params=_cp(("arbitrary", "arbitrary")),
        name="mlstm",
    )(proj, proj, proj, proj, gt, conv_w, bias_c, bias_r, out_g, tri, hm, segs, segq, expand, bd)


def _fox_kernel(q_ref, k_ref, v_ref, gr_ref, br_ref, qg_ref, kg_ref, bd_ref, tri_ref, y_ref,
                kn_ref, vx_ref, fn_ref, m_ref, acc_ref, *, t, tq, s_len):
    qi = pl.program_id(1)
    bd = bd_ref[...]

    @pl.when(qi == 0)
    def _():
        kg = kg_ref[...]
        ones = jnp.ones((t, HEAD_DIM), BF16)
        for r in range(s_len // t):
            rows = slice(r * t, (r + 1) * t)
            kn = _rms_heads(k_ref[rows, :], bd, kg)
            vv = v_ref[rows, :]
            for h in range(HEADS):
                sl = slice(h * HEAD_DIM, (h + 1) * HEAD_DIM)
                kn_ref[h, rows, :] = kn[:, sl].astype(BF16)
                vx_ref[h, rows, :] = jnp.concatenate([vv[:, sl].astype(BF16), ones], axis=1)
        carry = jnp.zeros((8, 1), F32)
        for r in range(s_len // t):
            cols = slice(r * t, (r + 1) * t)
            lf = _log_sigmoid(gr_ref[G_FF:G_FF + 8, cols] + br_ref[G_FF:G_FF + 8, :])
            cs = _dot_t_hi(lf, tri_ref[...]) + carry
            fn_ref[:, cols] = -LOG2E * cs
            carry = cs[:, t - 1:t]

    qn = _rms_heads(q_ref[...], bd, qg_ref[...]) * (QK_SCALE * LOG2E)
    qh = [qn[:, h * HEAD_DIM:(h + 1) * HEAD_DIM].astype(BF16) for h in range(HEADS)]
    krel = lax.broadcasted_iota(jnp.int32, (tq, t), 1) - lax.broadcasted_iota(jnp.int32, (tq, t), 0)
    m_ref[...] = jnp.full(m_ref.shape, NEG, F32)
    acc_ref[...] = jnp.zeros(acc_ref.shape, F32)

    def tile(kb, hs, diag=None):
        k0 = pl.multiple_of(kb * t, t)
        sc = {h: _dot_t(qh[h], kn_ref[h, pl.ds(k0, t), :]) + fn_ref[h:h + 1, pl.ds(k0, t)] for h in hs}
        if diag is not None:
            causal = krel <= -diag * t
            sc = {h: jnp.where(causal, sc[h], NEG) for h in hs}
        m_old = {h: m_ref[h] for h in hs}
        m_new = {h: jnp.maximum(m_old[h], jnp.max(sc[h], axis=1, keepdims=True)) for h in hs}
        p = {h: jnp.exp2(sc[h] - jnp.concatenate([m_new[h]] * (t // 128), axis=1)).astype(BF16) for h in hs}
        pv = {h: jnp.dot(p[h], vx_ref[h, pl.ds(k0, t), :], preferred_element_type=F32) for h in hs}
        for h in hs:
            acc_ref[h] = jnp.exp2(m_old[h] - m_new[h]) * acc_ref[h] + pv[h]
            m_ref[h] = m_new[h]

    groups = [tuple(range(g, g + FOX_HEAD_GROUP)) for g in range(0, HEADS, FOX_HEAD_GROUP)]

    def body(kp, c):
        for hs in groups:
            tile(2 * kp, hs)
            tile(2 * kp + 1, hs)
        return c

    lax.fori_loop(0, qi, body, 0)
    for dg in range(tq // t):
        for hs in groups:
            tile(qi * (tq // t) + dg, hs, dg)
    outs = []
    for h in range(HEADS):
        a = acc_ref[h]
        outs.append(a[:, :HEAD_DIM] / a[:, HEAD_DIM:])
    y_ref[...] = jnp.concatenate(outs, axis=1).astype(y_ref.dtype)


def _fox(proj, gt, l, bias_r, q_g, k_g, bd, tri):
    nb, s, _ = proj.shape
    t = ATT_T
    tq = FOX_TQ
    kern = functools.partial(_fox_kernel, t=t, tq=tq, s_len=s)
    return pl.pallas_call(
        kern,
        out_shape=jax.ShapeDtypeStruct((nb, s, GROUP), BF16),
        grid=(nb, s // tq),
        in_specs=[pl.BlockSpec((None, tq, GROUP), lambda b, i: (b, i, C_FQ // GROUP)),
                  pl.BlockSpec((None, s, GROUP), lambda b, i: (b, 0, C_FK // GROUP)),
                  pl.BlockSpec((None, s, GROUP), lambda b, i: (b, 0, C_FV // GROUP)),
                  pl.BlockSpec((None, GT_ROWS, s), lambda b, i: (b, 0, 0)),
                  pl.BlockSpec((None, GT_ROWS, 1), lambda b, i: (l, 0, 0)),
                  pl.BlockSpec((None, 1, GROUP), lambda b, i: (l, 0, 0)),
                  pl.BlockSpec((None, 1, GROUP), lambda b, i: (l, 0, 0)),
                  pl.BlockSpec((GROUP, GROUP), lambda b, i: (0, 0)),
                  pl.BlockSpec((t, t), lambda b, i: (0, 0))],
        out_specs=pl.BlockSpec((None, tq, GROUP), lambda b, i: (b, i, 0)),
        scratch_shapes=[pltpu.VMEM((HEADS, s, HEAD_DIM), BF16),
                        pltpu.VMEM((HEADS, s, 2 * HEAD_DIM), BF16),
                        pltpu.VMEM((8, s), F32),
                        pltpu.VMEM((HEADS, tq, 128), F32),
                        pltpu.VMEM((HEADS, tq, 2 * HEAD_DIM), F32)],
        compiler_params=_cp(("arbitrary", "arbitrary")),
        name="fox",
    )(proj, proj, proj, gt, bias_r, q_g, k_g, bd, tri)


def _sb_kernel(q_ref, k_ref, v_ref, sm_ref, y_ref, kb_ref, vb_ref, rest_ref, acc_ref, *, t, tq, s_len):
    qi = pl.program_id(1)

    @pl.when(qi == 0)
    def _():
        for r in range(s_len // t):
            rows = slice(r * t, (r + 1) * t)
            kk = k_ref[rows, :]
            vv = v_ref[rows, :]
            for h in range(HEADS):
                sl = slice(h * HEAD_DIM, (h + 1) * HEAD_DIM)
                kb_ref[h, rows, :] = kk[:, sl].astype(BF16)
                vb_ref[h, rows, :] = vv[:, sl].astype(BF16)

    qs = q_ref[...] * (QK_SCALE * LOG2E)
    qh = [qs[:, h * HEAD_DIM:(h + 1) * HEAD_DIM].astype(BF16) for h in range(HEADS)]
    krel = lax.broadcasted_iota(jnp.int32, (tq, t), 1) - lax.broadcasted_iota(jnp.int32, (tq, t), 0)
    from_here = sm_ref[...]
    rest_ref[...] = jnp.zeros(rest_ref.shape, F32)
    acc_ref[...] = jnp.zeros(acc_ref.shape, F32)

    def step(kb, diag=None):
        k0 = pl.multiple_of(kb * t, t)
        hs = range(HEADS)
        masked = diag is not None
        if masked:
            strict = krel < -diag * t
        z = [_dot_t(qh[h], kb_ref[h, pl.ds(k0, t), :]) for h in hs]
        u = [jnp.maximum(z[h], 0.0) + jnp.log2(1.0 + jnp.exp2(-jnp.abs(z[h]))) for h in hs]
        if masked:
            u = [jnp.where(strict, v, 0.0) for v in u]
        incl = [jnp.dot(u[h].astype(BF16), from_here, preferred_element_type=F32) for h in hs]
        rest = [rest_ref[h] for h in hs]
        a = [jnp.exp2(z[h] - incl[h] - jnp.concatenate([rest[h]] * (t // 128), axis=1)) for h in hs]
        if masked:
            a = [jnp.where(strict, v, 0.0) for v in a]
        av = [jnp.dot(a[h].astype(BF16), vb_ref[h, pl.ds(k0, t), :], preferred_element_type=F32) for h in hs]
        for h in hs:
            acc_ref[h] += av[h]
            rest_ref[h] = rest[h] + incl[h][:, 0:1]

    nd = tq // t
    for dg in reversed(range(nd)):
        step(qi * nd + dg, dg)

    n_un = qi * nd

    if nd % 2 == 1:
        @pl.when(n_un % 2 == 1)
        def _():
            step(n_un - 1)

    def body(j, c):
        kb = 2 * (n_un // 2 - 1 - j)
        step(kb + 1)
        step(kb)
        return c

    lax.fori_loop(0, n_un // 2, body, 0)
    y_ref[...] = jnp.concatenate([acc_ref[h] for h in range(HEADS)], axis=1).astype(y_ref.dtype)


def _sb(proj, after):
    nb, s, _ = proj.shape
    t = ATT_T
    tq = ATT_T
    kern = functools.partial(_sb_kernel, t=t, tq=tq, s_len=s)
    return pl.pallas_call(
        kern,
        out_shape=jax.ShapeDtypeStruct((nb, s, GROUP), BF16),
        grid=(nb, s // tq),
        in_specs=[pl.BlockSpec((None, tq, GROUP), lambda b, i: (b, i, C_SQ // GROUP)),
                  pl.BlockSpec((None, s, GROUP), lambda b, i: (b, 0, C_SK // GROUP)),
                  pl.BlockSpec((None, s, GROUP), lambda b, i: (b, 0, C_SV // GROUP)),
                  pl.BlockSpec((t, t), lambda b, i: (0, 0))],
        out_specs=pl.BlockSpec((None, tq, GROUP), lambda b, i: (b, i, 0)),
        scratch_shapes=[pltpu.VMEM((HEADS, s, HEAD_DIM), BF16),
                        pltpu.VMEM((HEADS, s, HEAD_DIM), BF16),
                        pltpu.VMEM((HEADS, tq, 128), F32),
                        pltpu.VMEM((HEADS, tq, HEAD_DIM), F32)],
        compiler_params=_cp(("arbitrary", "arbitrary")),
        name="stick_breaking",
    )(proj, proj, proj, after)


def _nsa_kernel(q_ref, kvs_ref, kvw_ref, ak_ref, av_ref, gc_ref, gb_ref, qg_ref, kg_ref, pos_ref,
                wk_ref, wv_ref, bd_ref, tc_ref, ovt_ref, tsb_ref, twb_ref, cb_ref, y_ref,
                ksel_ref, vsel_ref, kwin_ref, vwin_ref, kcmp_ref, vcmp_ref, m_ref, acc_ref,
                *, tq, s_len, n_sel):
    qi = pl.program_id(1)
    nc = s_len // CMP_STRIDE

    @pl.when(qi == 0)
    def _():
        zk = jnp.zeros((WINDOW, HEAD_DIM), BF16)
        zv = jnp.zeros((WINDOW, 2 * HEAD_DIM), BF16)
        ksel_ref[0:WINDOW, :] = zk
        vsel_ref[0:WINDOW, :] = zv
        kwin_ref[0:WINDOW, :] = zk
        vwin_ref[0:WINDOW, :] = zv
        ones = jnp.ones((512, HEAD_DIM), BF16)
        for r in range(s_len // 512):
            rows = slice(r * 512, (r + 1) * 512)
            dst = slice(WINDOW + r * 512, WINDOW + (r + 1) * 512)
            kvs = kvs_ref[rows, :]
            kvw = kvw_ref[rows, :]
            ksel_ref[dst, :] = _rms_lanes(kvs[:, :HEAD_DIM], kg_ref[1:2, :]).astype(BF16)
            kwin_ref[dst, :] = _rms_lanes(kvw[:, :HEAD_DIM], kg_ref[2:3, :]).astype(BF16)
            vsel_ref[dst, :] = jnp.concatenate([kvs[:, HEAD_DIM:].astype(BF16), ones], axis=1)
            vwin_ref[dst, :] = jnp.concatenate([kvw[:, HEAD_DIM:].astype(BF16), ones], axis=1)
        half = CMP_STRIDE * HEAD_DIM
        for a_ref, w_ref, dst_ref, norm in ((ak_ref, wk_ref, kcmp_ref, True), (av_ref, wv_ref, vcmp_ref, False)):
            a = a_ref[...]
            pos_b = _dot_hi(pos_ref[...], w_ref[...])[0:1, :]
            p1 = _dot_hi(a, w_ref[0:half, :])
            p2 = _dot_hi(a, w_ref[half:2 * half, :])
            c = p1 + pltpu.roll(p2, nc - 1, 0) + pos_b
            if norm:
                hi, lo = _split2(_rms_lanes(c, kg_ref[0:1, :]))
                kcmp_ref[0] = hi
                kcmp_ref[1] = lo
            else:
                dst_ref[...] = c.astype(BF16)

    t0 = qi * tq
    qn = _rms_heads(q_ref[...], bd_ref[...], qg_ref[...]) * (QK_SCALE * LOG2E)
    qf = [qn[:, h * HEAD_DIM:(h + 1) * HEAD_DIM] for h in range(HEADS)]
    qh = [q.astype(BF16) for q in qf]
    ql = [(qf[h] - qh[h].astype(F32)).astype(BF16) for h in range(HEADS)]

    hs = range(HEADS)
    k_hi, k_lo = kcmp_ref[0], kcmp_ref[1]
    dn = (((1,), (1,)), ((), ()))
    sc = [lax.dot_general(qh[h], k_hi, dn, preferred_element_type=F32)
          + lax.dot_general(ql[h], k_hi, dn, preferred_element_type=F32)
          + lax.dot_general(qh[h], k_lo, dn, preferred_element_type=F32) + tc_ref[h] for h in hs]
    e = [jnp.exp2(sc[h] - jnp.max(sc[h], axis=1, keepdims=True)) for h in hs]
    p = [jnp.where(tc_ref[h] > 0.5 * NEG, e[h] / jnp.sum(e[h], axis=1, keepdims=True), 0.0) for h in hs]
    o_cmp = [jnp.dot(p[h].astype(BF16), vcmp_ref[...], preferred_element_type=F32) for h in hs]
    ps_hi, ps_lo = _split2(p[0] + p[1] + p[2] + p[3])
    ovt = ovt_ref[...].astype(BF16)
    imp_t = (lax.dot_general(ovt, ps_hi, dn, preferred_element_type=F32)
             + lax.dot_general(ovt, ps_lo, dn, preferred_element_type=F32))

    nr = -(-n_sel // 8) * 8
    jj = lax.broadcasted_iota(jnp.int32, (nr, tq), 0)
    tt = t0 + lax.broadcasted_iota(jnp.int32, (nr, tq), 1)
    cur = tt // SEL_BLOCK
    forced = (jj == 0) | (jj == cur) | (jj == cur - 1)
    score = jnp.where(forced, -NEG, jnp.where(jj * SEL_BLOCK <= tt, imp_t[:nr], NEG))
    cnt = jnp.zeros((nr, tq), F32)
    for i in range(n_sel):
        ri = score[i:i + 1, :]
        cnt = cnt + jnp.where(jj > i, (ri >= score).astype(F32), (ri > score).astype(F32))
    sel_t = (cnt < float(min(N_SEL_TOP, n_sel))).astype(F32)
    if nr < 128:
        sel_t = jnp.concatenate([sel_t, jnp.zeros((128 - nr, tq), F32)], axis=0)
    selb = sel_t.T.astype(BF16)

    m_ref[...] = jnp.full(m_ref.shape, NEG, F32)
    acc_ref[...] = jnp.zeros(acc_ref.shape, F32)
    jrow = lax.broadcasted_iota(jnp.int32, (128, tq), 0)
    jcol = lax.broadcasted_iota(jnp.int32, (128, tq), 1) // SEL_BLOCK

    def sel_chunk(kp0, near):
        jb = kp0 // SEL_BLOCK - WINDOW // SEL_BLOCK
        expand = (jrow == jb + jcol).astype(BF16)
        picked = jnp.dot(selb, expand, preferred_element_type=F32) > 0.5
        kblk = ksel_ref[pl.ds(kp0, tq), :]
        vblk = vsel_ref[pl.ds(kp0, tq), :]
        hs = range(HEADS)
        bias = [cb_ref[h] if near is None else tsb_ref[h, :, near * tq:(near + 1) * tq] for h in hs]
        sck = [_dot_t(qh[h], kblk) + jnp.where(picked, bias[h], NEG) for h in hs]
        m_old = [m_ref[h] for h in hs]
        m_new = [jnp.maximum(m_old[h], jnp.max(sck[h], axis=1, keepdims=True)) for h in hs]
        pk = [jnp.exp2(sck[h] - jnp.concatenate([m_new[h]] * (tq // 128), axis=1)).astype(BF16) for h in hs]
        pv = [jnp.dot(pk[h], vblk, preferred_element_type=F32) for h in hs]
        for h in hs:
            acc_ref[h] = jnp.exp2(m_old[h] - m_new[h]) * acc_ref[h] + pv[h]
            m_ref[h] = m_new[h]

    def far(cp, carry):
        sel_chunk(pl.multiple_of(WINDOW + 2 * cp * tq, tq), None)
        sel_chunk(pl.multiple_of(WINDOW + (2 * cp + 1) * tq, tq), None)
        return carry

    n_far = jnp.maximum(qi - WINDOW // tq, 0)
    lax.fori_loop(0, n_far // 2, far, 0)

    @pl.when(n_far % 2 == 1)
    def _():
        sel_chunk(pl.multiple_of(WINDOW + (n_far - 1) * tq, tq), None)
    for r in range(NSA_NEAR // tq):
        @pl.when(qi + r >= WINDOW // tq)
        def _(r=r):
            sel_chunk(pl.multiple_of(t0 + r * tq, tq), r)

    kw = kwin_ref[pl.ds(pl.multiple_of(t0, tq), NSA_NEAR), :]
    vw = vwin_ref[pl.ds(pl.multiple_of(t0, tq), NSA_NEAR), :]
    in_seq = lax.broadcasted_iota(jnp.int32, (tq, NSA_NEAR), 1) >= WINDOW - t0
    g = _sigmoid(gc_ref[...] + gb_ref[...])
    sw = [jnp.where(in_seq, _dot_t(qh[h], kw) + twb_ref[h], NEG) for h in hs]
    pw = [jnp.exp2(sw[h] - jnp.max(sw[h], axis=1, keepdims=True)).astype(BF16) for h in hs]
    ows = [jnp.dot(pw[h], vw, preferred_element_type=F32) for h in hs]
    outs = []
    for h in hs:
        ow = ows[h]
        o_win = ow[:, :HEAD_DIM] / ow[:, HEAD_DIM:]
        a = acc_ref[h]
        o_sel = a[:, :HEAD_DIM] / a[:, HEAD_DIM:]
        c0 = G_NG + 3 * h
        outs.append(g[:, c0:c0 + 1] * o_cmp[h] + g[:, c0 + 1:c0 + 2] * o_sel + g[:, c0 + 2:c0 + 3] * o_win)
    y_ref[...] = jnp.concatenate(outs, axis=1).astype(y_ref.dtype)


def _nsa(proj, a_k, a_v, l, bias_c, q_g, k_g, pos, wk, wv, bd, tc, ovt, tsb, twb, cb):
    nb, s, _ = proj.shape
    tq = NSA_TQ
    nc = s // CMP_STRIDE
    kern = functools.partial(_nsa_kernel, tq=tq, s_len=s, n_sel=s // SEL_BLOCK)
    const2 = lambda b, i: (0, 0)
    const3 = lambda b, i: (0, 0, 0)
    lay3 = lambda b, i: (l, 0, 0)
    return pl.pallas_call(
        kern,
        out_shape=jax.ShapeDtypeStruct((nb, s, GROUP), BF16),
        grid=(nb, s // tq),
        in_specs=[pl.BlockSpec((None, tq, GROUP), lambda b, i: (b, i, C_NQ // GROUP)),
                  pl.BlockSpec((None, s, 128), lambda b, i: (b, 0, C_NKV // 128 + 1)),
                  pl.BlockSpec((None, s, 128), lambda b, i: (b, 0, C_NKV // 128 + 2)),
                  pl.BlockSpec((None, nc, CMP_STRIDE * HEAD_DIM), lambda b, i: (b, 0, 0)),
                  pl.BlockSpec((None, nc, CMP_STRIDE * HEAD_DIM), lambda b, i: (b, 0, 0)),
                  pl.BlockSpec((None, tq, 128), lambda b, i: (b, i, C_GATES // 128)),
                  pl.BlockSpec((None, 1, 128), lay3),
                  pl.BlockSpec((None, 1, GROUP), lay3),
                  pl.BlockSpec((None, 8, HEAD_DIM), lay3),
                  pl.BlockSpec((None, 8, CMP_BLOCK * HEAD_DIM), lay3),
                  pl.BlockSpec((None, CMP_BLOCK * HEAD_DIM, HEAD_DIM), lay3),
                  pl.BlockSpec((None, CMP_BLOCK * HEAD_DIM, HEAD_DIM), lay3),
                  pl.BlockSpec((GROUP, GROUP), const2),
                  pl.BlockSpec((HEADS, tq, nc), lambda b, i: (0, i, 0)),
                  pl.BlockSpec((128, nc), const2),
                  pl.BlockSpec((HEADS, tq, NSA_NEAR), const3),
                  pl.BlockSpec((HEADS, tq, NSA_NEAR), const3),
                  pl.BlockSpec(memory_space=pltpu.SMEM)],
        out_specs=pl.BlockSpec((None, tq, GROUP), lambda b, i: (b, i, 0)),
        scratch_shapes=[pltpu.VMEM((s + WINDOW, HEAD_DIM), BF16),
                        pltpu.VMEM((s + WINDOW, 2 * HEAD_DIM), BF16),
                        pltpu.VMEM((s + WINDOW, HEAD_DIM), BF16),
                        pltpu.VMEM((s + WINDOW, 2 * HEAD_DIM), BF16),
                        pltpu.VMEM((2, nc, HEAD_DIM), BF16),
                        pltpu.VMEM((nc, HEAD_DIM), BF16),
                        pltpu.VMEM((HEADS, tq, 128), F32),
                        pltpu.VMEM((HEADS, tq, 2 * HEAD_DIM), F32)],
        compiler_params=_cp(("arbitrary", "arbitrary")),
        name="nsa",
    )(proj, proj, proj, a_k, a_v, proj, bias_c, q_g, k_g, pos, wk, wv, bd, tc, ovt, tsb, twb, cb)


def _outproj_kernel(x_ref, ya_ref, yb_ref, yc_ref, yd_ref, w_ref, mod_ref, g_ref, xo_ref, h_ref):
    acc = jnp.dot(ya_ref[...], w_ref[0:GROUP, :], preferred_element_type=F32)
    acc += jnp.dot(yb_ref[...], w_ref[GROUP:2 * GROUP, :], preferred_element_type=F32)
    acc += jnp.dot(yc_ref[...], w_ref[2 * GROUP:3 * GROUP, :], preferred_element_type=F32)
    acc += jnp.dot(yd_ref[...], w_ref[3 * GROUP:4 * GROUP, :], preferred_element_type=F32)
    xn = x_ref[...] + mod_ref[2:3, :] * acc
    xo_ref[...] = xn
    h = _rms_lanes(xn, g_ref[...]) * (1.0 + mod_ref[4:5, :]) + mod_ref[3:4, :]
    h_ref[...] = h.astype(h_ref.dtype)


def _outproj(x, ys, mod, l, w_out, norm_g, h_dtype):
    nb, s, d = x.shape
    tm = 512
    yspec = pl.BlockSpec((None, tm, GROUP), lambda b, i: (b, i, 0))
    xspec = pl.BlockSpec((None, tm, d), lambda b, i: (b, i, 0))
    return pl.pallas_call(
        _outproj_kernel,
        out_shape=(jax.ShapeDtypeStruct((nb, s, d), F32), jax.ShapeDtypeStruct((nb, s, d), h_dtype)),
        grid=(nb, s // tm),
        in_specs=[xspec, yspec, yspec, yspec, yspec,
                  pl.BlockSpec((None, d, d), lambda b, i: (l, 0, 0)),
                  pl.BlockSpec((None, None, 6, d), lambda b, i: (l, b, 0, 0)),
                  pl.BlockSpec((None, 1, d), lambda b, i: (l, 0, 0))],
        out_specs=(xspec, xspec),
        compiler_params=_cp(("arbitrary", "arbitrary")),
        name="outproj",
    )(x, *ys, w_out, mod, norm_g)


def _ffn_kernel(h_ref, x_ref, mod_ref, wg_ref, wu_ref, wd_ref, o_ref, acc_ref, *, nf):
    f = pl.program_id(2)

    @pl.when(f == 0)
    def _():
        acc_ref[...] = jnp.zeros_like(acc_ref)

    h = h_ref[...]
    a = jnp.dot(h, wg_ref[...], preferred_element_type=F32)
    u = jnp.dot(h, wu_ref[...], preferred_element_type=F32)
    act = (a * _sigmoid(a) * u).astype(BF16)
    acc_ref[...] += jnp.dot(act, wd_ref[...], preferred_element_type=F32)

    @pl.when(f == nf - 1)
    def _():
        o_ref[...] = x_ref[...] + mod_ref[5:6, :] * acc_ref[...]


def _ffn(h, x, mod, l, li, wg, wu, wd):
    nb, s, d = x.shape
    dff = wg.shape[-1]
    tm, tf = FFN_TM, FFN_TF
    nf = dff // tf
    wmode = pl.Buffered(1) if nf == 1 else None
    xspec = pl.BlockSpec((None, tm, d), lambda b, i, f: (b, i, 0))
    return pl.pallas_call(
        functools.partial(_ffn_kernel, nf=nf),
        out_shape=jax.ShapeDtypeStruct((nb, s, d), F32),
        grid=(nb, s // tm, nf),
        in_specs=[xspec, xspec,
                  pl.BlockSpec((None, None, 6, d), lambda b, i, f: (l, b, 0, 0)),
                  pl.BlockSpec((None, d, tf), lambda b, i, f: (li, 0, f), pipeline_mode=wmode),
                  pl.BlockSpec((None, d, tf), lambda b, i, f: (li, 0, f), pipeline_mode=wmode),
                  pl.BlockSpec((None, tf, d), lambda b, i, f: (li, f, 0), pipeline_mode=wmode)],
        out_specs=xspec,
        scratch_shapes=[pltpu.VMEM((tm, d), F32)],
        compiler_params=_cp(("arbitrary", "arbitrary", "arbitrary")),
        name="ffn_dense",
    )(h, x, mod, wg, wu, wd)


def _router_kernel(h_ref, w_ref, b_ref, o_ref):
    h_hi, h_lo = _split2(h_ref[...])
    w_hi, w_lo = _split2(w_ref[...])
    logits = (jnp.dot(h_hi, w_hi, preferred_element_type=F32) + jnp.dot(h_lo, w_hi, preferred_element_type=F32)
              + jnp.dot(h_hi, w_lo, preferred_element_type=F32)) + b_ref[...]
    lane = lax.broadcasted_iota(jnp.int32, logits.shape, 1).astype(F32)
    lg = jnp.where(lane < N_EXPERTS, logits, -3e38)
    m1 = jnp.max(lg, axis=1, keepdims=True)
    i1 = jnp.min(jnp.where(lg == m1, lane, 128.0), axis=1, keepdims=True)
    lg2 = jnp.where(lane == i1, -3e38, lg)
    m2 = jnp.max(lg2, axis=1, keepdims=True)
    i2 = jnp.min(jnp.where(lg2 == m2, lane, 128.0), axis=1, keepdims=True)
    e2 = jnp.exp(m2 - m1)
    w1 = 1.0 / (1.0 + e2)
    w2 = e2 / (1.0 + e2)
    o_ref[...] = jnp.where(lane == 0, i1, jnp.where(lane == 1, i2, jnp.where(lane == 2, w1, jnp.where(lane == 3, w2, 0.0))))


def _router(hf, rw, rb):
    t, d = hf.shape
    tm = 1024
    return pl.pallas_call(
        _router_kernel,
        out_shape=jax.ShapeDtypeStruct((t, 128), F32),
        grid=(t // tm,),
        in_specs=[pl.BlockSpec((tm, d), lambda i: (i, 0)),
                  pl.BlockSpec((d, 128), lambda i: (0, 0)),
                  pl.BlockSpec((1, 128), lambda i: (0, 0))],
        out_specs=pl.BlockSpec((tm, 128), lambda i: (i, 0)),
        compiler_params=_cp(("arbitrary",)),
        name="moe_router",
    )(hf, rw, rb)


def _gather_copy(src_hbm, row, dst_buf, slot, r, sem):
    return pltpu.make_async_copy(src_hbm.at[pl.ds(row, 1)], dst_buf.at[slot, pl.ds(r, 1)], sem.at[slot])


def _experts_kernel(be_ref, tok_ref, nv_ref, h_hbm, wg_ref, wu_ref, wd_ref, y_ref,
                    xbuf, xb_ref, acc_ref, sem, *, bm, nf):
    del be_ref
    i = pl.program_id(0)
    f = pl.program_id(1)
    nvalid = nv_ref[0]
    slot = i % 2
    chunk = bm // nf

    def for_rows(fn):
        def body(r, c):
            fn(r)
            return c
        lax.fori_loop(0, bm, body, 0, unroll=8)

    @pl.when((f == 0) & (i == 0))
    def _():
        for_rows(lambda r: _gather_copy(h_hbm, tok_ref[r], xbuf, 0, r, sem).start())

    @pl.when((f == 0) & (i < nvalid))
    def _():
        for_rows(lambda r: _gather_copy(h_hbm, 0, xbuf, slot, r, sem).wait())
        xb_ref[...] = xbuf[slot].astype(BF16)
        acc_ref[...] = jnp.zeros_like(acc_ref)

    def compute(prefetch):
        third = -(-chunk // 3)

        def issue_rows(lo, hi):
            if prefetch:
                for j in range(lo, min(hi, chunk)):
                    r = f * chunk + j
                    _gather_copy(h_hbm, tok_ref[(i + 1) * bm + r], xbuf, 1 - slot, r, sem).start()

        x = xb_ref[...]
        issue_rows(0, third)
        a = jnp.dot(x, wg_ref[...], preferred_element_type=F32)
        issue_rows(third, 2 * third)
        u = jnp.dot(x, wu_ref[...], preferred_element_type=F32)
        issue_rows(2 * third, 3 * third)
        act = (a * _sigmoid(a) * u).astype(BF16)
        acc_ref[...] += jnp.dot(act, wd_ref[...], preferred_element_type=F32)

    @pl.when(i + 1 < nvalid)
    def _():
        compute(True)

    @pl.when(i + 1 == nvalid)
    def _():
        compute(False)

    @pl.when(f == nf - 1)
    def _():
        @pl.when(i < nvalid)
        def _():
            y_ref[...] = acc_ref[...]

        @pl.when(i >= nvalid)
        def _():
            y_ref[...] = jnp.zeros_like(y_ref)


def _experts(hf, blk_expert, row_tok, nvalid, wg, wu, wd, li):
    t, d = hf.shape
    bm, tf = MOE_BM, MOE_TF
    n_rows = row_tok.shape[0]
    n_blocks = n_rows // bm
    dff = wg.shape[-1]
    nf = dff // tf

    wmode = pl.Buffered(1) if nf == 1 else None

    def fidx(i, f, nv):
        return jnp.where(i < nv[0], f, nf - 1)

    grid_spec = pltpu.PrefetchScalarGridSpec(
        num_scalar_prefetch=3,
        grid=(n_blocks, nf),
        in_specs=[pl.BlockSpec(memory_space=pl.ANY),
                  pl.BlockSpec((None, None, d, tf), lambda i, f, be, tok, nv: (li, be[i], 0, fidx(i, f, nv)),
                               pipeline_mode=wmode),
                  pl.BlockSpec((None, None, d, tf), lambda i, f, be, tok, nv: (li, be[i], 0, fidx(i, f, nv)),
                               pipeline_mode=wmode),
                  pl.BlockSpec((None, None, tf, d), lambda i, f, be, tok, nv: (li, be[i], fidx(i, f, nv), 0),
                               pipeline_mode=wmode)],
        out_specs=pl.BlockSpec((bm, d), lambda i, f, be, tok, nv: (i, 0)),
        scratch_shapes=[pltpu.VMEM((2, bm, d), F32),
                        pltpu.VMEM((bm, d), BF16),
                        pltpu.VMEM((bm, d), F32),
                        pltpu.SemaphoreType.DMA((2,))],
    )
    return pl.pallas_call(
        functools.partial(_experts_kernel, bm=bm, nf=nf),
        out_shape=jax.ShapeDtypeStruct((n_rows, d), F32),
        grid_spec=grid_spec,
        compiler_params=_cp(("arbitrary", "arbitrary")),
        name="moe_experts",
    )(blk_expert, row_tok, nvalid, hf, wg, wu, wd)


def _combine_kernel(pos_ref, y_hbm, x_ref, mod_ref, rt_ref, o_ref, ybuf, sem, *, tm, s_len):
    g = pl.program_id(0) * pl.num_programs(1) + pl.program_id(1)
    n_tiles = pl.num_programs(0) * pl.num_programs(1)
    slot = g % 2

    def copy(tile, sl, k, r):
        row = pos_ref[(tile * tm + r) * 2 + k]
        return pltpu.make_async_copy(y_hbm.at[pl.ds(row, 1)], ybuf.at[2 * sl + k, pl.ds(r, 1)], sem.at[sl])

    def issue(tile, sl):
        def body(r, c):
            copy(tile, sl, 0, r).start()
            copy(tile, sl, 1, r).start()
            return c
        lax.fori_loop(0, tm, body, 0, unroll=8)

    @pl.when(g == 0)
    def _():
        issue(0, 0)

    @pl.when(g + 1 < n_tiles)
    def _():
        issue(g + 1, 1 - slot)

    def wait(r, c):
        copy(g, slot, 0, r).wait()
        copy(g, slot, 1, r).wait()
        return c

    lax.fori_loop(0, tm, wait, 0, unroll=8)
    rt = rt_ref[...]
    o_ref[...] = x_ref[...] + mod_ref[5:6, :] * (rt[:, 2:3] * ybuf[2 * slot] + rt[:, 3:4] * ybuf[2 * slot + 1])


def _combine(y, pos, x, mod, route, l):
    nb, s, d = x.shape
    tm = 256
    grid_spec = pltpu.PrefetchScalarGridSpec(
        num_scalar_prefetch=1,
        grid=(nb, s // tm),
        in_specs=[pl.BlockSpec(memory_space=pl.ANY),
                  pl.BlockSpec((None, tm, d), lambda b, i, p: (b, i, 0)),
                  pl.BlockSpec((None, None, 6, d), lambda b, i, p: (l, b, 0, 0)),
                  pl.BlockSpec((None, tm, 128), lambda b, i, p: (b, i, 0))],
        out_specs=pl.BlockSpec((None, tm, d), lambda b, i, p: (b, i, 0)),
        scratch_shapes=[pltpu.VMEM((4, tm, d), F32), pltpu.SemaphoreType.DMA((2,))],
    )
    return pl.pallas_call(
        functools.partial(_combine_kernel, tm=tm, s_len=s),
        out_shape=jax.ShapeDtypeStruct((nb, s, d), F32),
        grid_spec=grid_spec,
        compiler_params=_cp(("arbitrary", "arbitrary")),
        name="moe_combine",
    )(pos, y, x, mod, route.reshape(nb, s, 128))


def _moe(hf32, x, mod, l, li, router_w, router_b, wg, wu, wd):
    nb, s, d = x.shape
    t = nb * s
    bm = MOE_BM
    hf = hf32.reshape(t, d)
    rw = jnp.zeros((d, 128), F32).at[:, :N_EXPERTS].set(router_w[li])
    rb = jnp.zeros((1, 128), F32).at[0, :N_EXPERTS].set(router_b[li])
    route = _router(hf, rw, rb)
    e_flat = route[:, 0:2].astype(jnp.int32).reshape(-1)
    n_assign = 2 * t
    onehot = (e_flat[:, None] == jnp.arange(N_EXPERTS, dtype=jnp.int32)[None, :]).astype(jnp.int32)
    csum = jnp.cumsum(onehot, axis=0)
    rank = jnp.sum(onehot * csum, axis=1) - 1
    counts = csum[-1]
    padded = (counts + bm - 1) // bm * bm
    cum_padded = jnp.cumsum(padded)
    pstart = cum_padded - padded
    dest = pstart[e_flat] + rank
    n_blocks = n_assign // bm + N_EXPERTS
    n_rows = n_blocks * bm
    tok_flat = jnp.arange(n_assign, dtype=jnp.int32) // 2
    row_tok = jnp.zeros((n_rows,), jnp.int32).at[dest].set(tok_flat)
    blk_start = jnp.arange(n_blocks, dtype=jnp.int32) * bm
    blk_expert = jnp.minimum(jnp.searchsorted(cum_padded, blk_start, side='right'), N_EXPERTS - 1).astype(jnp.int32)
    nvalid = (cum_padded[-1] // bm).astype(jnp.int32).reshape(1)
    y = _experts(hf, blk_expert, row_tok, nvalid, wg, wu, wd, li)
    return _combine(y, dest.astype(jnp.int32), x, mod, route, l)


def _t5_bucket(dist):
    n = np.maximum(dist, 0)
    max_exact = NUM_BUCKETS // 2
    large = max_exact + (np.log(np.maximum(n, 1).astype(np.float32) / max_exact)
                         / math.log(MAX_DISTANCE / max_exact) * (NUM_BUCKETS - max_exact)).astype(np.int32)
    return np.where(n < max_exact, n, np.minimum(large, NUM_BUCKETS - 1)).astype(np.int32)


def _bias_by_bucket(rb, bucket):
    ids = jnp.asarray(bucket.astype(np.int8))[None]
    out = jnp.zeros((rb.shape[1],) + bucket.shape, F32)
    for k in range(NUM_BUCKETS):
        out = jnp.where(ids == k, rb[k][:, None, None], out)
    return out


def _nsa_tables(rel_bias, s):
    tq = NSA_TQ
    nc = s // CMP_STRIDE
    n_cmp = (s - CMP_BLOCK) // CMP_STRIDE + 1
    rb = rel_bias.astype(F32) * LOG2E
    t = np.arange(s)[:, None]
    cmp_end = np.arange(nc)[None, :] * CMP_STRIDE + CMP_BLOCK - 1
    ok = (cmp_end <= t) & (np.arange(nc)[None, :] < n_cmp)
    tc = jnp.where(jnp.asarray(ok)[None], _bias_by_bucket(rb, _t5_bucket(t - cmp_end)), NEG)
    dist = np.arange(tq)[:, None] - (np.arange(NSA_NEAR)[None, :] - WINDOW)
    tb = _bias_by_bucket(rb, _t5_bucket(dist))
    tsb = jnp.where(jnp.asarray(dist >= 0)[None], tb, NEG)
    twb = jnp.where(jnp.asarray((dist >= 0) & (dist < WINDOW))[None], tb, NEG)
    far_bucket = int(_t5_bucket(np.array([WINDOW]))[0])
    cb = rb[far_bucket]
    n_sel = s // SEL_BLOCK
    cs = np.arange(nc)[None, :] * CMP_STRIDE
    ss = np.arange(128)[:, None] * SEL_BLOCK
    ov = np.clip(np.minimum(cs + CMP_BLOCK, ss + SEL_BLOCK) - np.maximum(cs, ss), 0, None).astype(np.float32) / CMP_BLOCK
    ov = ov * (np.arange(128)[:, None] < n_sel) * (np.arange(nc)[None, :] < n_cmp)
    return tc, jnp.asarray(ov, F32), tsb, twb, cb


def _prep_layer_params(w_in, mlstm_gate_b, fox_f_b, nsa_gate_b):
    depth, d, _ = w_in.shape
    small = [w_in[:, :, 1024:1032], w_in[:, :, 1800:1804], w_in[:, :, 3212:3224]]
    w_p = jnp.concatenate([w_in[:, :, 0:1024], w_in[:, :, 1032:1800], w_in[:, :, 1804:2572], w_in[:, :, 2572:3212]]
                          + small + [jnp.zeros((depth, d, 128 - 24), w_in.dtype)], axis=-1).astype(BF16)
    wgt = jnp.concatenate(small + [jnp.zeros((depth, d, GT_ROWS - 24), w_in.dtype)], axis=-1)
    wgt = jnp.transpose(wgt, (0, 2, 1)).astype(BF16)
    gate_b = jnp.concatenate([mlstm_gate_b, fox_f_b, nsa_gate_b], axis=-1).astype(F32)
    bias_c = jnp.zeros((depth, 1, 128), F32).at[:, 0, :24].set(gate_b)
    bias_r = jnp.zeros((depth, GT_ROWS, 1), F32).at[:, :24, 0].set(gate_b)
    return w_p, wgt, bias_c, bias_r


def kernel(x, c, rel_bias, ada_w, ada_b, norm1_g, norm2_g, w_in, w_out, mlstm_conv_w, mlstm_gate_b, mlstm_out_g,
           fox_f_b, fox_q_g, fox_k_g, nsa_q_g, nsa_k_g, nsa_cmp_pos, nsa_cmp_wk, nsa_cmp_wv, nsa_gate_b,
           ffn_wg, ffn_wu, ffn_wd, moe_router_w, moe_router_b, moe_wg, moe_wu, moe_wd):
    nb, s, d = x.shape
    depth = w_in.shape[0]
    assert d == D_MODEL and s % 512 == 0 and s // SEL_BLOCK <= 128

    w_p, wgt, bias_c, bias_r = _prep_layer_params(w_in, mlstm_gate_b, fox_f_b, nsa_gate_b)
    w_out_b = w_out.astype(BF16)
    n1g = norm1_g.reshape(depth, 1, d)
    n2g = norm2_g.reshape(depth, 1, d)
    m_out_g = mlstm_out_g.reshape(depth, 1, GROUP)
    fq_g = jnp.tile(fox_q_g, (1, HEADS)).reshape(depth, 1, GROUP)
    fk_g = jnp.tile(fox_k_g, (1, HEADS)).reshape(depth, 1, GROUP)
    nq_g = jnp.tile(nsa_q_g, (1, HEADS)).reshape(depth, 1, GROUP)
    nk_g = jnp.zeros((depth, 8, HEAD_DIM), F32).at[:, :3].set(nsa_k_g)
    pos8 = jnp.zeros((depth, 8, CMP_BLOCK * HEAD_DIM), F32).at[:, 0].set(nsa_cmp_pos.reshape(depth, -1))
    ffn_wg_b, ffn_wu_b, ffn_wd_b = ffn_wg.astype(BF16), ffn_wu.astype(BF16), ffn_wd.astype(BF16)
    moe_wg_b, moe_wu_b, moe_wd_b = moe_wg.astype(BF16), moe_wu.astype(BF16), moe_wd.astype(BF16)
    hid = np.arange(GROUP) // HEAD_DIM
    bd = jnp.asarray((hid[:, None] == hid[None, :]).astype(np.float32) / HEAD_DIM)
    tri_l = jnp.asarray(np.tril(np.ones((MLSTM_L, MLSTM_L), np.float32)))
    tri_t = jnp.asarray(np.tril(np.ones((ATT_T, ATT_T), np.float32)))
    from_here = jnp.asarray(np.tril(np.ones((ATT_T, ATT_T), np.float32)), BF16)
    tc, ovt, tsb, twb, cb = _nsa_tables(rel_bias, s)

    mod = _adaln(c, ada_w, ada_b)
    for l in range(depth):
        proj, gt = _inproj(x, mod, l, n1g, w_p, wgt)
        y_a = _mlstm(proj, gt, l, mlstm_conv_w, bias_c, bias_r, m_out_g, tri_l, bd)
        y_b = _fox(proj, gt, l, bias_r, fq_g, fk_g, bd, tri_t)
        y_c = _sb(proj, from_here)
        a_k = proj[:, :, C_NKV:C_NKV + HEAD_DIM].reshape(nb, s // CMP_STRIDE, CMP_STRIDE * HEAD_DIM)
        a_v = proj[:, :, C_NKV + HEAD_DIM:C_NKV + 2 * HEAD_DIM].reshape(nb, s // CMP_STRIDE, CMP_STRIDE * HEAD_DIM)
        y_d = _nsa(proj, a_k, a_v, l, bias_c, nq_g, nk_g, pos8, nsa_cmp_wk, nsa_cmp_wv, bd, tc, ovt, tsb, twb, cb)
        if l % 2 == 0:
            x, h2 = _outproj(x, (y_a, y_b, y_c, y_d), mod, l, w_out_b, n2g, BF16)
            x = _ffn(h2, x, mod, l, l // 2, ffn_wg_b, ffn_wu_b, ffn_wd_b)
        else:
            x, h2 = _outproj(x, (y_a, y_b, y_c, y_d), mod, l, w_out_b, n2g, F32)
            x = _moe(h2, x, mod, l, l // 2, moe_router_w, moe_router_b, moe_wg_b, moe_wu_b, moe_wd_b)
    return x
```

```python
import functools
import math

import numpy as np
import jax
import jax.numpy as jnp
from jax import lax
from jax.experimental import pallas as pl
from jax.experimental.pallas import tpu as pltpu

F32 = jnp.float32
BF16 = jnp.bfloat16
HI = lax.Precision.HIGHEST

D_MODEL = 1024
HEADS = 4
HEAD_DIM = 64
GROUP = HEADS * HEAD_DIM
NORM_EPS = 1e-6
NEG = -1e30
QK_SCALE = HEAD_DIM ** -0.5
LOG2E = 1.4426950408889634
CMP_BLOCK = 32
CMP_STRIDE = 16
SEL_BLOCK = 64
N_SEL_TOP = 16
WINDOW = 512
NUM_BUCKETS = 32
MAX_DISTANCE = 128
N_EXPERTS = 8
VMEM_LIMIT = 56 * 1024 * 1024

PW = 3328
C_MQK, C_MV, C_MO = 0, 512, 768
C_FQ, C_FK, C_FV = 1024, 1280, 1536
C_SQ, C_SK, C_SV = 1792, 2048, 2304
C_NQ, C_NKV, C_GATES = 2560, 2816, 3200
G_MI, G_MF, G_FF, G_NG = 0, 4, 8, 12
GT_ROWS = 32

MLSTM_TS = 512
MLSTM_L = 128
ATT_T = 256
FOX_TQ = 512
FOX_HEAD_GROUP = 4
NSA_TQ = 256
NSA_NEAR = WINDOW + NSA_TQ
MOE_BM = 512
MOE_TF = 3584
FFN_TM = 512
FFN_TF = 2816


def _cp(sem, vmem=VMEM_LIMIT):
    return pltpu.CompilerParams(dimension_semantics=sem, vmem_limit_bytes=vmem)


def _dot(a, b):
    return jnp.dot(a.astype(BF16), b.astype(BF16), preferred_element_type=F32)


def _dot_t(a, b):
    return lax.dot_general(a.astype(BF16), b.astype(BF16), (((1,), (1,)), ((), ())), preferred_element_type=F32)


def _dot_hi(a, b):
    return jnp.dot(a, b, precision=HI, preferred_element_type=F32)


def _dot_t_hi(a, b):
    return lax.dot_general(a, b, (((1,), (1,)), ((), ())), precision=HI, preferred_element_type=F32)


def _sigmoid(x):
    return 1.0 / (1.0 + jnp.exp(-x))


def _log_sigmoid(x):
    return jnp.minimum(x, 0.0) - jnp.log1p(jnp.exp(-jnp.abs(x)))


def _rms_lanes(x, g):
    return x * lax.rsqrt(jnp.mean(x * x, axis=-1, keepdims=True) + NORM_EPS) * g


def _split2(a):
    hi = a.astype(BF16)
    return hi, (a - hi.astype(F32)).astype(BF16)


def _dot_2x(a, b):
    hi, lo = _split2(a)
    bb = b.astype(BF16)
    return jnp.dot(hi, bb, preferred_element_type=F32) + jnp.dot(lo, bb, preferred_element_type=F32)


def _dot_t_2x(a, b):
    hi, lo = _split2(a)
    bb = b.astype(BF16)
    dn = (((1,), (1,)), ((), ()))
    return (lax.dot_general(hi, bb, dn, preferred_element_type=F32)
            + lax.dot_general(lo, bb, dn, preferred_element_type=F32))


def _rms_heads(x, bd, g):
    return x * lax.rsqrt(_dot_2x(x * x, bd) + NORM_EPS) * g


def _adaln_kernel(c_ref, w_ref, b_ref, o_ref):
    c = c_ref[...]
    o_ref[...] = _dot_hi(c * _sigmoid(c), w_ref[...]) + b_ref[...]


def _adaln(c, ada_w, ada_b):
    depth, d, six_d = ada_w.shape
    nb = c.shape[0]
    out = pl.pallas_call(
        _adaln_kernel,
        out_shape=jax.ShapeDtypeStruct((depth, nb, six_d), F32),
        grid=(depth, six_d // d),
        in_specs=[pl.BlockSpec((nb, d), lambda l, j: (0, 0)),
                  pl.BlockSpec((None, d, d), lambda l, j: (l, 0, j)),
                  pl.BlockSpec((None, 1, d), lambda l, j: (l, 0, j))],
        out_specs=pl.BlockSpec((None, nb, d), lambda l, j: (l, 0, j)),
        compiler_params=_cp(("arbitrary", "arbitrary")),
        name="adaln",
    )(c, ada_w, ada_b.reshape(depth, 1, six_d))
    return out.reshape(depth, nb, 6, d)


def _inproj_kernel(x_ref, mod_ref, g_ref, w_ref, wgt_ref, o_ref, gt_ref):
    h = _rms_lanes(x_ref[...], g_ref[...]) * (1.0 + mod_ref[1:2, :]) + mod_ref[0:1, :]
    hb = h.astype(BF16)
    o_ref[...] = jnp.dot(hb, w_ref[...], preferred_element_type=F32)
    gt_ref[...] = lax.dot_general(wgt_ref[...], hb, (((1,), (1,)), ((), ())), preferred_element_type=F32)


def _inproj(x, mod, l, norm_g, w_p, wgt):
    nb, s, d = x.shape
    tm = 512
    return pl.pallas_call(
        _inproj_kernel,
        out_shape=(jax.ShapeDtypeStruct((nb, s, PW), F32), jax.ShapeDtypeStruct((nb, GT_ROWS, s), F32)),
        grid=(nb, s // tm),
        in_specs=[pl.BlockSpec((None, tm, d), lambda b, i: (b, i, 0)),
                  pl.BlockSpec((None, None, 6, d), lambda b, i: (l, b, 0, 0)),
                  pl.BlockSpec((None, 1, d), lambda b, i: (l, 0, 0)),
                  pl.BlockSpec((None, d, PW), lambda b, i: (l, 0, 0), pipeline_mode=pl.Buffered(1)),
                  pl.BlockSpec((None, GT_ROWS, d), lambda b, i: (l, 0, 0), pipeline_mode=pl.Buffered(1))],
        out_specs=(pl.BlockSpec((None, tm, PW), lambda b, i: (b, i, 0)),
                   pl.BlockSpec((None, GT_ROWS, tm), lambda b, i: (b, 0, i))),
        compiler_params=_cp(("arbitrary", "arbitrary")),
        name="inproj",
    )(x, mod, norm_g, w_p, wgt)


def _mlstm_kernel(qk_ref, v_ref, op_ref, gc_ref, gr_ref, cw_ref, bc_ref, br_ref, og_ref, tri_ref,
                  hm_ref, segs_ref, segq_ref, exp_ref, bd_ref, y_ref,
                  prev_ref, qs_ref, ks_ref, c_ref, n_ref, m_ref, *, ts, cl):
    @pl.when(pl.program_id(1) == 0)
    def _():
        prev_ref[...] = jnp.zeros_like(prev_ref)
        c_ref[...] = jnp.zeros_like(c_ref)
        n_ref[...] = jnp.zeros_like(n_ref)
        m_ref[...] = jnp.zeros_like(m_ref)

    x = qk_ref[...]
    xc = jnp.concatenate([prev_ref[...], x], axis=0)
    cw = cw_ref[...]
    y = (cw[0:1] * xc[8:8 + ts] + cw[1:2] * xc[7:7 + ts] + cw[2:3] * xc[6:6 + ts] + cw[3:4] * xc[5:5 + ts])
    prev_ref[...] = x[ts - 8:ts]
    y = y * _sigmoid(y)
    qs_ref[...] = y[:, :GROUP]
    ks_ref[...] = y[:, GROUP:] * QK_SCALE

    tri = tri_ref[...]
    lower = lax.broadcasted_iota(jnp.int32, (cl, cl), 0) >= lax.broadcasted_iota(jnp.int32, (cl, cl), 1)

    def chunk(ci, carry):
        r0 = pl.multiple_of(ci * cl, cl)
        gc = gc_ref[pl.ds(r0, cl), :] + bc_ref[...]
        gr = gr_ref[0:8, pl.ds(r0, cl)] + br_ref[0:8, :]
        lf_hi, lf_lo = _split2(_log_sigmoid(gc))
        trib = tri.astype(BF16)
        b_c = (jnp.dot(trib, lf_hi, preferred_element_type=F32)
               + jnp.dot(trib, lf_lo, preferred_element_type=F32))
        b_r = _dot_t_2x(_log_sigmoid(gr), tri)
        q = qs_ref[pl.ds(r0, cl), :]
        k = ks_ref[pl.ds(r0, cl), :]
        v = v_ref[pl.ds(r0, cl), :]
        kb = k.astype(BF16)
        hs = range(HEADS)
        ig_c = pltpu.roll(gc, G_MF - G_MI, 1)
        ig_r = pltpu.roll(gr, G_MF - G_MI, 0)
        u_c = ig_c - b_c
        u_r = ig_r - b_r
        g_row = b_c[cl - 1:cl, :]
        w_c = g_row + u_c
        m_loc = jnp.max(w_c, axis=0, keepdims=True)
        ew_c = jnp.exp(w_c - m_loc)
        m_prev = m_ref[...]
        pm = u_c
        row = lax.broadcasted_iota(jnp.int32, (cl, 128), 0)
        sh = 1
        while sh < cl:
            pm = jnp.where(row >= sh, jnp.maximum(pm, pltpu.roll(pm, sh, 0)), pm)
            sh *= 2
        mm_c = jnp.maximum(pm, m_prev)
        iw_c = jnp.exp(m_prev - mm_c)
        em_c = jnp.exp(-(b_c + mm_c))
        sb = []
        for h in hs:
            qm = jnp.where(hm_ref[h:h + 1, :] > 0.5, q, 0.0).astype(BF16)
            arg = jnp.where(lower, u_r[G_MF + h:G_MF + h + 1, :] - mm_c[:, G_MF + h:G_MF + h + 1], NEG)
            sb.append((_dot_t(qm, kb) * jnp.exp(arg)).astype(BF16))
        den_s = jnp.dot(jnp.concatenate(sb, axis=1), segs_ref[...], preferred_element_type=F32)
        pv = None
        for h in hs:
            vm = jnp.where(hm_ref[h:h + 1, :] > 0.5, v, 0.0).astype(BF16)
            t_h = jnp.dot(sb[h], vm, preferred_element_type=F32)
            pv = t_h if pv is None else pv + t_h
        c_prev = c_ref[...]
        n_prev = n_ref[...]
        inter = _dot_t(q, c_prev)
        qn_c = _dot_2x(q * n_prev, segq_ref[...])
        den_c = den_s + iw_c * qn_c
        lane = lax.broadcasted_iota(jnp.int32, (cl, 128), 1)
        r_c = jnp.where((lane >= G_MF) & (lane < G_MF + HEADS), 1.0 / jnp.maximum(jnp.abs(den_c), em_c), 0.0)
        expand = exp_ref[...]
        wide = _dot_2x(jnp.concatenate([ew_c, iw_c, r_c], axis=0), expand)
        ew_w, iw_w, r_w = wide[0:cl], wide[cl:2 * cl], wide[2 * cl:3 * cl]
        hh = (pv + iw_w * inter) * r_w
        hn = _rms_heads(hh, bd_ref[...], og_ref[...])
        y_ref[pl.ds(r0, cl), :] = (hn * _sigmoid(op_ref[pl.ds(r0, cl), :])).astype(y_ref.dtype)
        m_new = jnp.maximum(g_row + m_prev, m_loc)
        ab = jnp.concatenate([jnp.exp(g_row + m_prev - m_new), jnp.exp(m_loc - m_new),
                              jnp.zeros((6, 128), F32)], axis=0)
        ab_w = _dot_2x(ab, expand)
        c_full = lax.dot_general((v * ew_w).astype(BF16), kb, (((0,), (0,)), ((), ())), preferred_element_type=F32)
        c_ref[...] = ab_w[0:1] * c_prev + ab_w[1:2] * jnp.where(bd_ref[...] > 0.0, c_full, 0.0)
        n_ref[...] = ab_w[0:1] * n_prev + ab_w[1:2] * jnp.sum(ew_w * k, axis=0, keepdims=True)
        m_ref[...] = m_new
        return carry

    lax.fori_loop(0, ts // cl, chunk, 0, unroll=4)


def _mlstm_consts(cl):
    hid = np.arange(GROUP) // HEAD_DIM
    hm = np.zeros((8, GROUP), np.float32)
    hm[:HEADS] = (hid[None, :] == np.arange(HEADS)[:, None])
    segs = np.zeros((HEADS * cl, 128), np.float32)
    segs[np.arange(HEADS * cl), G_MF + np.arange(HEADS * cl) // cl] = 1.0
    segq = np.zeros((GROUP, 128), np.float32)
    segq[np.arange(GROUP), G_MF + hid] = 1.0
    return jnp.asarray(hm), jnp.asarray(segs, BF16), jnp.asarray(segq, BF16), jnp.asarray(segq.T, BF16)


def _mlstm(proj, gt, l, conv_w, bias_c, bias_r, out_g, tri, bd):
    nb, s, _ = proj.shape
    ts, cl = MLSTM_TS, MLSTM_L
    kern = functools.partial(_mlstm_kernel, ts=ts, cl=cl)
    hm, segs, segq, expand = _mlstm_consts(cl)
    const2 = lambda b, i: (0, 0)
    return pl.pallas_call(
        kern,
        out_shape=jax.ShapeDtypeStruct((nb, s, GROUP), BF16),
        grid=(nb, s // ts),
        in_specs=[pl.BlockSpec((None, ts, 2 * GROUP), lambda b, i: (b, i, C_MQK // (2 * GROUP))),
                  pl.BlockSpec((None, ts, GROUP), lambda b, i: (b, i, C_MV // GROUP)),
                  pl.BlockSpec((None, ts, GROUP), lambda b, i: (b, i, C_MO // GROUP)),
                  pl.BlockSpec((None, ts, 128), lambda b, i: (b, i, C_GATES // 128)),
                  pl.BlockSpec((None, GT_ROWS, ts), lambda b, i: (b, 0, i)),
                  pl.BlockSpec((None, 4, 2 * GROUP), lambda b, i: (l, 0, 0)),
                  pl.BlockSpec((None, 1, 128), lambda b, i: (l, 0, 0)),
                  pl.BlockSpec((None, GT_ROWS, 1), lambda b, i: (l, 0, 0)),
                  pl.BlockSpec((None, 1, GROUP), lambda b, i: (l, 0, 0)),
                  pl.BlockSpec((cl, cl), const2),
                  pl.BlockSpec((8, GROUP), const2),
                  pl.BlockSpec((HEADS * cl, 128), const2),
                  pl.BlockSpec((GROUP, 128), const2),
                  pl.BlockSpec((128, GROUP), const2),
                  pl.BlockSpec((GROUP, GROUP), const2)],
        out_specs=pl.BlockSpec((None, ts, GROUP), lambda b, i: (b, i, 0)),
        scratch_shapes=[pltpu.VMEM((8, 2 * GROUP), F32),
                        pltpu.VMEM((ts, GROUP), F32),
                        pltpu.VMEM((ts, GROUP), F32),
                        pltpu.VMEM((GROUP, GROUP), F32),
                        pltpu.VMEM((1, GROUP), F32),
                        pltpu.VMEM((1, 128), F32)],
        compiler_params=_cp(("arbitrary", "arbitrary")),
        name="mlstm",
    )(proj, proj, proj, proj, gt, conv_w, bias_c, bias_r, out_g, tri, hm, segs, segq, expand, bd)


def _fox_kernel(q_ref, k_ref, v_ref, gr_ref, br_ref, qg_ref, kg_ref, bd_ref, tri_ref, y_ref,
                kn_ref, vx_ref, fn_ref, m_ref, acc_ref, *, t, tq, s_len):
    qi = pl.program_id(1)
    bd = bd_ref[...]

    @pl.when(qi == 0)
    def _():
        kg = kg_ref[...]
        ones = jnp.ones((t, HEAD_DIM), BF16)
        for r in range(s_len // t):
            rows = slice(r * t, (r + 1) * t)
            kn = _rms_heads(k_ref[rows, :], bd, kg)
            vv = v_ref[rows, :]
            for h in range(HEADS):
                sl = slice(h * HEAD_DIM, (h + 1) * HEAD_DIM)
                kn_ref[h, rows, :] = kn[:, sl].astype(BF16)
                vx_ref[h, rows, :] = jnp.concatenate([vv[:, sl].astype(BF16), ones], axis=1)
        carry = jnp.zeros((8, 1), F32)
        for r in range(s_len // t):
            cols = slice(r * t, (r + 1) * t)
            lf = _log_sigmoid(gr_ref[G_FF:G_FF + 8, cols] + br_ref[G_FF:G_FF + 8, :])
            cs = _dot_t_hi(lf, tri_ref[...]) + carry
            fn_ref[:, cols] = -LOG2E * cs
            carry = cs[:, t - 1:t]

    qn = _rms_heads(q_ref[...], bd, qg_ref[...]) * (QK_SCALE * LOG2E)
    qh = [qn[:, h * HEAD_DIM:(h + 1) * HEAD_DIM].astype(BF16) for h in range(HEADS)]
    krel = lax.broadcasted_iota(jnp.int32, (tq, t), 1) - lax.broadcasted_iota(jnp.int32, (tq, t), 0)
    m_ref[...] = jnp.full(m_ref.shape, NEG, F32)
    acc_ref[...] = jnp.zeros(acc_ref.shape, F32)

    def tile(kb, hs, diag=None):
        k0 = pl.multiple_of(kb * t, t)
        sc = {h: _dot_t(qh[h], kn_ref[h, pl.ds(k0, t), :]) + fn_ref[h:h + 1, pl.ds(k0, t)] for h in hs}
        if diag is not None:
            causal = krel <= -diag * t
            sc = {h: jnp.where(causal, sc[h], NEG) for h in hs}
        m_old = {h: m_ref[h] for h in hs}
        m_new = {h: jnp.maximum(m_old[h], jnp.max(sc[h], axis=1, keepdims=True)) for h in hs}
        p = {h: jnp.exp2(sc[h] - jnp.concatenate([m_new[h]] * (t // 128), axis=1)).astype(BF16) for h in hs}
        pv = {h: jnp.dot(p[h], vx_ref[h, pl.ds(k0, t), :], preferred_element_type=F32) for h in hs}
        for h in hs:
            acc_ref[h] = jnp.exp2(m_old[h] - m_new[h]) * acc_ref[h] + pv[h]
            m_ref[h] = m_new[h]

    groups = [tuple(range(g, g + FOX_HEAD_GROUP)) for g in range(0, HEADS, FOX_HEAD_GROUP)]

    def body(kp, c):
        for hs in groups:
            tile(2 * kp, hs)
            tile(2 * kp + 1, hs)
        return c

    lax.fori_loop(0, qi, body, 0)
    for dg in range(tq // t):
        for hs in groups:
            tile(qi * (tq // t) + dg, hs, dg)
    outs = []
    for h in range(HEADS):
        a = acc_ref[h]
        outs.append(a[:, :HEAD_DIM] / a[:, HEAD_DIM:])
    y_ref[...] = jnp.concatenate(outs, axis=1).astype(y_ref.dtype)


def _fox(proj, gt, l, bias_r, q_g, k_g, bd, tri):
    nb, s, _ = proj.shape
    t = ATT_T
    tq = FOX_TQ
    kern = functools.partial(_fox_kernel, t=t, tq=tq, s_len=s)
    return pl.pallas_call(
        kern,
        out_shape=jax.ShapeDtypeStruct((nb, s, GROUP), BF16),
        grid=(nb, s // tq),
        in_specs=[pl.BlockSpec((None, tq, GROUP), lambda b, i: (b, i, C_FQ // GROUP)),
                  pl.BlockSpec((None, s, GROUP), lambda b, i: (b, 0, C_FK // GROUP)),
                  pl.BlockSpec((None, s, GROUP), lambda b, i: (b, 0, C_FV // GROUP)),
                  pl.BlockSpec((None, GT_ROWS, s), lambda b, i: (b, 0, 0)),
                  pl.BlockSpec((None, GT_ROWS, 1), lambda b, i: (l, 0, 0)),
                  pl.BlockSpec((None, 1, GROUP), lambda b, i: (l, 0, 0)),
                  pl.BlockSpec((None, 1, GROUP), lambda b, i: (l, 0, 0)),
                  pl.BlockSpec((GROUP, GROUP), lambda b, i: (0, 0)),
                  pl.BlockSpec((t, t), lambda b, i: (0, 0))],
        out_specs=pl.BlockSpec((None, tq, GROUP), lambda b, i: (b, i, 0)),
        scratch_shapes=[pltpu.VMEM((HEADS, s, HEAD_DIM), BF16),
                        pltpu.VMEM((HEADS, s, 2 * HEAD_DIM), BF16),
                        pltpu.VMEM((8, s), F32),
                        pltpu.VMEM((HEADS, tq, 128), F32),
                        pltpu.VMEM((HEADS, tq, 2 * HEAD_DIM), F32)],
        compiler_params=_cp(("arbitrary", "arbitrary")),
        name="fox",
    )(proj, proj, proj, gt, bias_r, q_g, k_g, bd, tri)


def _sb_kernel(q_ref, k_ref, v_ref, sm_ref, y_ref, kb_ref, vb_ref, rest_ref, acc_ref, *, t, tq, s_len):
    qi = pl.program_id(1)

    @pl.when(qi == 0)
    def _():
        for r in range(s_len // t):
            rows = slice(r * t, (r + 1) * t)
            kk = k_ref[rows, :]
            vv = v_ref[rows, :]
            for h in range(HEADS):
                sl = slice(h * HEAD_DIM, (h + 1) * HEAD_DIM)
                kb_ref[h, rows, :] = kk[:, sl].astype(BF16)
                vb_ref[h, rows, :] = vv[:, sl].astype(BF16)

    qs = q_ref[...] * (QK_SCALE * LOG2E)
    qh = [qs[:, h * HEAD_DIM:(h + 1) * HEAD_DIM].astype(BF16) for h in range(HEADS)]
    krel = lax.broadcasted_iota(jnp.int32, (tq, t), 1) - lax.broadcasted_iota(jnp.int32, (tq, t), 0)
    from_here = sm_ref[...]
    rest_ref[...] = jnp.zeros(rest_ref.shape, F32)
    acc_ref[...] = jnp.zeros(acc_ref.shape, F32)

    def step(kb, diag=None):
        k0 = pl.multiple_of(kb * t, t)
        hs = range(HEADS)
        masked = diag is not None
        if masked:
            strict = krel < -diag * t
        z = [_dot_t(qh[h], kb_ref[h, pl.ds(k0, t), :]) for h in hs]
        u = [jnp.maximum(z[h], 0.0) + jnp.log2(1.0 + jnp.exp2(-jnp.abs(z[h]))) for h in hs]
        if masked:
            u = [jnp.where(strict, v, 0.0) for v in u]
        incl = [jnp.dot(u[h].astype(BF16), from_here, preferred_element_type=F32) for h in hs]
        rest = [rest_ref[h] for h in hs]
        a = [jnp.exp2(z[h] - incl[h] - jnp.concatenate([rest[h]] * (t // 128), axis=1)) for h in hs]
        if masked:
            a = [jnp.where(strict, v, 0.0) for v in a]
        av = [jnp.dot(a[h].astype(BF16), vb_ref[h, pl.ds(k0, t), :], preferred_element_type=F32) for h in hs]
        for h in hs:
            acc_ref[h] += av[h]
            rest_ref[h] = rest[h] + incl[h][:, 0:1]

    nd = tq // t
    for dg in reversed(range(nd)):
        step(qi * nd + dg, dg)

    n_un = qi * nd

    unroll = 4

    def single(j, c):
        step(n_un - 1 - j)
        return c

    lax.fori_loop(0, n_un % unroll, single, 0)

    def body(j, c):
        kb = unroll * (n_un // unroll - 1 - j)
        for o in reversed(range(unroll)):
            step(kb + o)
        return c

    lax.fori_loop(0, n_un // unroll, body, 0)
    y_ref[...] = jnp.concatenate([acc_ref[h] for h in range(HEADS)], axis=1).astype(y_ref.dtype)


def _sb(proj, after):
    nb, s, _ = proj.shape
    t = ATT_T
    tq = ATT_T
    kern = functools.partial(_sb_kernel, t=t, tq=tq, s_len=s)
    return pl.pallas_call(
        kern,
        out_shape=jax.ShapeDtypeStruct((nb, s, GROUP), BF16),
        grid=(nb, s // tq),
        in_specs=[pl.BlockSpec((None, tq, GROUP), lambda b, i: (b, i, C_SQ // GROUP)),
                  pl.BlockSpec((None, s, GROUP), lambda b, i: (b, 0, C_SK // GROUP)),
                  pl.BlockSpec((None, s, GROUP), lambda b, i: (b, 0, C_SV // GROUP)),
                  pl.BlockSpec((t, t), lambda b, i: (0, 0))],
        out_specs=pl.BlockSpec((None, tq, GROUP), lambda b, i: (b, i, 0)),
        scratch_shapes=[pltpu.VMEM((HEADS, s, HEAD_DIM), BF16),
                        pltpu.VMEM((HEADS, s, HEAD_DIM), BF16),
                        pltpu.VMEM((HEADS, tq, 128), F32),
                        pltpu.VMEM((HEADS, tq, HEAD_DIM), F32)],
        compiler_params=_cp(("arbitrary", "arbitrary")),
        name="stick_breaking",
    )(proj, proj, proj, after)


def _nsa_kernel(q_ref, kvs_ref, kvw_ref, ak_ref, av_ref, gc_ref, gb_ref, qg_ref, kg_ref, pos_ref,
                wk_ref, wv_ref, bd_ref, tc_ref, ovt_ref, tsb_ref, twb_ref, cb_ref, y_ref,
                ksel_ref, vsel_ref, kwin_ref, vwin_ref, kcmp_ref, vcmp_ref, m_ref, acc_ref,
                *, tq, s_len, n_sel):
    qi = pl.program_id(1)
    nc = s_len // CMP_STRIDE

    @pl.when(qi == 0)
    def _():
        zk = jnp.zeros((WINDOW, HEAD_DIM), BF16)
        zv = jnp.zeros((WINDOW, 2 * HEAD_DIM), BF16)
        ksel_ref[0:WINDOW, :] = zk
        vsel_ref[0:WINDOW, :] = zv
        kwin_ref[0:WINDOW, :] = zk
        vwin_ref[0:WINDOW, :] = zv
        ones = jnp.ones((512, HEAD_DIM), BF16)
        for r in range(s_len // 512):
            rows = slice(r * 512, (r + 1) * 512)
            dst = slice(WINDOW + r * 512, WINDOW + (r + 1) * 512)
            kvs = kvs_ref[rows, :]
            kvw = kvw_ref[rows, :]
            ksel_ref[dst, :] = _rms_lanes(kvs[:, :HEAD_DIM], kg_ref[1:2, :]).astype(BF16)
            kwin_ref[dst, :] = _rms_lanes(kvw[:, :HEAD_DIM], kg_ref[2:3, :]).astype(BF16)
            vsel_ref[dst, :] = jnp.concatenate([kvs[:, HEAD_DIM:].astype(BF16), ones], axis=1)
            vwin_ref[dst, :] = jnp.concatenate([kvw[:, HEAD_DIM:].astype(BF16), ones], axis=1)
        half = CMP_STRIDE * HEAD_DIM
        for a_ref, w_ref, dst_ref, norm in ((ak_ref, wk_ref, kcmp_ref, True), (av_ref, wv_ref, vcmp_ref, False)):
            a = a_ref[...]
            pos_b = _dot_hi(pos_ref[...], w_ref[...])[0:1, :]
            p1 = _dot_hi(a, w_ref[0:half, :])
            p2 = _dot_hi(a, w_ref[half:2 * half, :])
            c = p1 + pltpu.roll(p2, nc - 1, 0) + pos_b
            if norm:
                hi, lo = _split2(_rms_lanes(c, kg_ref[0:1, :]))
                kcmp_ref[0] = hi
                kcmp_ref[1] = lo
            else:
                dst_ref[...] = c.astype(BF16)

    t0 = qi * tq
    qn = _rms_heads(q_ref[...], bd_ref[...], qg_ref[...]) * (QK_SCALE * LOG2E)
    qf = [qn[:, h * HEAD_DIM:(h + 1) * HEAD_DIM] for h in range(HEADS)]
    qh = [q.astype(BF16) for q in qf]
    ql = [(qf[h] - qh[h].astype(F32)).astype(BF16) for h in range(HEADS)]

    hs = range(HEADS)
    k_hi, k_lo = kcmp_ref[0], kcmp_ref[1]
    dn = (((1,), (1,)), ((), ()))
    sc = [lax.dot_general(qh[h], k_hi, dn, preferred_element_type=F32)
          + lax.dot_general(ql[h], k_hi, dn, preferred_element_type=F32)
          + lax.dot_general(qh[h], k_lo, dn, preferred_element_type=F32) + tc_ref[h] for h in hs]
    e = [jnp.exp2(sc[h] - jnp.max(sc[h], axis=1, keepdims=True)) for h in hs]
    p = [jnp.where(tc_ref[h] > 0.5 * NEG, e[h] / jnp.sum(e[h], axis=1, keepdims=True), 0.0) for h in hs]
    o_cmp = [jnp.dot(p[h].astype(BF16), vcmp_ref[...], preferred_element_type=F32) for h in hs]
    ps_hi, ps_lo = _split2(p[0] + p[1] + p[2] + p[3])
    ovt = ovt_ref[...].astype(BF16)
    imp_t = (lax.dot_general(ovt, ps_hi, dn, preferred_element_type=F32)
             + lax.dot_general(ovt, ps_lo, dn, preferred_element_type=F32))

    nr = -(-n_sel // 8) * 8
    jj = lax.broadcasted_iota(jnp.int32, (nr, tq), 0)
    tt = t0 + lax.broadcasted_iota(jnp.int32, (nr, tq), 1)
    cur = tt // SEL_BLOCK
    forced = (jj == 0) | (jj == cur) | (jj == cur - 1)
    score = jnp.where(forced, -NEG, jnp.where(jj * SEL_BLOCK <= tt, imp_t[:nr], NEG))
    cnt = jnp.zeros((nr, tq), F32)
    for i in range(n_sel):
        ri = score[i:i + 1, :]
        cnt = cnt + jnp.where(jj > i, (ri >= score).astype(F32), (ri > score).astype(F32))
    sel_t = (cnt < float(min(N_SEL_TOP, n_sel))).astype(F32)
    if nr < 128:
        sel_t = jnp.concatenate([sel_t, jnp.zeros((128 - nr, tq), F32)], axis=0)
    selb = sel_t.T.astype(BF16)

    m_ref[...] = jnp.full(m_ref.shape, NEG, F32)
    acc_ref[...] = jnp.zeros(acc_ref.shape, F32)
    jrow = lax.broadcasted_iota(jnp.int32, (128, tq), 0)
    jcol = lax.broadcasted_iota(jnp.int32, (128, tq), 1) // SEL_BLOCK

    def sel_chunk(kp0, near):
        jb = kp0 // SEL_BLOCK - WINDOW // SEL_BLOCK
        expand = (jrow == jb + jcol).astype(BF16)
        picked = jnp.dot(selb, expand, preferred_element_type=F32) > 0.5
        kblk = ksel_ref[pl.ds(kp0, tq), :]
        vblk = vsel_ref[pl.ds(kp0, tq), :]
        hs = range(HEADS)
        bias = [cb_ref[h] if near is None else tsb_ref[h, :, near * tq:(near + 1) * tq] for h in hs]
        sck = [_dot_t(qh[h], kblk) + jnp.where(picked, bias[h], NEG) for h in hs]
        m_old = [m_ref[h] for h in hs]
        m_new = [jnp.maximum(m_old[h], jnp.max(sck[h], axis=1, keepdims=True)) for h in hs]
        pk = [jnp.exp2(sck[h] - jnp.concatenate([m_new[h]] * (tq // 128), axis=1)).astype(BF16) for h in hs]
        pv = [jnp.dot(pk[h], vblk, preferred_element_type=F32) for h in hs]
        for h in hs:
            acc_ref[h] = jnp.exp2(m_old[h] - m_new[h]) * acc_ref[h] + pv[h]
            m_ref[h] = m_new[h]

    def far(cp, carry):
        sel_chunk(pl.multiple_of(WINDOW + 2 * cp * tq, tq), None)
        sel_chunk(pl.multiple_of(WINDOW + (2 * cp + 1) * tq, tq), None)
        return carry

    n_far = jnp.maximum(qi - WINDOW // tq, 0)
    lax.fori_loop(0, n_far // 2, far, 0)

    @pl.when(n_far % 2 == 1)
    def _():
        sel_chunk(pl.multiple_of(WINDOW + (n_far - 1) * tq, tq), None)
    for r in range(NSA_NEAR // tq):
        @pl.when(qi + r >= WINDOW // tq)
        def _(r=r):
            sel_chunk(pl.multiple_of(t0 + r * tq, tq), r)

    kw = kwin_ref[pl.ds(pl.multiple_of(t0, tq), NSA_NEAR), :]
    vw = vwin_ref[pl.ds(pl.multiple_of(t0, tq), NSA_NEAR), :]
    in_seq = lax.broadcasted_iota(jnp.int32, (tq, NSA_NEAR), 1) >= WINDOW - t0
    g = _sigmoid(gc_ref[...] + gb_ref[...])
    sw = [jnp.where(in_seq, _dot_t(qh[h], kw) + twb_ref[h], NEG) for h in hs]
    pw = [jnp.exp2(sw[h] - jnp.max(sw[h], axis=1, keepdims=True)).astype(BF16) for h in hs]
    ows = [jnp.dot(pw[h], vw, preferred_element_type=F32) for h in hs]
    outs = []
    for h in hs:
        ow = ows[h]
        o_win = ow[:, :HEAD_DIM] / ow[:, HEAD_DIM:]
        a = acc_ref[h]
        o_sel = a[:, :HEAD_DIM] / a[:, HEAD_DIM:]
        c0 = G_NG + 3 * h
        outs.append(g[:, c0:c0 + 1] * o_cmp[h] + g[:, c0 + 1:c0 + 2] * o_sel + g[:, c0 + 2:c0 + 3] * o_win)
    y_ref[...] = jnp.concatenate(outs, axis=1).astype(y_ref.dtype)


def _nsa(proj, a_k, a_v, l, bias_c, q_g, k_g, pos, wk, wv, bd, tc, ovt, tsb, twb, cb):
    nb, s, _ = proj.shape
    tq = NSA_TQ
    nc = s // CMP_STRIDE
    kern = functools.partial(_nsa_kernel, tq=tq, s_len=s, n_sel=s // SEL_BLOCK)
    const2 = lambda b, i: (0, 0)
    const3 = lambda b, i: (0, 0, 0)
    lay3 = lambda b, i: (l, 0, 0)
    return pl.pallas_call(
        kern,
        out_shape=jax.ShapeDtypeStruct((nb, s, GROUP), BF16),
        grid=(nb, s // tq),
        in_specs=[pl.BlockSpec((None, tq, GROUP), lambda b, i: (b, i, C_NQ // GROUP)),
                  pl.BlockSpec((None, s, 128), lambda b, i: (b, 0, C_NKV // 128 + 1)),
                  pl.BlockSpec((None, s, 128), lambda b, i: (b, 0, C_NKV // 128 + 2)),
                  pl.BlockSpec((None, nc, CMP_STRIDE * HEAD_DIM), lambda b, i: (b, 0, 0)),
                  pl.BlockSpec((None, nc, CMP_STRIDE * HEAD_DIM), lambda b, i: (b, 0, 0)),
                  pl.BlockSpec((None, tq, 128), lambda b, i: (b, i, C_GATES // 128)),
                  pl.BlockSpec((None, 1, 128), lay3),
                  pl.BlockSpec((None, 1, GROUP), lay3),
                  pl.BlockSpec((None, 8, HEAD_DIM), lay3),
                  pl.BlockSpec((None, 8, CMP_BLOCK * HEAD_DIM), lay3),
                  pl.BlockSpec((None, CMP_BLOCK * HEAD_DIM, HEAD_DIM), lay3),
                  pl.BlockSpec((None, CMP_BLOCK * HEAD_DIM, HEAD_DIM), lay3),
                  pl.BlockSpec((GROUP, GROUP), const2),
                  pl.BlockSpec((HEADS, tq, nc), lambda b, i: (0, i, 0)),
                  pl.BlockSpec((128, nc), const2),
                  pl.BlockSpec((HEADS, tq, NSA_NEAR), const3),
                  pl.BlockSpec((HEADS, tq, NSA_NEAR), const3),
                  pl.BlockSpec(memory_space=pltpu.SMEM)],
        out_specs=pl.BlockSpec((None, tq, GROUP), lambda b, i: (b, i, 0)),
        scratch_shapes=[pltpu.VMEM((s + WINDOW, HEAD_DIM), BF16),
                        pltpu.VMEM((s + WINDOW, 2 * HEAD_DIM), BF16),
                        pltpu.VMEM((s + WINDOW, HEAD_DIM), BF16),
                        pltpu.VMEM((s + WINDOW, 2 * HEAD_DIM), BF16),
                        pltpu.VMEM((2, nc, HEAD_DIM), BF16),
                        pltpu.VMEM((nc, HEAD_DIM), BF16),
                        pltpu.VMEM((HEADS, tq, 128), F32),
                        pltpu.VMEM((HEADS, tq, 2 * HEAD_DIM), F32)],
        compiler_params=_cp(("arbitrary", "arbitrary")),
        name="nsa",
    )(proj, proj, proj, a_k, a_v, proj, bias_c, q_g, k_g, pos, wk, wv, bd, tc, ovt, tsb, twb, cb)


def _outproj_kernel(x_ref, ya_ref, yb_ref, yc_ref, yd_ref, w_ref, mod_ref, g_ref, xo_ref, h_ref):
    acc = jnp.dot(ya_ref[...], w_ref[0:GROUP, :], preferred_element_type=F32)
    acc += jnp.dot(yb_ref[...], w_ref[GROUP:2 * GROUP, :], preferred_element_type=F32)
    acc += jnp.dot(yc_ref[...], w_ref[2 * GROUP:3 * GROUP, :], preferred_element_type=F32)
    acc += jnp.dot(yd_ref[...], w_ref[3 * GROUP:4 * GROUP, :], preferred_element_type=F32)
    xn = x_ref[...] + mod_ref[2:3, :] * acc
    xo_ref[...] = xn
    h = _rms_lanes(xn, g_ref[...]) * (1.0 + mod_ref[4:5, :]) + mod_ref[3:4, :]
    h_ref[...] = h.astype(h_ref.dtype)


def _outproj(x, ys, mod, l, w_out, norm_g, h_dtype):
    nb, s, d = x.shape
    tm = 512
    yspec = pl.BlockSpec((None, tm, GROUP), lambda b, i: (b, i, 0))
    xspec = pl.BlockSpec((None, tm, d), lambda b, i: (b, i, 0))
    return pl.pallas_call(
        _outproj_kernel,
        out_shape=(jax.ShapeDtypeStruct((nb, s, d), F32), jax.ShapeDtypeStruct((nb, s, d), h_dtype)),
        grid=(nb, s // tm),
        in_specs=[xspec, yspec, yspec, yspec, yspec,
                  pl.BlockSpec((None, d, d), lambda b, i: (l, 0, 0)),
                  pl.BlockSpec((None, None, 6, d), lambda b, i: (l, b, 0, 0)),
                  pl.BlockSpec((None, 1, d), lambda b, i: (l, 0, 0))],
        out_specs=(xspec, xspec),
        compiler_params=_cp(("arbitrary", "arbitrary")),
        name="outproj",
    )(x, *ys, w_out, mod, norm_g)


def _ffn_kernel(h_ref, x_ref, mod_ref, wg_ref, wu_ref, wd_ref, o_ref, acc_ref, *, nf):
    f = pl.program_id(2)

    @pl.when(f == 0)
    def _():
        acc_ref[...] = jnp.zeros_like(acc_ref)

    h = h_ref[...]
    a = jnp.dot(h, wg_ref[...], preferred_element_type=F32)
    u = jnp.dot(h, wu_ref[...], preferred_element_type=F32)
    act = (a * _sigmoid(a) * u).astype(BF16)
    acc_ref[...] += jnp.dot(act, wd_ref[...], preferred_element_type=F32)

    @pl.when(f == nf - 1)
    def _():
        o_ref[...] = x_ref[...] + mod_ref[5:6, :] * acc_ref[...]


def _ffn(h, x, mod, l, li, wg, wu, wd):
    nb, s, d = x.shape
    dff = wg.shape[-1]
    tm, tf = FFN_TM, FFN_TF
    nf = dff // tf
    wmode = pl.Buffered(1) if nf == 1 else None
    xspec = pl.BlockSpec((None, tm, d), lambda b, i, f: (b, i, 0))
    return pl.pallas_call(
        functools.partial(_ffn_kernel, nf=nf),
        out_shape=jax.ShapeDtypeStruct((nb, s, d), F32),
        grid=(nb, s // tm, nf),
        in_specs=[xspec, xspec,
                  pl.BlockSpec((None, None, 6, d), lambda b, i, f: (l, b, 0, 0)),
                  pl.BlockSpec((None, d, tf), lambda b, i, f: (li, 0, f), pipeline_mode=wmode),
                  pl.BlockSpec((None, d, tf), lambda b, i, f: (li, 0, f), pipeline_mode=wmode),
                  pl.BlockSpec((None, tf, d), lambda b, i, f: (li, f, 0), pipeline_mode=wmode)],
        out_specs=xspec,
        scratch_shapes=[pltpu.VMEM((tm, d), F32)],
        compiler_params=_cp(("arbitrary", "arbitrary", "arbitrary")),
        name="ffn_dense",
    )(h, x, mod, wg, wu, wd)


def _router_kernel(h_ref, w_ref, b_ref, o_ref):
    h_hi, h_lo = _split2(h_ref[...])
    w_hi, w_lo = _split2(w_ref[...])
    logits = (jnp.dot(h_hi, w_hi, preferred_element_type=F32) + jnp.dot(h_lo, w_hi, preferred_element_type=F32)
              + jnp.dot(h_hi, w_lo, preferred_element_type=F32)) + b_ref[...]
    lane = lax.broadcasted_iota(jnp.int32, logits.shape, 1).astype(F32)
    lg = jnp.where(lane < N_EXPERTS, logits, -3e38)
    m1 = jnp.max(lg, axis=1, keepdims=True)
    i1 = jnp.min(jnp.where(lg == m1, lane, 128.0), axis=1, keepdims=True)
    lg2 = jnp.where(lane == i1, -3e38, lg)
    m2 = jnp.max(lg2, axis=1, keepdims=True)
    i2 = jnp.min(jnp.where(lg2 == m2, lane, 128.0), axis=1, keepdims=True)
    e2 = jnp.exp(m2 - m1)
    w1 = 1.0 / (1.0 + e2)
    w2 = e2 / (1.0 + e2)
    o_ref[...] = jnp.where(lane == 0, i1, jnp.where(lane == 1, i2, jnp.where(lane == 2, w1, jnp.where(lane == 3, w2, 0.0))))


def _router(hf, rw, rb):
    t, d = hf.shape
    tm = 1024
    return pl.pallas_call(
        _router_kernel,
        out_shape=jax.ShapeDtypeStruct((t, 128), F32),
        grid=(t // tm,),
        in_specs=[pl.BlockSpec((tm, d), lambda i: (i, 0)),
                  pl.BlockSpec((d, 128), lambda i: (0, 0)),
                  pl.BlockSpec((1, 128), lambda i: (0, 0))],
        out_specs=pl.BlockSpec((tm, 128), lambda i: (i, 0)),
        compiler_params=_cp(("arbitrary",)),
        name="moe_router",
    )(hf, rw, rb)


def _gather_copy(src_hbm, row, dst_buf, slot, r, sem):
    return pltpu.make_async_copy(src_hbm.at[pl.ds(row, 1)], dst_buf.at[slot, pl.ds(r, 1)], sem.at[slot])


def _experts_kernel(be_ref, tok_ref, nv_ref, h_hbm, wg_ref, wu_ref, wd_ref, y_ref,
                    xbuf, xb_ref, acc_ref, sem, *, bm, nf):
    del be_ref
    i = pl.program_id(0)
    f = pl.program_id(1)
    nvalid = nv_ref[0]
    slot = i % 2
    chunk = bm // nf

    def for_rows(fn):
        def body(r, c):
            fn(r)
            return c
        lax.fori_loop(0, bm, body, 0, unroll=8)

    @pl.when((f == 0) & (i == 0))
    def _():
        for_rows(lambda r: _gather_copy(h_hbm, tok_ref[r], xbuf, 0, r, sem).start())

    @pl.when((f == 0) & (i < nvalid))
    def _():
        for_rows(lambda r: _gather_copy(h_hbm, 0, xbuf, slot, r, sem).wait())
        xb_ref[...] = xbuf[slot].astype(BF16)
        acc_ref[...] = jnp.zeros_like(acc_ref)

    def compute(prefetch):
        third = -(-chunk // 3)

        def issue_rows(lo, hi):
            if prefetch:
                for j in range(lo, min(hi, chunk)):
                    r = f * chunk + j
                    _gather_copy(h_hbm, tok_ref[(i + 1) * bm + r], xbuf, 1 - slot, r, sem).start()

        x = xb_ref[...]
        issue_rows(0, third)
        a = jnp.dot(x, wg_ref[...], preferred_element_type=F32)
        issue_rows(third, 2 * third)
        u = jnp.dot(x, wu_ref[...], preferred_element_type=F32)
        issue_rows(2 * third, 3 * third)
        act = (a * _sigmoid(a) * u).astype(BF16)
        acc_ref[...] += jnp.dot(act, wd_ref[...], preferred_element_type=F32)

    @pl.when(i + 1 < nvalid)
    def _():
        compute(True)

    @pl.when(i + 1 == nvalid)
    def _():
        compute(False)

    @pl.when(f == nf - 1)
    def _():
        @pl.when(i < nvalid)
        def _():
            y_ref[...] = acc_ref[...]

        @pl.when(i >= nvalid)
        def _():
            y_ref[...] = jnp.zeros_like(y_ref)


def _experts(hf, blk_expert, row_tok, nvalid, wg, wu, wd, li):
    t, d = hf.shape
    bm, tf = MOE_BM, MOE_TF
    n_rows = row_tok.shape[0]
    n_blocks = n_rows // bm
    dff = wg.shape[-1]
    nf = dff // tf

    wmode = pl.Buffered(1) if nf == 1 else None

    def fidx(i, f, nv):
        return jnp.where(i < nv[0], f, nf - 1)

    grid_spec = pltpu.PrefetchScalarGridSpec(
        num_scalar_prefetch=3,
        grid=(n_blocks, nf),
        in_specs=[pl.BlockSpec(memory_space=pl.ANY),
                  pl.BlockSpec((None, None, d, tf), lambda i, f, be, tok, nv: (li, be[i], 0, fidx(i, f, nv)),
                               pipeline_mode=wmode),
                  pl.BlockSpec((None, None, d, tf), lambda i, f, be, tok, nv: (li, be[i], 0, fidx(i, f, nv)),
                               pipeline_mode=wmode),
                  pl.BlockSpec((None, None, tf, d), lambda i, f, be, tok, nv: (li, be[i], fidx(i, f, nv), 0),
                               pipeline_mode=wmode)],
        out_specs=pl.BlockSpec((bm, d), lambda i, f, be, tok, nv: (i, 0)),
        scratch_shapes=[pltpu.VMEM((2, bm, d), F32),
                        pltpu.VMEM((bm, d), BF16),
                        pltpu.VMEM((bm, d), F32),
                        pltpu.SemaphoreType.DMA((2,))],
    )
    return pl.pallas_call(
        functools.partial(_experts_kernel, bm=bm, nf=nf),
        out_shape=jax.ShapeDtypeStruct((n_rows, d), F32),
        grid_spec=grid_spec,
        compiler_params=_cp(("arbitrary", "arbitrary")),
        name="moe_experts",
    )(blk_expert, row_tok, nvalid, hf, wg, wu, wd)


def _combine_kernel(pos_ref, y_hbm, x_ref, mod_ref, rt_ref, o_ref, ybuf, sem, *, tm, s_len):
    g = pl.program_id(0) * pl.num_programs(1) + pl.program_id(1)
    n_tiles = pl.num_programs(0) * pl.num_programs(1)
    slot = g % 2

    def copy(tile, sl, k, r):
        row = pos_ref[(tile * tm + r) * 2 + k]
        return pltpu.make_async_copy(y_hbm.at[pl.ds(row, 1)], ybuf.at[2 * sl + k, pl.ds(r, 1)], sem.at[sl])

    def issue(tile, sl):
        def body(r, c):
            copy(tile, sl, 0, r).start()
            copy(tile, sl, 1, r).start()
            return c
        lax.fori_loop(0, tm, body, 0, unroll=8)

    @pl.when(g == 0)
    def _():
        issue(0, 0)

    @pl.when(g + 1 < n_tiles)
    def _():
        issue(g + 1, 1 - slot)

    def wait(r, c):
        copy(g, slot, 0, r).wait()
        copy(g, slot, 1, r).wait()
        return c

    lax.fori_loop(0, tm, wait, 0, unroll=8)
    rt = rt_ref[...]
    o_ref[...] = x_ref[...] + mod_ref[5:6, :] * (rt[:, 2:3] * ybuf[2 * slot] + rt[:, 3:4] * ybuf[2 * slot + 1])


def _combine(y, pos, x, mod, route, l):
    nb, s, d = x.shape
    tm = 256
    grid_spec = pltpu.PrefetchScalarGridSpec(
        num_scalar_prefetch=1,
        grid=(nb, s // tm),
        in_specs=[pl.BlockSpec(memory_space=pl.ANY),
                  pl.BlockSpec((None, tm, d), lambda b, i, p: (b, i, 0)),
                  pl.BlockSpec((None, None, 6, d), lambda b, i, p: (l, b, 0, 0)),
                  pl.BlockSpec((None, tm, 128), lambda b, i, p: (b, i, 0))],
        out_specs=pl.BlockSpec((None, tm, d), lambda b, i, p: (b, i, 0)),
        scratch_shapes=[pltpu.VMEM((4, tm, d), F32), pltpu.SemaphoreType.DMA((2,))],
    )
    return pl.pallas_call(
        functools.partial(_combine_kernel, tm=tm, s_len=s),
        out_shape=jax.ShapeDtypeStruct((nb, s, d), F32),
        grid_spec=grid_spec,
        compiler_params=_cp(("arbitrary", "arbitrary")),
        name="moe_combine",
    )(pos, y, x, mod, route.reshape(nb, s, 128))


def _moe(hf32, x, mod, l, li, router_w, router_b, wg, wu, wd):
    nb, s, d = x.shape
    t = nb * s
    bm = MOE_BM
    hf = hf32.reshape(t, d)
    rw = jnp.zeros((d, 128), F32).at[:, :N_EXPERTS].set(router_w[li])
    rb = jnp.zeros((1, 128), F32).at[0, :N_EXPERTS].set(router_b[li])
    route = _router(hf, rw, rb)
    e_flat = route[:, 0:2].astype(jnp.int32).reshape(-1)
    n_assign = 2 * t
    onehot = (e_flat[:, None] == jnp.arange(N_EXPERTS, dtype=jnp.int32)[None, :]).astype(jnp.int32)
    csum = jnp.cumsum(onehot, axis=0)
    rank = jnp.sum(onehot * csum, axis=1) - 1
    counts = csum[-1]
    padded = (counts + bm - 1) // bm * bm
    cum_padded = jnp.cumsum(padded)
    pstart = cum_padded - padded
    dest = pstart[e_flat] + rank
    n_blocks = n_assign // bm + N_EXPERTS
    n_rows = n_blocks * bm
    tok_flat = jnp.arange(n_assign, dtype=jnp.int32) // 2
    row_tok = jnp.zeros((n_rows,), jnp.int32).at[dest].set(tok_flat)
    blk_start = jnp.arange(n_blocks, dtype=jnp.int32) * bm
    blk_expert = jnp.minimum(jnp.searchsorted(cum_padded, blk_start, side='right'), N_EXPERTS - 1).astype(jnp.int32)
    nvalid = (cum_padded[-1] // bm).astype(jnp.int32).reshape(1)
    y = _experts(hf, blk_expert, row_tok, nvalid, wg, wu, wd, li)
    return _combine(y, dest.astype(jnp.int32), x, mod, route, l)


def _t5_bucket(dist):
    n = np.maximum(dist, 0)
    max_exact = NUM_BUCKETS // 2
    large = max_exact + (np.log(np.maximum(n, 1).astype(np.float32) / max_exact)
                         / math.log(MAX_DISTANCE / max_exact) * (NUM_BUCKETS - max_exact)).astype(np.int32)
    return np.where(n < max_exact, n, np.minimum(large, NUM_BUCKETS - 1)).astype(np.int32)


def _bias_by_bucket(rb, bucket):
    ids = jnp.asarray(bucket.astype(np.int8))[None]
    out = jnp.zeros((rb.shape[1],) + bucket.shape, F32)
    for k in range(NUM_BUCKETS):
        out = jnp.where(ids == k, rb[k][:, None, None], out)
    return out


def _nsa_tables(rel_bias, s):
    tq = NSA_TQ
    nc = s // CMP_STRIDE
    n_cmp = (s - CMP_BLOCK) // CMP_STRIDE + 1
    rb = rel_bias.astype(F32) * LOG2E
    t = np.arange(s)[:, None]
    cmp_end = np.arange(nc)[None, :] * CMP_STRIDE + CMP_BLOCK - 1
    ok = (cmp_end <= t) & (np.arange(nc)[None, :] < n_cmp)
    tc = jnp.where(jnp.asarray(ok)[None], _bias_by_bucket(rb, _t5_bucket(t - cmp_end)), NEG)
    dist = np.arange(tq)[:, None] - (np.arange(NSA_NEAR)[None, :] - WINDOW)
    tb = _bias_by_bucket(rb, _t5_bucket(dist))
    tsb = jnp.where(jnp.asarray(dist >= 0)[None], tb, NEG)
    twb = jnp.where(jnp.asarray((dist >= 0) & (dist < WINDOW))[None], tb, NEG)
    far_bucket = int(_t5_bucket(np.array([WINDOW]))[0])
    cb = rb[far_bucket]
    n_sel = s // SEL_BLOCK
    cs = np.arange(nc)[None, :] * CMP_STRIDE
    ss = np.arange(128)[:, None] * SEL_BLOCK
    ov = np.clip(np.minimum(cs + CMP_BLOCK, ss + SEL_BLOCK) - np.maximum(cs, ss), 0, None).astype(np.float32) / CMP_BLOCK
    ov = ov * (np.arange(128)[:, None] < n_sel) * (np.arange(nc)[None, :] < n_cmp)
    return tc, jnp.asarray(ov, F32), tsb, twb, cb


def _prep_layer_params(w_in, mlstm_gate_b, fox_f_b, nsa_gate_b):
    depth, d, _ = w_in.shape
    small = [w_in[:, :, 1024:1032], w_in[:, :, 1800:1804], w_in[:, :, 3212:3224]]
    w_p = jnp.concatenate([w_in[:, :, 0:1024], w_in[:, :, 1032:1800], w_in[:, :, 1804:2572], w_in[:, :, 2572:3212]]
                          + small + [jnp.zeros((depth, d, 128 - 24), w_in.dtype)], axis=-1).astype(BF16)
    wgt = jnp.concatenate(small + [jnp.zeros((depth, d, GT_ROWS - 24), w_in.dtype)], axis=-1)
    wgt = jnp.transpose(wgt, (0, 2, 1)).astype(BF16)
    gate_b = jnp.concatenate([mlstm_gate_b, fox_f_b, nsa_gate_b], axis=-1).astype(F32)
    bias_c = jnp.zeros((depth, 1, 128), F32).at[:, 0, :24].set(gate_b)
    bias_r = jnp.zeros((depth, GT_ROWS, 1), F32).at[:, :24, 0].set(gate_b)
    return w_p, wgt, bias_c, bias_r


def kernel(x, c, rel_bias, ada_w, ada_b, norm1_g, norm2_g, w_in, w_out, mlstm_conv_w, mlstm_gate_b, mlstm_out_g,
           fox_f_b, fox_q_g, fox_k_g, nsa_q_g, nsa_k_g, nsa_cmp_pos, nsa_cmp_wk, nsa_cmp_wv, nsa_gate_b,
           ffn_wg, ffn_wu, ffn_wd, moe_router_w, moe_router_b, moe_wg, moe_wu, moe_wd):
    nb, s, d = x.shape
    depth = w_in.shape[0]
    assert d == D_MODEL and s % 512 == 0 and s // SEL_BLOCK <= 128

    w_p, wgt, bias_c, bias_r = _prep_layer_params(w_in, mlstm_gate_b, fox_f_b, nsa_gate_b)
    w_out_b = w_out.astype(BF16)
    n1g = norm1_g.reshape(depth, 1, d)
    n2g = norm2_g.reshape(depth, 1, d)
    m_out_g = mlstm_out_g.reshape(depth, 1, GROUP)
    fq_g = jnp.tile(fox_q_g, (1, HEADS)).reshape(depth, 1, GROUP)
    fk_g = jnp.tile(fox_k_g, (1, HEADS)).reshape(depth, 1, GROUP)
    nq_g = jnp.tile(nsa_q_g, (1, HEADS)).reshape(depth, 1, GROUP)
    nk_g = jnp.zeros((depth, 8, HEAD_DIM), F32).at[:, :3].set(nsa_k_g)
    pos8 = jnp.zeros((depth, 8, CMP_BLOCK * HEAD_DIM), F32).at[:, 0].set(nsa_cmp_pos.reshape(depth, -1))
    ffn_wg_b, ffn_wu_b, ffn_wd_b = ffn_wg.astype(BF16), ffn_wu.astype(BF16), ffn_wd.astype(BF16)
    moe_wg_b, moe_wu_b, moe_wd_b = moe_wg.astype(BF16), moe_wu.astype(BF16), moe_wd.astype(BF16)
    hid = np.arange(GROUP) // HEAD_DIM
    bd = jnp.asarray((hid[:, None] == hid[None, :]).astype(np.float32) / HEAD_DIM)
    tri_l = jnp.asarray(np.tril(np.ones((MLSTM_L, MLSTM_L), np.float32)))
    tri_t = jnp.asarray(np.tril(np.ones((ATT_T, ATT_T), np.float32)))
    from_here = jnp.asarray(np.tril(np.ones((ATT_T, ATT_T), np.float32)), BF16)
    tc, ovt, tsb, twb, cb = _nsa_tables(rel_bias, s)

    mod = _adaln(c, ada_w, ada_b)
    for l in range(depth):
        proj, gt = _inproj(x, mod, l, n1g, w_p, wgt)
        y_a = _mlstm(proj, gt, l, mlstm_conv_w, bias_c, bias_r, m_out_g, tri_l, bd)
        y_b = _fox(proj, gt, l, bias_r, fq_g, fk_g, bd, tri_t)
        y_c = _sb(proj, from_here)
        a_k = proj[:, :, C_NKV:C_NKV + HEAD_DIM].reshape(nb, s // CMP_STRIDE, CMP_STRIDE * HEAD_DIM)
        a_v = proj[:, :, C_NKV + HEAD_DIM:C_NKV + 2 * HEAD_DIM].reshape(nb, s // CMP_STRIDE, CMP_STRIDE * HEAD_DIM)
        y_d = _nsa(proj, a_k, a_v, l, bias_c, nq_g, nk_g, pos8, nsa_cmp_wk, nsa_cmp_wv, bd, tc, ovt, tsb, twb, cb)
        if l % 2 == 0:
            x, h2 = _outproj(x, (y_a, y_b, y_c, y_d), mod, l, w_out_b, n2g, BF16)
            x = _ffn(h2, x, mod, l, l // 2, ffn_wg_b, ffn_wu_b, ffn_wd_b)
        else:
            x, h2 = _outproj(x, (y_a, y_b, y_c, y_d), mod, l, w_out_b, n2g, F32)
            x = _moe(h2, x, mod, l, l // 2, moe_router_w, moe_router_b, moe_wg_b, moe_wu_b, moe_wd_b)
    return x
```
